```python
import jax
import jax.numpy as jnp
from jax import lax
import numpy as np

D_MODEL = 1024
BATCH = 2
SEQ = 8192
DEPTH = 1

EPS = 1e-6
ROPE_THETA = 10000.0
Q_BLOCK = 128

A_HEADS = 8
A_HEAD_DIM = 64
A_WIDTH = A_HEADS * A_HEAD_DIM
MOBA_BLOCK = 256
MOBA_TOPK = 3

B_HEADS = 8
B_NOPE = 64
B_ROPE = 32
B_QK = B_NOPE + B_ROPE
B_V = 64
B_Q_RANK = 384
B_KV_RANK = 256
B_WIDTH = B_HEADS * B_V

D_FF = 4 * D_MODEL

IN_SPLITS = (A_WIDTH, A_WIDTH, A_WIDTH, B_Q_RANK, B_KV_RANK, B_ROPE, D_MODEL, D_MODEL)
IN_WIDTH = sum(IN_SPLITS)

kernel_name = "hybrid_gated_moba_mla_block"


def rmsnorm(x, g):
    xf = x.astype(jnp.float32)
    y = xf * lax.rsqrt(jnp.mean(xf * xf, axis=-1, keepdims=True) + EPS)
    return (y * g.astype(jnp.float32)).astype(x.dtype)


def rope(x, pos):
    half = x.shape[-1] // 2
    inv_freq = jnp.power(ROPE_THETA, -jnp.arange(half, dtype=jnp.float32) / half)
    ang = pos.astype(jnp.float32)[:, None] * inv_freq[None, :]
    cos = jnp.cos(ang)[None, :, None, :]
    sin = jnp.sin(ang)[None, :, None, :]
    xf = x.astype(jnp.float32)
    x1, x2 = xf[..., :half], xf[..., half:]
    return jnp.concatenate([x1 * cos - x2 * sin, x2 * cos + x1 * sin], axis=-1).astype(x.dtype)


def moba_attention(q, k, v):
    b, h, s, dh = q.shape
    nb = -(-s // MOBA_BLOCK)
    pad = nb * MOBA_BLOCK - s
    padw = ((0, 0), (0, 0), (0, pad), (0, 0))
    kb = jnp.pad(k, padw).reshape(b, h, nb, MOBA_BLOCK, dh)
    vb = jnp.pad(v, padw).reshape(b, h, nb, MOBA_BLOCK, dh)
    k_mean = jnp.mean(kb.astype(jnp.float32), axis=3)
    topk = min(MOBA_TOPK, nb)
    scale = dh ** -0.5
    bi = jnp.arange(b)[:, None, None, None]
    hi = jnp.arange(h)[None, :, None, None]
    blk_ids = jnp.arange(nb)
    in_blk = jnp.arange(MOBA_BLOCK)

    def one_chunk(c):
        start = c * Q_BLOCK
        qc = lax.dynamic_slice_in_dim(q, start, Q_BLOCK, axis=2)
        qpos = start + jnp.arange(Q_BLOCK)
        qblk = start // MOBA_BLOCK
        gate = jnp.einsum('bhtd,bhnd->bhtn', qc.astype(jnp.float32), k_mean)
        gate = jnp.where(blk_ids < qblk, gate, -jnp.inf)
        _, sel = lax.top_k(gate, topk)
        valid = sel < qblk
        k_sel = kb[bi, hi, sel]
        v_sel = vb[bi, hi, sel]
        k_own = lax.dynamic_index_in_dim(kb, qblk, axis=2, keepdims=False)
        v_own = lax.dynamic_index_in_dim(vb, qblk, axis=2, keepdims=False)
        own_mask = (qblk * MOBA_BLOCK + in_blk)[None, :] <= qpos[:, None]
        s_sel = jnp.einsum('bhtd,bhtkld->bhtkl', qc, k_sel,
                           preferred_element_type=jnp.float32) * scale
        s_sel = jnp.where(valid[..., None], s_sel, -jnp.inf)
        s_own = jnp.einsum('bhtd,bhld->bhtl', qc, k_own,
                           preferred_element_type=jnp.float32) * scale
        s_own = jnp.where(own_mask, s_own, -jnp.inf)
        scores = jnp.concatenate(
            [s_sel.reshape(b, h, Q_BLOCK, topk * MOBA_BLOCK), s_own], axis=-1)
        p = jax.nn.softmax(scores, axis=-1).astype(v.dtype)
        p_sel = p[..., :topk * MOBA_BLOCK].reshape(b, h, Q_BLOCK, topk, MOBA_BLOCK)
        p_own = p[..., topk * MOBA_BLOCK:]
        return (jnp.einsum('bhtkl,bhtkld->bhtd', p_sel, v_sel)
                + jnp.einsum('bhtl,bhld->bhtd', p_own, v_own))

    out = lax.map(one_chunk, jnp.arange(s // Q_BLOCK))
    return jnp.moveaxis(out, 0, 2).reshape(b, h, s, dh)


def mla_attention(q_nope, q_rope, k_nope, k_rope, v):
    b, h, s, _ = q_nope.shape
    scale = B_QK ** -0.5
    kpos = jnp.arange(s)

    def one_chunk(c):
        start = c * Q_BLOCK
        qn = lax.dynamic_slice_in_dim(q_nope, start, Q_BLOCK, axis=2)
        qr = lax.dynamic_slice_in_dim(q_rope, start, Q_BLOCK, axis=2)
        qpos = start + jnp.arange(Q_BLOCK)
        sc = (jnp.einsum('bhtd,bhsd->bhts', qn, k_nope, preferred_element_type=jnp.float32)
              + jnp.einsum('bhtd,bsd->bhts', qr, k_rope, preferred_element_type=jnp.float32)) * scale
        sc = jnp.where(kpos[None, :] <= qpos[:, None], sc, -jnp.inf)
        p = jax.nn.softmax(sc, axis=-1).astype(v.dtype)
        return jnp.einsum('bhts,bhsd->bhtd', p, v)

    out = lax.map(one_chunk, jnp.arange(s // Q_BLOCK))
    return jnp.moveaxis(out, 0, 2).reshape(b, h, s, v.shape[-1])


def setup_inputs(seed: int = 0) -> dict:
    key = jax.random.key(seed)
    ks = jax.random.split(key, 15)

    def dense(k, fan_in, shape):
        return jax.random.normal(k, shape, jnp.float32) * fan_in ** -0.5

    def gain(k, shape):
        return 1.0 + 0.02 * jax.random.normal(k, shape, jnp.float32)

    return {
        "x": jax.random.normal(ks[0], (BATCH, SEQ, D_MODEL), jnp.float32),
        "g_mix": gain(ks[1], (DEPTH, D_MODEL)),
        "w_in": dense(ks[2], D_MODEL, (DEPTH, D_MODEL, IN_WIDTH)),
        "b_gate": 0.02 * jax.random.normal(ks[3], (DEPTH, 2, D_MODEL), jnp.float32),
        "g_cq": gain(ks[4], (DEPTH, B_Q_RANK)),
        "w_q_up": dense(ks[5], B_Q_RANK, (DEPTH, B_Q_RANK, B_HEADS * B_QK)),
        "g_ckv": gain(ks[6], (DEPTH, B_KV_RANK)),
        "w_kv_up": dense(ks[7], B_KV_RANK, (DEPTH, B_KV_RANK, B_HEADS * (B_NOPE + B_V))),
        "w_o_a": dense(ks[8], A_WIDTH, (DEPTH, A_WIDTH, D_MODEL)),
        "w_o_b": dense(ks[9], B_WIDTH, (DEPTH, B_WIDTH, D_MODEL)),
        "w_out": dense(ks[10], D_MODEL, (DEPTH, D_MODEL, D_MODEL)),
        "g_mlp": gain(ks[11], (DEPTH, D_MODEL)),
        "w_ff1": dense(ks[12], D_MODEL, (DEPTH, D_MODEL, D_FF)),
        "w_ff2": dense(ks[13], D_FF, (DEPTH, D_FF, D_MODEL)),
        "g_final": gain(ks[14], (D_MODEL,)),
    }


def reference(x, g_mix, w_in, b_gate, g_cq, w_q_up, g_ckv, w_kv_up, w_o_a, w_o_b,
              w_out, g_mlp, w_ff1, w_ff2, g_final):
    b, s, _ = x.shape
    pos = jnp.arange(s)
    offsets = [int(o) for o in np.cumsum(IN_SPLITS)[:-1]]
    for l in range(DEPTH):
        h = rmsnorm(x, g_mix[l])
        proj = jnp.einsum('bsd,de->bse', h, w_in[l])
        qa, ka, va, cq, ckv, kr, gate_a, gate_b = jnp.split(proj, offsets, axis=-1)

        qa = rope(qa.reshape(b, s, A_HEADS, A_HEAD_DIM), pos).transpose(0, 2, 1, 3)
        ka = rope(ka.reshape(b, s, A_HEADS, A_HEAD_DIM), pos).transpose(0, 2, 1, 3)
        va = va.reshape(b, s, A_HEADS, A_HEAD_DIM).transpose(0, 2, 1, 3)
        o_a = moba_attention(qa, ka, va).transpose(0, 2, 1, 3).reshape(b, s, A_WIDTH)

        qb = jnp.einsum('bsr,re->bse', rmsnorm(cq, g_cq[l]), w_q_up[l]).reshape(b, s, B_HEADS, B_QK)
        q_nope = qb[..., :B_NOPE].transpose(0, 2, 1, 3)
        q_rope = rope(qb[..., B_NOPE:], pos).transpose(0, 2, 1, 3)
        kv = jnp.einsum('bsr,re->bse', rmsnorm(ckv, g_ckv[l]), w_kv_up[l]).reshape(
            b, s, B_HEADS, B_NOPE + B_V)
        k_nope = kv[..., :B_NOPE].transpose(0, 2, 1, 3)
        v_b = kv[..., B_NOPE:].transpose(0, 2, 1, 3)
        k_rope = rope(kr[:, :, None, :], pos)[:, :, 0, :]
        o_b = mla_attention(q_nope, q_rope, k_nope, k_rope, v_b).transpose(0, 2, 1, 3).reshape(
            b, s, B_WIDTH)

        y_a = jnp.einsum('bse,ed->bsd', o_a, w_o_a[l])
        y_b = jnp.einsum('bse,ed->bsd', o_b, w_o_b[l])
        g_a = jax.nn.sigmoid(gate_a + b_gate[l, 0])
        g_b = jax.nn.sigmoid(gate_b + b_gate[l, 1])
        x = x + jnp.einsum('bsd,de->bse', g_a * y_a + g_b * y_b, w_out[l])

        h2 = rmsnorm(x, g_mlp[l])
        u = jax.nn.relu(jnp.einsum('bsd,df->bsf', h2, w_ff1[l]))
        x = x + jnp.einsum('bsf,fd->bsd', u * u, w_ff2[l])
    return rmsnorm(x, g_final)
```

```python
import functools

import jax
import jax.numpy as jnp
from jax import lax
from jax.experimental import pallas as pl
from jax.experimental.pallas import tpu as pltpu

EPS = 1e-6
ROPE_THETA = 10000.0

A_HEADS = 8
A_HEAD_DIM = 64
MOBA_BLOCK = 256
MOBA_TOPK = 3

B_HEADS = 8
B_NOPE = 64
B_ROPE = 32
B_QK = B_NOPE + B_ROPE
B_V = 64
B_Q_RANK = 384
B_KV_RANK = 256

LANES = 128
HEAD_SLOT = 128
ROW_TILE = 512
Q_TILE = MOBA_BLOCK
K_TILE = MOBA_BLOCK
HEADS_PER_STEP = 4
MASK_VALUE = -1e30
VMEM_LIMIT_BYTES = 56 * 1024 * 1024

F32 = jnp.float32
BF16 = jnp.bfloat16


def _rms(x, g):
    return x * lax.rsqrt(jnp.mean(x * x, axis=-1, keepdims=True) + EPS) * g


def _swap_halves(x, lane, half):
    n = x.shape[-1]
    fwd = pltpu.roll(x, n - half, axis=1)
    bwd = pltpu.roll(x, half, axis=1)
    return jnp.where((lane % (2 * half)) < half, fwd, bwd)


def _rope_lanes(x, cos, sin_signed, half):
    lane = lax.broadcasted_iota(jnp.int32, (x.shape[0], LANES), 1)
    groups = []
    for g in range(x.shape[1] // LANES):
        xg = x[:, g * LANES:(g + 1) * LANES]
        groups.append(xg * cos + _swap_halves(xg, lane, half) * sin_signed)
    return groups


def _store_transposed(dst_ref, row0, t):
    rows = t.shape[0]
    for qt in range(ROW_TILE // Q_TILE):
        dst_ref[qt, row0:row0 + rows, :] = t[:, qt * Q_TILE:(qt + 1) * Q_TILE].astype(dst_ref.dtype)


def _pre_kernel(x_ref, gmix_ref, cos_a_ref, sin_a_ref, cos_b_ref, sin_b_ref,
                w_qkv_ref, w_cq_ref, w_ckv_ref, w_kr_ref, w_gate_ref, b_gate_ref,
                g_cq_ref, w_qup_ref, g_ckv_ref, w_kup_ref, w_vup_ref,
                qa_t_ref, ka_ref, va_t_ref, kmean_ref, gate_ref, qb_t_ref, kb_ref, vb_t_ref):
    a_width = A_HEADS * A_HEAD_DIM
    h = _rms(x_ref[...], gmix_ref[...]).astype(BF16)

    qkv = jnp.dot(h, w_qkv_ref[...], preferred_element_type=F32)
    cos_a, sin_a = cos_a_ref[...], sin_a_ref[...]
    q_groups = _rope_lanes(qkv[:, :a_width], cos_a, sin_a, A_HEAD_DIM // 2)
    k_groups = _rope_lanes(qkv[:, a_width:2 * a_width], cos_a, sin_a, A_HEAD_DIM // 2)
    sub = lax.broadcasted_iota(jnp.int32, (LANES, ROW_TILE), 0)
    for g, qg in enumerate(q_groups):
        t = (qg * (A_HEAD_DIM ** -0.5)).T
        _store_transposed(qa_t_ref, (2 * g) * HEAD_SLOT, jnp.where(sub < A_HEAD_DIM, t, 0.0))
        _store_transposed(qa_t_ref, (2 * g + 1) * HEAD_SLOT, jnp.where(sub >= A_HEAD_DIM, t, 0.0))
    for g, kg in enumerate(k_groups):
        ka_ref[:, g * LANES:(g + 1) * LANES] = kg.astype(BF16)
        for blk in range(ROW_TILE // MOBA_BLOCK):
            kmean_ref[0, blk:blk + 1, g * LANES:(g + 1) * LANES] = jnp.mean(
                kg[blk * MOBA_BLOCK:(blk + 1) * MOBA_BLOCK], axis=0, keepdims=True)
    for g in range(a_width // LANES):
        vg = qkv[:, 2 * a_width + g * LANES:2 * a_width + (g + 1) * LANES]
        _store_transposed(va_t_ref, g * LANES, vg.T)

    gate = jnp.dot(h, w_gate_ref[...], preferred_element_type=F32) + b_gate_ref[...]
    gate_ref[...] = jax.nn.sigmoid(gate).astype(BF16)

    cos_b, sin_b = cos_b_ref[...], sin_b_ref[...]
    cq = jnp.dot(h, w_cq_ref[...], preferred_element_type=F32)
    cqn = _rms(cq, g_cq_ref[...]).astype(BF16)
    qb = jnp.dot(cqn, w_qup_ref[...], preferred_element_type=F32)
    for hd, qg in enumerate(_rope_lanes(qb, cos_b, sin_b, B_ROPE // 2)):
        _store_transposed(qb_t_ref, hd * HEAD_SLOT, (qg * (B_QK ** -0.5)).T)
    ckv = jnp.dot(h, w_ckv_ref[...], preferred_element_type=F32)
    ckvn = _rms(ckv, g_ckv_ref[...]).astype(BF16)
    k_nope = jnp.dot(ckvn, w_kup_ref[...], preferred_element_type=F32)
    kr = jnp.dot(h, w_kr_ref[...], preferred_element_type=F32)
    kr = _rope_lanes(kr, cos_b, sin_b, B_ROPE // 2)[0]
    for hd in range(B_HEADS):
        kb_ref[:, hd * HEAD_SLOT:(hd + 1) * HEAD_SLOT] = (
            k_nope[:, hd * HEAD_SLOT:(hd + 1) * HEAD_SLOT] + kr).astype(BF16)
    vb = jnp.dot(ckvn, w_vup_ref[...], preferred_element_type=F32)
    for g in range(B_HEADS * B_V // LANES):
        _store_transposed(vb_t_ref, g * LANES, vb[:, g * LANES:(g + 1) * LANES].T)


def _attn_kernel(*refs, moba):
    if moba:
        q_t_ref, k_ref, v_t_ref, kmean_ref, o_t_ref, m_scr, l_scr, acc_scr, bias_scr = refs
    else:
        q_t_ref, k_ref, v_t_ref, o_t_ref, m_scr, l_scr, acc_scr = refs
    i = pl.program_id(2)
    heads = HEADS_PER_STEP
    vdim = v_t_ref.shape[1] // heads

    def key_lanes(hd):
        g = hd // 2 if moba else hd
        return slice(g * LANES, (g + 1) * LANES)

    def q_rows(hd):
        return slice(hd * HEAD_SLOT, (hd + 1) * HEAD_SLOT)

    if moba:
        nblk = kmean_ref.shape[0]
        blk = lax.broadcasted_iota(jnp.int32, (nblk, Q_TILE), 0).astype(F32)
        past = blk < i.astype(F32)
        for hd in range(heads):
            gate = jnp.dot(kmean_ref[:, key_lanes(hd)].astype(BF16), q_t_ref[q_rows(hd), :],
                           preferred_element_type=F32)
            gate = jnp.where(past, gate, -jnp.inf)
            bias = jnp.full(gate.shape, MASK_VALUE, F32)
            for _ in range(MOBA_TOPK):
                best = jnp.max(gate, axis=0, keepdims=True)
                first = jnp.min(jnp.where(gate == best, blk, float(nblk)), axis=0, keepdims=True)
                pick = blk == first
                bias = jnp.where(pick, 0.0, bias)
                gate = jnp.where(pick, -jnp.inf, gate)
            bias_scr[hd] = jnp.where(past, bias, MASK_VALUE)

    def step(j, diagonal):
        start = pl.multiple_of(j * K_TILE, K_TILE)
        k_blk = k_ref[pl.ds(start, K_TILE), :]
        v_t = v_t_ref[j]
        for hd in range(heads):
            s = jnp.dot(k_blk[:, key_lanes(hd)], q_t_ref[q_rows(hd), :],
                        preferred_element_type=F32)
            if diagonal:
                key_pos = lax.broadcasted_iota(jnp.int32, s.shape, 0)
                qry_pos = lax.broadcasted_iota(jnp.int32, s.shape, 1)
                s = jnp.where(key_pos <= qry_pos, s, MASK_VALUE)
                m_new = jnp.max(s, axis=0, keepdims=True)
                p = jnp.exp(s - m_new)
                l_scr[hd] = jnp.sum(p, axis=0, keepdims=True)
                acc_scr[hd] = jnp.dot(v_t[hd * vdim:(hd + 1) * vdim], p.astype(BF16),
                                      preferred_element_type=F32)
            else:
                if moba:
                    s = s + bias_scr[hd, pl.ds(j, 1), :]
                m_old = m_scr[hd]
                m_new = jnp.maximum(m_old, jnp.max(s, axis=0, keepdims=True))
                alpha = jnp.exp(m_old - m_new)
                p = jnp.exp(s - m_new)
                l_scr[hd] = alpha * l_scr[hd] + jnp.sum(p, axis=0, keepdims=True)
                acc_scr[hd] = alpha * acc_scr[hd] + jnp.dot(
                    v_t[hd * vdim:(hd + 1) * vdim], p.astype(BF16), preferred_element_type=F32)
            m_scr[hd] = m_new

    step(i, diagonal=True)

    def body(j, carry):
        step(j, diagonal=False)
        return carry

    lax.fori_loop(0, i, body, 0)

    for hd in range(heads):
        o_t_ref[hd * vdim:(hd + 1) * vdim, :] = (acc_scr[hd] / l_scr[hd]).astype(o_t_ref.dtype)


def _post_kernel(x_ref, oa_t_ref, ob_t_ref, gate_ref, w_oa_ref, w_ob_ref, w_out_ref,
                 g_mlp_ref, w_ff1_ref, w_ff2_ref, g_final_ref, out_ref, *, ff_chunk, final_norm):
    d_model = x_ref.shape[1]
    contract_rows = (((0,), (0,)), ((), ()))
    ya, yb = [], []
    for qt in range(ROW_TILE // Q_TILE):
        ya.append(lax.dot_general(oa_t_ref[qt], w_oa_ref[...], contract_rows, preferred_element_type=F32))
        yb.append(lax.dot_general(ob_t_ref[qt], w_ob_ref[...], contract_rows, preferred_element_type=F32))
    ya = jnp.concatenate(ya, axis=0)
    yb = jnp.concatenate(yb, axis=0)
    mixed = gate_ref[:, :d_model].astype(F32) * ya + gate_ref[:, d_model:].astype(F32) * yb
    x1 = x_ref[...] + jnp.dot(mixed.astype(BF16), w_out_ref[...], preferred_element_type=F32)

    h2 = _rms(x1, g_mlp_ref[...]).astype(BF16)
    mlp = jnp.zeros_like(x1)
    for c in range(w_ff1_ref.shape[1] // ff_chunk):
        u = jnp.maximum(jnp.dot(h2, w_ff1_ref[:, c * ff_chunk:(c + 1) * ff_chunk],
                                preferred_element_type=F32), 0.0)
        mlp = mlp + jnp.dot((u * u).astype(BF16), w_ff2_ref[c * ff_chunk:(c + 1) * ff_chunk, :],
                            preferred_element_type=F32)
    x2 = x1 + mlp
    out_ref[...] = _rms(x2, g_final_ref[...]) if final_norm else x2


def _resident(shape):
    return pl.BlockSpec(shape, lambda *_: (0,) * len(shape), pipeline_mode=pl.Buffered(1))


def _rope_tables(seq, dim):
    half = dim // 2
    inv_freq = jnp.power(ROPE_THETA, -jnp.arange(half, dtype=F32) / half)
    ang = jnp.arange(seq, dtype=F32)[:, None] * inv_freq[None, :]
    cos = jnp.concatenate([jnp.cos(ang), jnp.cos(ang)], axis=1)
    sin = jnp.concatenate([-jnp.sin(ang), jnp.sin(ang)], axis=1)
    return cos, sin


def _params(*semantics):
    return pltpu.CompilerParams(dimension_semantics=semantics, vmem_limit_bytes=VMEM_LIMIT_BYTES)


def _pre_call(x2, seq, g_mix, w_in, b_gate, g_cq, w_q_up, g_ckv, w_kv_up):
    rows, d_model = x2.shape
    a_width = A_HEADS * A_HEAD_DIM
    o_q, o_k, o_v = 0, a_width, 2 * a_width
    o_cq = 3 * a_width
    o_ckv = o_cq + B_Q_RANK
    o_kr = o_ckv + B_KV_RANK
    o_gate = o_kr + B_ROPE
    w_in = w_in.astype(BF16)
    w_qkv = w_in[:, o_q:o_cq]
    w_cq = w_in[:, o_cq:o_ckv]
    w_ckv = w_in[:, o_ckv:o_kr]
    w_kr = jnp.pad(w_in[:, o_kr:o_gate], ((0, 0), (B_NOPE, HEAD_SLOT - B_QK)))
    w_gate = w_in[:, o_gate:]
    w_qup = jnp.pad(w_q_up.astype(BF16).reshape(B_Q_RANK, B_HEADS, B_QK),
                    ((0, 0), (0, 0), (0, HEAD_SLOT - B_QK))).reshape(B_Q_RANK, B_HEADS * HEAD_SLOT)
    w_kv = w_kv_up.astype(BF16).reshape(B_KV_RANK, B_HEADS, B_NOPE + B_V)
    w_kup = jnp.pad(w_kv[:, :, :B_NOPE], ((0, 0), (0, 0), (0, HEAD_SLOT - B_NOPE))).reshape(
        B_KV_RANK, B_HEADS * HEAD_SLOT)
    w_vup = w_kv[:, :, B_NOPE:].reshape(B_KV_RANK, B_HEADS * B_V)

    cos_a, sin_a = _rope_tables(seq, A_HEAD_DIM)
    cos_a, sin_a = jnp.tile(cos_a, (1, LANES // A_HEAD_DIM)), jnp.tile(sin_a, (1, LANES // A_HEAD_DIM))
    cos_b, sin_b = _rope_tables(seq, B_ROPE)
    pad = ((0, 0), (B_NOPE, HEAD_SLOT - B_QK))
    cos_b = jnp.pad(cos_b, pad, constant_values=1.0)
    sin_b = jnp.pad(sin_b, pad)

    n_tiles = rows // ROW_TILE
    seq_tiles = seq // ROW_TILE
    q_per_tile = ROW_TILE // Q_TILE
    n_q = rows // Q_TILE
    row_spec = lambda w: pl.BlockSpec((ROW_TILE, w), lambda r: (r, 0))
    table_spec = pl.BlockSpec((ROW_TILE, LANES), lambda r: (r % seq_tiles, 0))
    t_spec = lambda h: pl.BlockSpec((q_per_tile, h, Q_TILE), lambda r: (r, 0, 0))
    vec = lambda a: a.reshape(1, -1).astype(F32)
    operands = [
        (x2, row_spec(d_model)), (vec(g_mix), _resident((1, d_model))),
        (cos_a, table_spec), (sin_a, table_spec), (cos_b, table_spec), (sin_b, table_spec),
        (w_qkv, _resident(w_qkv.shape)), (w_cq, _resident(w_cq.shape)), (w_ckv, _resident(w_ckv.shape)),
        (w_kr, _resident(w_kr.shape)), (w_gate, _resident(w_gate.shape)),
        (vec(b_gate), _resident((1, 2 * d_model))),
        (vec(g_cq), _resident((1, B_Q_RANK))), (w_qup, _resident(w_qup.shape)),
        (vec(g_ckv), _resident((1, B_KV_RANK))), (w_kup, _resident(w_kup.shape)),
        (w_vup, _resident(w_vup.shape)),
    ]
    out_shape = [
        jax.ShapeDtypeStruct((n_q, A_HEADS * HEAD_SLOT, Q_TILE), BF16),
        jax.ShapeDtypeStruct((rows, a_width), BF16),
        jax.ShapeDtypeStruct((n_q, a_width, Q_TILE), BF16),
        jax.ShapeDtypeStruct((n_tiles, ROW_TILE // MOBA_BLOCK, a_width), F32),
        jax.ShapeDtypeStruct((rows, 2 * d_model), BF16),
        jax.ShapeDtypeStruct((n_q, B_HEADS * HEAD_SLOT, Q_TILE), BF16),
        jax.ShapeDtypeStruct((rows, B_HEADS * HEAD_SLOT), BF16),
        jax.ShapeDtypeStruct((n_q, B_HEADS * B_V, Q_TILE), BF16),
    ]
    out_specs = [
        t_spec(A_HEADS * HEAD_SLOT), row_spec(a_width), t_spec(a_width),
        pl.BlockSpec((1, ROW_TILE // MOBA_BLOCK, a_width), lambda r: (r, 0, 0)),
        row_spec(2 * d_model), t_spec(B_HEADS * HEAD_SLOT), row_spec(B_HEADS * HEAD_SLOT),
        t_spec(B_HEADS * B_V),
    ]
    return pl.pallas_call(
        _pre_kernel,
        grid=(n_tiles,),
        in_specs=[s for _, s in operands],
        out_specs=out_specs,
        out_shape=out_shape,
        compiler_params=_params("parallel"),
        name="pre_proj",
    )(*[a for a, _ in operands])


def _attn_call(q_t, k, v_t, kmean, *, batch, seq, moba):
    nblk = seq // Q_TILE
    heads = q_t.shape[1] // HEAD_SLOT
    groups = heads // HEADS_PER_STEP
    vdim = v_t.shape[2] // heads
    key_lanes = k.shape[2] // groups
    in_specs = [
        pl.BlockSpec((None, HEADS_PER_STEP * HEAD_SLOT, Q_TILE), lambda b, g, i: (b * nblk + i, g, 0)),
        pl.BlockSpec((None, seq, key_lanes), lambda b, g, i: (b, 0, g)),
        pl.BlockSpec((None, nblk, HEADS_PER_STEP * vdim, K_TILE), lambda b, g, i: (b, 0, g, 0)),
    ]
    operands = [q_t, k, v_t]
    scratch = [
        pltpu.VMEM((HEADS_PER_STEP, 1, Q_TILE), F32),
        pltpu.VMEM((HEADS_PER_STEP, 1, Q_TILE), F32),
        pltpu.VMEM((HEADS_PER_STEP, vdim, Q_TILE), F32),
    ]
    if moba:
        in_specs.append(pl.BlockSpec((None, nblk, key_lanes), lambda b, g, i: (b, 0, g)))
        operands.append(kmean)
        scratch.append(pltpu.VMEM((HEADS_PER_STEP, nblk, Q_TILE), F32))
    return pl.pallas_call(
        functools.partial(_attn_kernel, moba=moba),
        grid=(batch, groups, nblk),
        in_specs=in_specs,
        out_specs=pl.BlockSpec((None, HEADS_PER_STEP * vdim, Q_TILE), lambda b, g, i: (b * nblk + i, g, 0)),
        out_shape=jax.ShapeDtypeStruct((batch * nblk, heads * vdim, Q_TILE), BF16),
        scratch_shapes=scratch,
        compiler_params=_params("parallel", "parallel", "arbitrary"),
        name="moba_attn" if moba else "mla_attn",
    )(*operands)


def _post_call(x2, oa_t, ob_t, gates, w_o_a, w_o_b, w_out, g_mlp, w_ff1, w_ff2, g_final, final_norm):
    rows, d_model = x2.shape
    q_per_tile = ROW_TILE // Q_TILE
    row_spec = lambda w: pl.BlockSpec((ROW_TILE, w), lambda r: (r, 0))
    t_spec = lambda a: pl.BlockSpec((q_per_tile,) + a.shape[1:], lambda r: (r, 0, 0))
    vec = lambda a: a.reshape(1, -1).astype(F32)
    weights = [w.astype(BF16) for w in (w_o_a, w_o_b, w_out)]
    operands = [(x2, row_spec(d_model)), (oa_t, t_spec(oa_t)), (ob_t, t_spec(ob_t)),
                (gates, row_spec(2 * d_model))]
    operands += [(w, _resident(w.shape)) for w in weights]
    operands += [(vec(g_mlp), _resident((1, d_model))),
                 (w_ff1.astype(BF16), _resident(w_ff1.shape)), (w_ff2.astype(BF16), _resident(w_ff2.shape)),
                 (vec(g_final), _resident((1, d_model)))]
    return pl.pallas_call(
        functools.partial(_post_kernel, ff_chunk=1024, final_norm=final_norm),
        grid=(rows // ROW_TILE,),
        in_specs=[s for _, s in operands],
        out_specs=row_spec(d_model),
        out_shape=jax.ShapeDtypeStruct((rows, d_model), F32),
        compiler_params=_params("parallel"),
        name="post_mlp",
    )(*[a for a, _ in operands])


def kernel(x, g_mix, w_in, b_gate, g_cq, w_q_up, g_ckv, w_kv_up, w_o_a, w_o_b, w_out, g_mlp, w_ff1,
           w_ff2, g_final):
    batch, seq, d_model = x.shape
    depth = w_in.shape[0]
    assert seq % ROW_TILE == 0 and ROW_TILE % MOBA_BLOCK == 0
    assert A_HEADS % HEADS_PER_STEP == 0 and B_HEADS % HEADS_PER_STEP == 0 and HEADS_PER_STEP % 2 == 0
    nblk = seq // Q_TILE
    x2 = x.reshape(batch * seq, d_model)
    for l in range(depth):
        qa_t, ka, va_t, kmean, gates, qb_t, kb, vb_t = _pre_call(
            x2, seq, g_mix[l], w_in[l], b_gate[l], g_cq[l], w_q_up[l], g_ckv[l], w_kv_up[l])
        oa_t = _attn_call(qa_t, ka.reshape(batch, seq, -1), va_t.reshape(batch, nblk, -1, K_TILE),
                          kmean.reshape(batch, nblk, -1), batch=batch, seq=seq, moba=True)
        ob_t = _attn_call(qb_t, kb.reshape(batch, seq, -1), vb_t.reshape(batch, nblk, -1, K_TILE),
                          None, batch=batch, seq=seq, moba=False)
        x2 = _post_call(x2, oa_t, ob_t, gates, w_o_a[l], w_o_b[l], w_out[l], g_mlp[l], w_ff1[l],
                        w_ff2[l], g_final, final_norm=(l == depth - 1))
    return x2.reshape(batch, seq, d_model)
```

```python
import functools

import jax
import jax.numpy as jnp
from jax import lax
from jax.experimental import pallas as pl
from jax.experimental.pallas import tpu as pltpu

EPS = 1e-6
ROPE_THETA = 10000.0

A_HEADS = 8
A_HEAD_DIM = 64
MOBA_BLOCK = 256
MOBA_TOPK = 3

B_HEADS = 8
B_NOPE = 64
B_ROPE = 32
B_QK = B_NOPE + B_ROPE
B_V = 64
B_Q_RANK = 384
B_KV_RANK = 256

LANES = 128
HEAD_SLOT = 128
ROW_TILE = 512
Q_TILE = MOBA_BLOCK
K_TILE = MOBA_BLOCK
HEADS_PER_STEP = 4
MASK_VALUE = -1e30
VMEM_LIMIT_BYTES = 56 * 1024 * 1024

F32 = jnp.float32
BF16 = jnp.bfloat16


def _rms(x, g):
    return x * lax.rsqrt(jnp.mean(x * x, axis=-1, keepdims=True) + EPS) * g


def _swap_halves(x, lane, half):
    n = x.shape[-1]
    fwd = pltpu.roll(x, n - half, axis=1)
    bwd = pltpu.roll(x, half, axis=1)
    return jnp.where((lane % (2 * half)) < half, fwd, bwd)


def _rope_lanes(x, cos, sin_signed, half):
    lane = lax.broadcasted_iota(jnp.int32, (x.shape[0], LANES), 1)
    groups = []
    for g in range(x.shape[1] // LANES):
        xg = x[:, g * LANES:(g + 1) * LANES]
        groups.append(xg * cos + _swap_halves(xg, lane, half) * sin_signed)
    return groups


def _store_transposed(dst_ref, row0, t):
    rows = t.shape[0]
    for qt in range(ROW_TILE // Q_TILE):
        dst_ref[qt, row0:row0 + rows, :] = t[:, qt * Q_TILE:(qt + 1) * Q_TILE].astype(dst_ref.dtype)


def _pre_kernel(x_ref, gmix_ref, cos_a_ref, sin_a_ref, cos_b_ref, sin_b_ref,
                w_qkv_ref, w_cq_ref, w_ckv_ref, w_kr_ref, w_gate_ref, b_gate_ref,
                g_cq_ref, w_qup_ref, g_ckv_ref, w_kup_ref, w_vup_ref,
                qa_t_ref, ka_ref, va_t_ref, kmean_ref, gate_ref, qb_t_ref, kb_ref, vb_t_ref):
    a_width = A_HEADS * A_HEAD_DIM
    h = _rms(x_ref[...], gmix_ref[...]).astype(BF16)

    qkv = jnp.dot(h, w_qkv_ref[...], preferred_element_type=F32)
    cos_a, sin_a = cos_a_ref[...], sin_a_ref[...]
    q_groups = _rope_lanes(qkv[:, :a_width], cos_a, sin_a, A_HEAD_DIM // 2)
    k_groups = _rope_lanes(qkv[:, a_width:2 * a_width], cos_a, sin_a, A_HEAD_DIM // 2)
    sub = lax.broadcasted_iota(jnp.int32, (LANES, ROW_TILE), 0)
    for g, qg in enumerate(q_groups):
        t = (qg * (A_HEAD_DIM ** -0.5)).T
        _store_transposed(qa_t_ref, (2 * g) * HEAD_SLOT, jnp.where(sub < A_HEAD_DIM, t, 0.0))
        _store_transposed(qa_t_ref, (2 * g + 1) * HEAD_SLOT, jnp.where(sub >= A_HEAD_DIM, t, 0.0))
    for g, kg in enumerate(k_groups):
        ka_ref[:, g * LANES:(g + 1) * LANES] = kg.astype(BF16)
        for blk in range(ROW_TILE // MOBA_BLOCK):
            kmean_ref[0, blk:blk + 1, g * LANES:(g + 1) * LANES] = jnp.mean(
                kg[blk * MOBA_BLOCK:(blk + 1) * MOBA_BLOCK], axis=0, keepdims=True)
    for g in range(a_width // LANES):
        vg = qkv[:, 2 * a_width + g * LANES:2 * a_width + (g + 1) * LANES]
        _store_transposed(va_t_ref, g * LANES, vg.T)

    gate = jnp.dot(h, w_gate_ref[...], preferred_element_type=F32) + b_gate_ref[...]
    gate_ref[...] = jax.nn.sigmoid(gate).astype(BF16)

    cos_b, sin_b = cos_b_ref[...], sin_b_ref[...]
    cq = jnp.dot(h, w_cq_ref[...], preferred_element_type=F32)
    cqn = _rms(cq, g_cq_ref[...]).astype(BF16)
    qb = jnp.dot(cqn, w_qup_ref[...], preferred_element_type=F32)
    for hd, qg in enumerate(_rope_lanes(qb, cos_b, sin_b, B_ROPE // 2)):
        _store_transposed(qb_t_ref, hd * HEAD_SLOT, (qg * (B_QK ** -0.5)).T)
    ckv = jnp.dot(h, w_ckv_ref[...], preferred_element_type=F32)
    ckvn = _rms(ckv, g_ckv_ref[...]).astype(BF16)
    k_nope = jnp.dot(ckvn, w_kup_ref[...], preferred_element_type=F32)
    kr = jnp.dot(h, w_kr_ref[...], preferred_element_type=F32)
    kr = _rope_lanes(kr, cos_b, sin_b, B_ROPE // 2)[0]
    for hd in range(B_HEADS):
        kb_ref[:, hd * HEAD_SLOT:(hd + 1) * HEAD_SLOT] = (
            k_nope[:, hd * HEAD_SLOT:(hd + 1) * HEAD_SLOT] + kr).astype(BF16)
    vb = jnp.dot(ckvn, w_vup_ref[...], preferred_element_type=F32)
    for g in range(B_HEADS * B_V // LANES):
        _store_transposed(vb_t_ref, g * LANES, vb[:, g * LANES:(g + 1) * LANES].T)


def _attn_kernel(*refs, moba):
    if moba:
        q_t_ref, k_ref, v_t_ref, kmean_ref, o_t_ref = refs[:5]
        s_scr, smax_scr, p_scr, alpha_scr, m_scr, l_scr, acc_scr, bias_scr = refs[5:]
    else:
        q_t_ref, k_ref, v_t_ref, o_t_ref = refs[:4]
        s_scr, smax_scr, p_scr, alpha_scr, m_scr, l_scr, acc_scr = refs[4:]
    i = pl.program_id(2)
    heads = HEADS_PER_STEP
    vdim = v_t_ref.shape[1] // heads

    def key_lanes(hd):
        g = hd // 2 if moba else hd
        return slice(g * LANES, (g + 1) * LANES)

    def q_rows(hd):
        return slice(hd * HEAD_SLOT, (hd + 1) * HEAD_SLOT)

    if moba:
        nblk = kmean_ref.shape[0]
        blk = lax.broadcasted_iota(jnp.int32, (nblk, Q_TILE), 0).astype(F32)
        past = blk < i.astype(F32)
        for hd in range(heads):
            gate = jnp.dot(kmean_ref[:, key_lanes(hd)].astype(BF16), q_t_ref[q_rows(hd), :],
                           preferred_element_type=F32)
            gate = jnp.where(past, gate, -jnp.inf)
            bias = jnp.full(gate.shape, MASK_VALUE, F32)
            for _ in range(MOBA_TOPK):
                best = jnp.max(gate, axis=0, keepdims=True)
                first = jnp.min(jnp.where(gate == best, blk, float(nblk)), axis=0, keepdims=True)
                pick = blk == first
                bias = jnp.where(pick, 0.0, bias)
                gate = jnp.where(pick, -jnp.inf, gate)
            bias_scr[hd] = jnp.where(past, bias, MASK_VALUE)

    def qk_stage(j, diagonal):
        start = pl.multiple_of(j * K_TILE, K_TILE)
        k_blk = k_ref[pl.ds(start, K_TILE), :]
        for hd in range(heads):
            s = jnp.dot(k_blk[:, key_lanes(hd)], q_t_ref[q_rows(hd), :],
                        preferred_element_type=F32)
            if diagonal:
                key_pos = lax.broadcasted_iota(jnp.int32, s.shape, 0)
                qry_pos = lax.broadcasted_iota(jnp.int32, s.shape, 1)
                s = jnp.where(key_pos <= qry_pos, s, MASK_VALUE)
            elif moba:
                s = s + bias_scr[hd, pl.ds(j, 1), :]
            s_scr[hd] = s
            smax_scr[hd] = jnp.max(s, axis=0, keepdims=True)

    def softmax_stage():
        for hd in range(heads):
            m_old = m_scr[hd]
            m_new = jnp.maximum(m_old, smax_scr[hd])
            alpha = jnp.exp(m_old - m_new)
            p = jnp.exp(s_scr[hd] - m_new)
            l_scr[hd] = alpha * l_scr[hd] + jnp.sum(p, axis=0, keepdims=True)
            p_scr[hd] = p.astype(BF16)
            alpha_scr[hd] = alpha
            m_scr[hd] = m_new

    def pv_stage(j):
        v_t = v_t_ref[j]
        for hd in range(heads):
            acc_scr[hd] = alpha_scr[hd] * acc_scr[hd] + jnp.dot(
                v_t[hd * vdim:(hd + 1) * vdim], p_scr[hd], preferred_element_type=F32)

    m_scr[...] = jnp.full(m_scr.shape, MASK_VALUE, F32)
    l_scr[...] = jnp.zeros(l_scr.shape, F32)
    acc_scr[...] = jnp.zeros(acc_scr.shape, F32)
    alpha_scr[...] = jnp.ones(alpha_scr.shape, F32)
    p_scr[...] = jnp.zeros(p_scr.shape, BF16)
    qk_stage(i, diagonal=True)

    def body(t, carry):
        pv_stage(jnp.where(t == 1, i, jnp.maximum(t - 2, 0)))
        softmax_stage()
        qk_stage(t, diagonal=False)
        return carry

    lax.fori_loop(0, i + 1, body, 0)
    pv_stage(jnp.maximum(i - 1, 0))

    for hd in range(heads):
        o_t_ref[hd * vdim:(hd + 1) * vdim, :] = (acc_scr[hd] / l_scr[hd]).astype(o_t_ref.dtype)


def _post_kernel(x_ref, oa_t_ref, ob_t_ref, gate_ref, w_oa_ref, w_ob_ref, w_out_ref,
                 g_mlp_ref, w_ff1_ref, w_ff2_ref, g_final_ref, out_ref, *, ff_chunk, final_norm):
    d_model = x_ref.shape[1]
    contract_rows = (((0,), (0,)), ((), ()))
    ya, yb = [], []
    for qt in range(ROW_TILE // Q_TILE):
        ya.append(lax.dot_general(oa_t_ref[qt], w_oa_ref[...], contract_rows, preferred_element_type=F32))
        yb.append(lax.dot_general(ob_t_ref[qt], w_ob_ref[...], contract_rows, preferred_element_type=F32))
    ya = jnp.concatenate(ya, axis=0)
    yb = jnp.concatenate(yb, axis=0)
    mixed = gate_ref[:, :d_model].astype(F32) * ya + gate_ref[:, d_model:].astype(F32) * yb
    x1 = x_ref[...] + jnp.dot(mixed.astype(BF16), w_out_ref[...], preferred_element_type=F32)

    h2 = _rms(x1, g_mlp_ref[...]).astype(BF16)
    mlp = jnp.zeros_like(x1)
    for c in range(w_ff1_ref.shape[1] // ff_chunk):
        u = jnp.maximum(jnp.dot(h2, w_ff1_ref[:, c * ff_chunk:(c + 1) * ff_chunk],
                                preferred_element_type=F32), 0.0)
        mlp = mlp + jnp.dot((u * u).astype(BF16), w_ff2_ref[c * ff_chunk:(c + 1) * ff_chunk, :],
                            preferred_element_type=F32)
    x2 = x1 + mlp
    out_ref[...] = _rms(x2, g_final_ref[...]) if final_norm else x2


def _resident(shape):
    return pl.BlockSpec(shape, lambda *_: (0,) * len(shape), pipeline_mode=pl.Buffered(1))


def _rope_tables(seq, dim):
    half = dim // 2
    inv_freq = jnp.power(ROPE_THETA, -jnp.arange(half, dtype=F32) / half)
    ang = jnp.arange(seq, dtype=F32)[:, None] * inv_freq[None, :]
    cos = jnp.concatenate([jnp.cos(ang), jnp.cos(ang)], axis=1)
    sin = jnp.concatenate([-jnp.sin(ang), jnp.sin(ang)], axis=1)
    return cos, sin


def _params(*semantics):
    return pltpu.CompilerParams(dimension_semantics=semantics, vmem_limit_bytes=VMEM_LIMIT_BYTES)


def _pre_call(x2, seq, g_mix, w_in, b_gate, g_cq, w_q_up, g_ckv, w_kv_up):
    rows, d_model = x2.shape
    a_width = A_HEADS * A_HEAD_DIM
    o_q, o_k, o_v = 0, a_width, 2 * a_width
    o_cq = 3 * a_width
    o_ckv = o_cq + B_Q_RANK
    o_kr = o_ckv + B_KV_RANK
    o_gate = o_kr + B_ROPE
    w_in = w_in.astype(BF16)
    w_qkv = w_in[:, o_q:o_cq]
    w_cq = w_in[:, o_cq:o_ckv]
    w_ckv = w_in[:, o_ckv:o_kr]
    w_kr = jnp.pad(w_in[:, o_kr:o_gate], ((0, 0), (B_NOPE, HEAD_SLOT - B_QK)))
    w_gate = w_in[:, o_gate:]
    w_qup = jnp.pad(w_q_up.astype(BF16).reshape(B_Q_RANK, B_HEADS, B_QK),
                    ((0, 0), (0, 0), (0, HEAD_SLOT - B_QK))).reshape(B_Q_RANK, B_HEADS * HEAD_SLOT)
    w_kv = w_kv_up.astype(BF16).reshape(B_KV_RANK, B_HEADS, B_NOPE + B_V)
    w_kup = jnp.pad(w_kv[:, :, :B_NOPE], ((0, 0), (0, 0), (0, HEAD_SLOT - B_NOPE))).reshape(
        B_KV_RANK, B_HEADS * HEAD_SLOT)
    w_vup = w_kv[:, :, B_NOPE:].reshape(B_KV_RANK, B_HEADS * B_V)

    cos_a, sin_a = _rope_tables(seq, A_HEAD_DIM)
    cos_a, sin_a = jnp.tile(cos_a, (1, LANES // A_HEAD_DIM)), jnp.tile(sin_a, (1, LANES // A_HEAD_DIM))
    cos_b, sin_b = _rope_tables(seq, B_ROPE)
    pad = ((0, 0), (B_NOPE, HEAD_SLOT - B_QK))
    cos_b = jnp.pad(cos_b, pad, constant_values=1.0)
    sin_b = jnp.pad(sin_b, pad)

    n_tiles = rows // ROW_TILE
    seq_tiles = seq // ROW_TILE
    q_per_tile = ROW_TILE // Q_TILE
    n_q = rows // Q_TILE
    row_spec = lambda w: pl.BlockSpec((ROW_TILE, w), lambda r: (r, 0))
    table_spec = pl.BlockSpec((ROW_TILE, LANES), lambda r: (r % seq_tiles, 0))
    t_spec = lambda h: pl.BlockSpec((q_per_tile, h, Q_TILE), lambda r: (r, 0, 0))
    vec = lambda a: a.reshape(1, -1).astype(F32)
    operands = [
        (x2, row_spec(d_model)), (vec(g_mix), _resident((1, d_model))),
        (cos_a, table_spec), (sin_a, table_spec), (cos_b, table_spec), (sin_b, table_spec),
        (w_qkv, _resident(w_qkv.shape)), (w_cq, _resident(w_cq.shape)), (w_ckv, _resident(w_ckv.shape)),
        (w_kr, _resident(w_kr.shape)), (w_gate, _resident(w_gate.shape)),
        (vec(b_gate), _resident((1, 2 * d_model))),
        (vec(g_cq), _resident((1, B_Q_RANK))), (w_qup, _resident(w_qup.shape)),
        (vec(g_ckv), _resident((1, B_KV_RANK))), (w_kup, _resident(w_kup.shape)),
        (w_vup, _resident(w_vup.shape)),
    ]
    out_shape = [
        jax.ShapeDtypeStruct((n_q, A_HEADS * HEAD_SLOT, Q_TILE), BF16),
        jax.ShapeDtypeStruct((rows, a_width), BF16),
        jax.ShapeDtypeStruct((n_q, a_width, Q_TILE), BF16),
        jax.ShapeDtypeStruct((n_tiles, ROW_TILE // MOBA_BLOCK, a_width), F32),
        jax.ShapeDtypeStruct((rows, 2 * d_model), BF16),
        jax.ShapeDtypeStruct((n_q, B_HEADS * HEAD_SLOT, Q_TILE), BF16),
        jax.ShapeDtypeStruct((rows, B_HEADS * HEAD_SLOT), BF16),
        jax.ShapeDtypeStruct((n_q, B_HEADS * B_V, Q_TILE), BF16),
    ]
    out_specs = [
        t_spec(A_HEADS * HEAD_SLOT), row_spec(a_width), t_spec(a_width),
        pl.BlockSpec((1, ROW_TILE // MOBA_BLOCK, a_width), lambda r: (r, 0, 0)),
        row_spec(2 * d_model), t_spec(B_HEADS * HEAD_SLOT), row_spec(B_HEADS * HEAD_SLOT),
        t_spec(B_HEADS * B_V),
    ]
    return pl.pallas_call(
        _pre_kernel,
        grid=(n_tiles,),
        in_specs=[s for _, s in operands],
        out_specs=out_specs,
        out_shape=out_shape,
        compiler_params=_params("parallel"),
        name="pre_proj",
    )(*[a for a, _ in operands])


def _attn_call(q_t, k, v_t, kmean, *, batch, seq, moba):
    nblk = seq // Q_TILE
    heads = q_t.shape[1] // HEAD_SLOT
    groups = heads // HEADS_PER_STEP
    vdim = v_t.shape[2] // heads
    key_lanes = k.shape[2] // groups
    in_specs = [
        pl.BlockSpec((None, HEADS_PER_STEP * HEAD_SLOT, Q_TILE), lambda b, g, i: (b * nblk + i, g, 0)),
        pl.BlockSpec((None, seq, key_lanes), lambda b, g, i: (b, 0, g)),
        pl.BlockSpec((None, nblk, HEADS_PER_STEP * vdim, K_TILE), lambda b, g, i: (b, 0, g, 0)),
    ]
    operands = [q_t, k, v_t]
    stat = pltpu.VMEM((HEADS_PER_STEP, 1, Q_TILE), F32)
    scratch = [
        pltpu.VMEM((HEADS_PER_STEP, K_TILE, Q_TILE), F32),
        stat,
        pltpu.VMEM((HEADS_PER_STEP, K_TILE, Q_TILE), BF16),
        stat,
        stat,
        stat,
        pltpu.VMEM((HEADS_PER_STEP, vdim, Q_TILE), F32),
    ]
    if moba:
        in_specs.append(pl.BlockSpec((None, nblk, key_lanes), lambda b, g, i: (b, 0, g)))
        operands.append(kmean)
        scratch.append(pltpu.VMEM((HEADS_PER_STEP, nblk, Q_TILE), F32))
    return pl.pallas_call(
        functools.partial(_attn_kernel, moba=moba),
        grid=(batch, groups, nblk),
        in_specs=in_specs,
        out_specs=pl.BlockSpec((None, HEADS_PER_STEP * vdim, Q_TILE), lambda b, g, i: (b * nblk + i, g, 0)),
        out_shape=jax.ShapeDtypeStruct((batch * nblk, heads * vdim, Q_TILE), BF16),
        scratch_shapes=scratch,
        compiler_params=_params("parallel", "parallel", "arbitrary"),
        name="moba_attn" if moba else "mla_attn",
    )(*operands)


def _post_call(x2, oa_t, ob_t, gates, w_o_a, w_o_b, w_out, g_mlp, w_ff1, w_ff2, g_final, final_norm):
    rows, d_model = x2.shape
    q_per_tile = ROW_TILE // Q_TILE
    row_spec = lambda w: pl.BlockSpec((ROW_TILE, w), lambda r: (r, 0))
    t_spec = lambda a: pl.BlockSpec((q_per_tile,) + a.shape[1:], lambda r: (r, 0, 0))
    vec = lambda a: a.reshape(1, -1).astype(F32)
    weights = [w.astype(BF16) for w in (w_o_a, w_o_b, w_out)]
    operands = [(x2, row_spec(d_model)), (oa_t, t_spec(oa_t)), (ob_t, t_spec(ob_t)),
                (gates, row_spec(2 * d_model))]
    operands += [(w, _resident(w.shape)) for w in weights]
    operands += [(vec(g_mlp), _resident((1, d_model))),
                 (w_ff1.astype(BF16), _resident(w_ff1.shape)), (w_ff2.astype(BF16), _resident(w_ff2.shape)),
                 (vec(g_final), _resident((1, d_model)))]
    return pl.pallas_call(
        functools.partial(_post_kernel, ff_chunk=1024, final_norm=final_norm),
        grid=(rows // ROW_TILE,),
        in_specs=[s for _, s in operands],
        out_specs=row_spec(d_model),
        out_shape=jax.ShapeDtypeStruct((rows, d_model), F32),
        compiler_params=_params("parallel"),
        name="post_mlp",
    )(*[a for a, _ in operands])


def kernel(x, g_mix, w_in, b_gate, g_cq, w_q_up, g_ckv, w_kv_up, w_o_a, w_o_b, w_out, g_mlp, w_ff1,
           w_ff2, g_final):
    batch, seq, d_model = x.shape
    depth = w_in.shape[0]
    assert seq % ROW_TILE == 0 and ROW_TILE % MOBA_BLOCK == 0
    assert A_HEADS % HEADS_PER_STEP == 0 and B_HEADS % HEADS_PER_STEP == 0 and HEADS_PER_STEP % 2 == 0
    nblk = seq // Q_TILE
    x2 = x.reshape(batch * seq, d_model)
    for l in range(depth):
        qa_t, ka, va_t, kmean, gates, qb_t, kb, vb_t = _pre_call(
            x2, seq, g_mix[l], w_in[l], b_gate[l], g_cq[l], w_q_up[l], g_ckv[l], w_kv_up[l])
        oa_t = _attn_call(qa_t, ka.reshape(batch, seq, -1), va_t.reshape(batch, nblk, -1, K_TILE),
                          kmean.reshape(batch, nblk, -1), batch=batch, seq=seq, moba=True)
        ob_t = _attn_call(qb_t, kb.reshape(batch, seq, -1), vb_t.reshape(batch, nblk, -1, K_TILE),
                          None, batch=batch, seq=seq, moba=False)
        x2 = _post_call(x2, oa_t, ob_t, gates, w_o_a[l], w_o_b[l], w_out[l], g_mlp[l], w_ff1[l],
                        w_ff2[l], g_final, final_norm=(l == depth - 1))
    return x2.reshape(batch, seq, d_model)
```

```python
import functools

import jax
import jax.numpy as jnp
from jax import lax
from jax.experimental import pallas as pl
from jax.experimental.pallas import tpu as pltpu

EPS = 1e-6
ROPE_THETA = 10000.0

A_HEADS = 8
A_HEAD_DIM = 64
MOBA_BLOCK = 256
MOBA_TOPK = 3

B_HEADS = 8
B_NOPE = 64
B_ROPE = 32
B_QK = B_NOPE + B_ROPE
B_V = 64
B_Q_RANK = 384
B_KV_RANK = 256

LANES = 128
HEAD_SLOT = 128
ROW_TILE = 512
Q_TILE = MOBA_BLOCK
K_TILE = MOBA_BLOCK
HEADS_PER_STEP = 8
MASK_VALUE = -1e30
LOG2_E = 1.4426950408889634
SUM_ROWS = 16
VMEM_LIMIT_BYTES = 56 * 1024 * 1024

F32 = jnp.float32
BF16 = jnp.bfloat16


def _rms(x, g):
    return x * lax.rsqrt(jnp.mean(x * x, axis=-1, keepdims=True) + EPS) * g


def _swap_halves(x, lane, half):
    n = x.shape[-1]
    fwd = pltpu.roll(x, n - half, axis=1)
    bwd = pltpu.roll(x, half, axis=1)
    return jnp.where((lane % (2 * half)) < half, fwd, bwd)


def _rope_lanes(x, cos, sin_signed, half):
    lane = lax.broadcasted_iota(jnp.int32, (x.shape[0], LANES), 1)
    groups = []
    for g in range(x.shape[1] // LANES):
        xg = x[:, g * LANES:(g + 1) * LANES]
        groups.append(xg * cos + _swap_halves(xg, lane, half) * sin_signed)
    return groups


def _store_transposed(dst_ref, row0, t):
    rows = t.shape[0]
    for qt in range(ROW_TILE // Q_TILE):
        dst_ref[qt, row0:row0 + rows, :] = t[:, qt * Q_TILE:(qt + 1) * Q_TILE].astype(dst_ref.dtype)


def _pre_kernel(x_ref, gmix_ref, cos_a_ref, sin_a_ref, cos_b_ref, sin_b_ref,
                w_qkv_ref, w_cq_ref, w_ckv_ref, w_kr_ref, w_gate_ref, b_gate_ref,
                g_cq_ref, w_qup_ref, g_ckv_ref, w_kup_ref, w_vup_ref,
                qa_t_ref, ka_ref, va_t_ref, kmean_ref, gate_ref, qb_t_ref, kb_ref, vb_t_ref):
    a_width = A_HEADS * A_HEAD_DIM
    h = _rms(x_ref[...], gmix_ref[...]).astype(BF16)

    qkv = jnp.dot(h, w_qkv_ref[...], preferred_element_type=F32)
    cos_a, sin_a = cos_a_ref[...], sin_a_ref[...]
    q_groups = _rope_lanes(qkv[:, :a_width], cos_a, sin_a, A_HEAD_DIM // 2)
    k_groups = _rope_lanes(qkv[:, a_width:2 * a_width], cos_a, sin_a, A_HEAD_DIM // 2)
    sub = lax.broadcasted_iota(jnp.int32, (LANES, ROW_TILE), 0)
    for g, qg in enumerate(q_groups):
        t = (qg * (A_HEAD_DIM ** -0.5 * LOG2_E)).T
        _store_transposed(qa_t_ref, (2 * g) * HEAD_SLOT, jnp.where(sub < A_HEAD_DIM, t, 0.0))
        _store_transposed(qa_t_ref, (2 * g + 1) * HEAD_SLOT, jnp.where(sub >= A_HEAD_DIM, t, 0.0))
    for g, kg in enumerate(k_groups):
        ka_ref[:, g * LANES:(g + 1) * LANES] = kg.astype(BF16)
        for blk in range(ROW_TILE // MOBA_BLOCK):
            kmean_ref[0, blk:blk + 1, g * LANES:(g + 1) * LANES] = jnp.mean(
                kg[blk * MOBA_BLOCK:(blk + 1) * MOBA_BLOCK], axis=0, keepdims=True)
    for g in range(a_width // LANES):
        vg = qkv[:, 2 * a_width + g * LANES:2 * a_width + (g + 1) * LANES]
        _store_transposed(va_t_ref, g * LANES, vg.T)

    gate = jnp.dot(h, w_gate_ref[...], preferred_element_type=F32) + b_gate_ref[...]
    gate_ref[...] = jax.nn.sigmoid(gate).astype(BF16)

    cos_b, sin_b = cos_b_ref[...], sin_b_ref[...]
    cq = jnp.dot(h, w_cq_ref[...], preferred_element_type=F32)
    cqn = _rms(cq, g_cq_ref[...]).astype(BF16)
    qb = jnp.dot(cqn, w_qup_ref[...], preferred_element_type=F32)
    for hd, qg in enumerate(_rope_lanes(qb, cos_b, sin_b, B_ROPE // 2)):
        _store_transposed(qb_t_ref, hd * HEAD_SLOT, (qg * (B_QK ** -0.5 * LOG2_E)).T)
    ckv = jnp.dot(h, w_ckv_ref[...], preferred_element_type=F32)
    ckvn = _rms(ckv, g_ckv_ref[...]).astype(BF16)
    k_nope = jnp.dot(ckvn, w_kup_ref[...], preferred_element_type=F32)
    kr = jnp.dot(h, w_kr_ref[...], preferred_element_type=F32)
    kr = _rope_lanes(kr, cos_b, sin_b, B_ROPE // 2)[0]
    for hd in range(B_HEADS):
        kb_ref[:, hd * HEAD_SLOT:(hd + 1) * HEAD_SLOT] = (
            k_nope[:, hd * HEAD_SLOT:(hd + 1) * HEAD_SLOT] + kr).astype(BF16)
    vb = jnp.dot(ckvn, w_vup_ref[...], preferred_element_type=F32)
    for g in range(B_HEADS * B_V // LANES):
        _store_transposed(vb_t_ref, g * LANES, vb[:, g * LANES:(g + 1) * LANES].T)


def _attn_kernel(*refs, moba):
    if moba:
        q_t_ref, k_ref, v_t_ref, kmean_ref, o_t_ref = refs[:5]
        s_scr, smax_scr, p_scr, alpha_scr, m_scr, acc_scr, bias_scr, sbias_scr = refs[5:]
    else:
        q_t_ref, k_ref, v_t_ref, o_t_ref = refs[:4]
        s_scr, smax_scr, p_scr, alpha_scr, m_scr, acc_scr = refs[4:]
    i = pl.program_id(2)
    heads = HEADS_PER_STEP
    vdim = v_t_ref.shape[1] // heads

    def key_lanes(hd):
        g = hd // 2 if moba else hd
        return slice(g * LANES, (g + 1) * LANES)

    def q_rows(hd):
        return slice(hd * HEAD_SLOT, (hd + 1) * HEAD_SLOT)

    if moba:
        nblk = kmean_ref.shape[0]
        blk = lax.broadcasted_iota(jnp.int32, (nblk, Q_TILE), 0).astype(F32)
        past = blk < i.astype(F32)
        for hd in range(heads):
            gate = jnp.dot(kmean_ref[:, key_lanes(hd)].astype(BF16), q_t_ref[q_rows(hd), :],
                           preferred_element_type=F32)
            gate = jnp.where(past, gate, -jnp.inf)
            bias = jnp.full(gate.shape, MASK_VALUE, F32)
            for _ in range(MOBA_TOPK):
                best = jnp.max(gate, axis=0, keepdims=True)
                first = jnp.min(jnp.where(gate == best, blk, float(nblk)), axis=0, keepdims=True)
                pick = blk == first
                bias = jnp.where(pick, 0.0, bias)
                gate = jnp.where(pick, -jnp.inf, gate)
            bias_scr[hd] = jnp.where(past, bias, MASK_VALUE)

    def qk_stage(j, diagonal):
        start = pl.multiple_of(j * K_TILE, K_TILE)
        k_blk = k_ref[pl.ds(start, K_TILE), :]
        for hd in range(heads):
            s = jnp.dot(k_blk[:, key_lanes(hd)], q_t_ref[q_rows(hd), :],
                        preferred_element_type=F32)
            if diagonal:
                key_pos = lax.broadcasted_iota(jnp.int32, s.shape, 0)
                qry_pos = lax.broadcasted_iota(jnp.int32, s.shape, 1)
                s = jnp.where(key_pos <= qry_pos, s, MASK_VALUE)
            s_scr[hd] = s
            smax = jnp.max(s, axis=0, keepdims=True)
            if moba:
                bias = jnp.zeros_like(smax) if diagonal else bias_scr[hd, pl.ds(j, 1), :]
                sbias_scr[hd] = bias
                smax = smax + bias
            smax_scr[hd] = smax

    def softmax_stage():
        for hd in range(heads):
            m_old = m_scr[hd]
            m_new = jnp.maximum(m_old, smax_scr[hd])
            shift = m_new - sbias_scr[hd] if moba else m_new
            p_scr[hd] = jnp.exp2(s_scr[hd] - shift).astype(BF16)
            alpha_scr[hd] = jnp.exp2(m_old - m_new)
            m_scr[hd] = m_new

    ones_rows = jnp.ones((SUM_ROWS, K_TILE), BF16)

    def pv_stage(j):
        v_t = v_t_ref[j]
        for hd in range(heads):
            v_ext = jnp.concatenate([v_t[hd * vdim:(hd + 1) * vdim], ones_rows], axis=0)
            acc_scr[hd] = alpha_scr[hd] * acc_scr[hd] + jnp.dot(
                v_ext, p_scr[hd], preferred_element_type=F32)

    m_scr[...] = jnp.full(m_scr.shape, MASK_VALUE, F32)
    acc_scr[...] = jnp.zeros(acc_scr.shape, F32)
    alpha_scr[...] = jnp.ones(alpha_scr.shape, F32)
    p_scr[...] = jnp.zeros(p_scr.shape, BF16)
    qk_stage(i, diagonal=True)

    def body(t, carry):
        pv_stage(jnp.where(t == 1, i, jnp.maximum(t - 2, 0)))
        softmax_stage()
        qk_stage(t, diagonal=False)
        return carry

    lax.fori_loop(0, i + 1, body, 0)
    pv_stage(jnp.maximum(i - 1, 0))

    for hd in range(heads):
        acc = acc_scr[hd]
        o_t_ref[hd * vdim:(hd + 1) * vdim, :] = (acc[:vdim] / acc[vdim:vdim + 1]).astype(o_t_ref.dtype)


def _post_kernel(x_ref, oa_t_ref, ob_t_ref, gate_ref, w_oa_ref, w_ob_ref, w_out_ref,
                 g_mlp_ref, w_ff1_ref, w_ff2_ref, g_final_ref, out_ref, *, ff_chunk, final_norm):
    d_model = x_ref.shape[1]
    contract_rows = (((0,), (0,)), ((), ()))
    ya, yb = [], []
    for qt in range(ROW_TILE // Q_TILE):
        ya.append(lax.dot_general(oa_t_ref[qt], w_oa_ref[...], contract_rows, preferred_element_type=F32))
        yb.append(lax.dot_general(ob_t_ref[qt], w_ob_ref[...], contract_rows, preferred_element_type=F32))
    ya = jnp.concatenate(ya, axis=0)
    yb = jnp.concatenate(yb, axis=0)
    mixed = gate_ref[:, :d_model].astype(F32) * ya + gate_ref[:, d_model:].astype(F32) * yb
    x1 = x_ref[...] + jnp.dot(mixed.astype(BF16), w_out_ref[...], preferred_element_type=F32)

    h2 = _rms(x1, g_mlp_ref[...]).astype(BF16)
    mlp = jnp.zeros_like(x1)
    for c in range(w_ff1_ref.shape[1] // ff_chunk):
        u = jnp.maximum(jnp.dot(h2, w_ff1_ref[:, c * ff_chunk:(c + 1) * ff_chunk],
                                preferred_element_type=F32), 0.0)
        mlp = mlp + jnp.dot((u * u).astype(BF16), w_ff2_ref[c * ff_chunk:(c + 1) * ff_chunk, :],
                            preferred_element_type=F32)
    x2 = x1 + mlp
    out_ref[...] = _rms(x2, g_final_ref[...]) if final_norm else x2


def _resident(shape):
    return pl.BlockSpec(shape, lambda *_: (0,) * len(shape), pipeline_mode=pl.Buffered(1))


def _rope_tables(seq, dim):
    half = dim // 2
    inv_freq = jnp.power(ROPE_THETA, -jnp.arange(half, dtype=F32) / half)
    ang = jnp.arange(seq, dtype=F32)[:, None] * inv_freq[None, :]
    cos = jnp.concatenate([jnp.cos(ang), jnp.cos(ang)], axis=1)
    sin = jnp.concatenate([-jnp.sin(ang), jnp.sin(ang)], axis=1)
    return cos, sin


def _params(*semantics):
    return pltpu.CompilerParams(dimension_semantics=semantics, vmem_limit_bytes=VMEM_LIMIT_BYTES)


def _pre_call(x2, seq, g_mix, w_in, b_gate, g_cq, w_q_up, g_ckv, w_kv_up):
    rows, d_model = x2.shape
    a_width = A_HEADS * A_HEAD_DIM
    o_q, o_k, o_v = 0, a_width, 2 * a_width
    o_cq = 3 * a_width
    o_ckv = o_cq + B_Q_RANK
    o_kr = o_ckv + B_KV_RANK
    o_gate = o_kr + B_ROPE
    w_in = w_in.astype(BF16)
    w_qkv = w_in[:, o_q:o_cq]
    w_cq = w_in[:, o_cq:o_ckv]
    w_ckv = w_in[:, o_ckv:o_kr]
    w_kr = jnp.pad(w_in[:, o_kr:o_gate], ((0, 0), (B_NOPE, HEAD_SLOT - B_QK)))
    w_gate = w_in[:, o_gate:]
    w_qup = jnp.pad(w_q_up.astype(BF16).reshape(B_Q_RANK, B_HEADS, B_QK),
                    ((0, 0), (0, 0), (0, HEAD_SLOT - B_QK))).reshape(B_Q_RANK, B_HEADS * HEAD_SLOT)
    w_kv = w_kv_up.astype(BF16).reshape(B_KV_RANK, B_HEADS, B_NOPE + B_V)
    w_kup = jnp.pad(w_kv[:, :, :B_NOPE], ((0, 0), (0, 0), (0, HEAD_SLOT - B_NOPE))).reshape(
        B_KV_RANK, B_HEADS * HEAD_SLOT)
    w_vup = w_kv[:, :, B_NOPE:].reshape(B_KV_RANK, B_HEADS * B_V)

    cos_a, sin_a = _rope_tables(seq, A_HEAD_DIM)
    cos_a, sin_a = jnp.tile(cos_a, (1, LANES // A_HEAD_DIM)), jnp.tile(sin_a, (1, LANES // A_HEAD_DIM))
    cos_b, sin_b = _rope_tables(seq, B_ROPE)
    pad = ((0, 0), (B_NOPE, HEAD_SLOT - B_QK))
    cos_b = jnp.pad(cos_b, pad, constant_values=1.0)
    sin_b = jnp.pad(sin_b, pad)

    n_tiles = rows // ROW_TILE
    seq_tiles = seq // ROW_TILE
    q_per_tile = ROW_TILE // Q_TILE
    n_q = rows // Q_TILE
    row_spec = lambda w: pl.BlockSpec((ROW_TILE, w), lambda r: (r, 0))
    table_spec = pl.BlockSpec((ROW_TILE, LANES), lambda r: (r % seq_tiles, 0))
    t_spec = lambda h: pl.BlockSpec((q_per_tile, h, Q_TILE), lambda r: (r, 0, 0))
    vec = lambda a: a.reshape(1, -1).astype(F32)
    operands = [
        (x2, row_spec(d_model)), (vec(g_mix), _resident((1, d_model))),
        (cos_a, table_spec), (sin_a, table_spec), (cos_b, table_spec), (sin_b, table_spec),
        (w_qkv, _resident(w_qkv.shape)), (w_cq, _resident(w_cq.shape)), (w_ckv, _resident(w_ckv.shape)),
        (w_kr, _resident(w_kr.shape)), (w_gate, _resident(w_gate.shape)),
        (vec(b_gate), _resident((1, 2 * d_model))),
        (vec(g_cq), _resident((1, B_Q_RANK))), (w_qup, _resident(w_qup.shape)),
        (vec(g_ckv), _resident((1, B_KV_RANK))), (w_kup, _resident(w_kup.shape)),
        (w_vup, _resident(w_vup.shape)),
    ]
    out_shape = [
        jax.ShapeDtypeStruct((n_q, A_HEADS * HEAD_SLOT, Q_TILE), BF16),
        jax.ShapeDtypeStruct((rows, a_width), BF16),
        jax.ShapeDtypeStruct((n_q, a_width, Q_TILE), BF16),
        jax.ShapeDtypeStruct((n_tiles, ROW_TILE // MOBA_BLOCK, a_width), F32),
        jax.ShapeDtypeStruct((rows, 2 * d_model), BF16),
        jax.ShapeDtypeStruct((n_q, B_HEADS * HEAD_SLOT, Q_TILE), BF16),
        jax.ShapeDtypeStruct((rows, B_HEADS * HEAD_SLOT), BF16),
        jax.ShapeDtypeStruct((n_q, B_HEADS * B_V, Q_TILE), BF16),
    ]
    out_specs = [
        t_spec(A_HEADS * HEAD_SLOT), row_spec(a_width), t_spec(a_width),
        pl.BlockSpec((1, ROW_TILE // MOBA_BLOCK, a_width), lambda r: (r, 0, 0)),
        row_spec(2 * d_model), t_spec(B_HEADS * HEAD_SLOT), row_spec(B_HEADS * HEAD_SLOT),
        t_spec(B_HEADS * B_V),
    ]
    return pl.pallas_call(
        _pre_kernel,
        grid=(n_tiles,),
        in_specs=[s for _, s in operands],
        out_specs=out_specs,
        out_shape=out_shape,
        compiler_params=_params("parallel"),
        name="pre_proj",
    )(*[a for a, _ in operands])


def _attn_call(q_t, k, v_t, kmean, *, batch, seq, moba):
    nblk = seq // Q_TILE
    heads = q_t.shape[1] // HEAD_SLOT
    groups = heads // HEADS_PER_STEP
    vdim = v_t.shape[2] // heads
    key_lanes = k.shape[2] // groups
    in_specs = [
        pl.BlockSpec((None, HEADS_PER_STEP * HEAD_SLOT, Q_TILE), lambda b, g, i: (b * nblk + i, g, 0)),
        pl.BlockSpec((None, seq, key_lanes), lambda b, g, i: (b, 0, g), pipeline_mode=pl.Buffered(1)),
        pl.BlockSpec((None, nblk, HEADS_PER_STEP * vdim, K_TILE), lambda b, g, i: (b, 0, g, 0),
                     pipeline_mode=pl.Buffered(1)),
    ]
    operands = [q_t, k, v_t]
    stat = pltpu.VMEM((HEADS_PER_STEP, 1, Q_TILE), F32)
    scratch = [
        pltpu.VMEM((HEADS_PER_STEP, K_TILE, Q_TILE), F32),
        stat,
        pltpu.VMEM((HEADS_PER_STEP, K_TILE, Q_TILE), BF16),
        stat,
        stat,
        pltpu.VMEM((HEADS_PER_STEP, vdim + SUM_ROWS, Q_TILE), F32),
    ]
    if moba:
        in_specs.append(pl.BlockSpec((None, nblk, key_lanes), lambda b, g, i: (b, 0, g)))
        operands.append(kmean)
        scratch.append(pltpu.VMEM((HEADS_PER_STEP, nblk, Q_TILE), F32))
        scratch.append(stat)
    return pl.pallas_call(
        functools.partial(_attn_kernel, moba=moba),
        grid=(batch, groups, nblk),
        in_specs=in_specs,
        out_specs=pl.BlockSpec((None, HEADS_PER_STEP * vdim, Q_TILE), lambda b, g, i: (b * nblk + i, g, 0)),
        out_shape=jax.ShapeDtypeStruct((batch * nblk, heads * vdim, Q_TILE), BF16),
        scratch_shapes=scratch,
        compiler_params=_params("parallel", "parallel", "arbitrary"),
        name="moba_attn" if moba else "mla_attn",
    )(*operands)


def _post_call(x2, oa_t, ob_t, gates, w_o_a, w_o_b, w_out, g_mlp, w_ff1, w_ff2, g_final, final_norm):
    rows, d_model = x2.shape
    q_per_tile = ROW_TILE // Q_TILE
    row_spec = lambda w: pl.BlockSpec((ROW_TILE, w), lambda r: (r, 0))
    t_spec = lambda a: pl.BlockSpec((q_per_tile,) + a.shape[1:], lambda r: (r, 0, 0))
    vec = lambda a: a.reshape(1, -1).astype(F32)
    weights = [w.astype(BF16) for w in (w_o_a, w_o_b, w_out)]
    operands = [(x2, row_spec(d_model)), (oa_t, t_spec(oa_t)), (ob_t, t_spec(ob_t)),
                (gates, row_spec(2 * d_model))]
    operands += [(w, _resident(w.shape)) for w in weights]
    operands += [(vec(g_mlp), _resident((1, d_model))),
                 (w_ff1.astype(BF16), _resident(w_ff1.shape)), (w_ff2.astype(BF16), _resident(w_ff2.shape)),
                 (vec(g_final), _resident((1, d_model)))]
    return pl.pallas_call(
        functools.partial(_post_kernel, ff_chunk=1024, final_norm=final_norm),
        grid=(rows // ROW_TILE,),
        in_specs=[s for _, s in operands],
        out_specs=row_spec(d_model),
        out_shape=jax.ShapeDtypeStruct((rows, d_model), F32),
        compiler_params=_params("parallel"),
        name="post_mlp",
    )(*[a for a, _ in operands])


def kernel(x, g_mix, w_in, b_gate, g_cq, w_q_up, g_ckv, w_kv_up, w_o_a, w_o_b, w_out, g_mlp, w_ff1,
           w_ff2, g_final):
    batch, seq, d_model = x.shape
    depth = w_in.shape[0]
    assert seq % ROW_TILE == 0 and ROW_TILE % MOBA_BLOCK == 0
    assert A_HEADS % HEADS_PER_STEP == 0 and B_HEADS % HEADS_PER_STEP == 0 and HEADS_PER_STEP % 2 == 0
    nblk = seq // Q_TILE
    x2 = x.reshape(batch * seq, d_model)
    for l in range(depth):
        qa_t, ka, va_t, kmean, gates, qb_t, kb, vb_t = _pre_call(
            x2, seq, g_mix[l], w_in[l], b_gate[l], g_cq[l], w_q_up[l], g_ckv[l], w_kv_up[l])
        oa_t = _attn_call(qa_t, ka.reshape(batch, seq, -1), va_t.reshape(batch, nblk, -1, K_TILE),
                          kmean.reshape(batch, nblk, -1), batch=batch, seq=seq, moba=True)
        ob_t = _attn_call(qb_t, kb.reshape(batch, seq, -1), vb_t.reshape(batch, nblk, -1, K_TILE),
                          None, batch=batch, seq=seq, moba=False)
        x2 = _post_call(x2, oa_t, ob_t, gates, w_o_a[l], w_o_b[l], w_out[l], g_mlp[l], w_ff1[l],
                        w_ff2[l], g_final, final_norm=(l == depth - 1))
    return x2.reshape(batch, seq, d_model)
```

```python
import functools

import jax
import jax.numpy as jnp
from jax import lax
from jax.experimental import pallas as pl
from jax.experimental.pallas import tpu as pltpu

EPS = 1e-6
ROPE_THETA = 10000.0

A_HEADS = 8
A_HEAD_DIM = 64
MOBA_BLOCK = 256
MOBA_TOPK = 3

B_HEADS = 8
B_NOPE = 64
B_ROPE = 32
B_QK = B_NOPE + B_ROPE
B_V = 64
B_Q_RANK = 384
B_KV_RANK = 256

LANES = 128
HEAD_SLOT = 128
ROW_TILE = 512
Q_TILE = MOBA_BLOCK
K_TILE = MOBA_BLOCK
HEADS_PER_STEP = 8
MASK_VALUE = -1e30
LOG2_E = 1.4426950408889634
SUM_ROWS = 16
VMEM_LIMIT_BYTES = 56 * 1024 * 1024

F32 = jnp.float32
BF16 = jnp.bfloat16


def _rms(x, g):
    return x * lax.rsqrt(jnp.mean(x * x, axis=-1, keepdims=True) + EPS) * g


def _swap_halves(x, lane, half):
    n = x.shape[-1]
    fwd = pltpu.roll(x, n - half, axis=1)
    bwd = pltpu.roll(x, half, axis=1)
    return jnp.where((lane % (2 * half)) < half, fwd, bwd)


def _rope_lanes(x, cos, sin_signed, half):
    lane = lax.broadcasted_iota(jnp.int32, (x.shape[0], LANES), 1)
    groups = []
    for g in range(x.shape[1] // LANES):
        xg = x[:, g * LANES:(g + 1) * LANES]
        groups.append(xg * cos + _swap_halves(xg, lane, half) * sin_signed)
    return groups


def _store_transposed(dst_ref, row0, t):
    rows = t.shape[0]
    for qt in range(ROW_TILE // Q_TILE):
        dst_ref[qt, row0:row0 + rows, :] = t[:, qt * Q_TILE:(qt + 1) * Q_TILE].astype(dst_ref.dtype)


def _pre_kernel(x_ref, gmix_ref, cos_a_ref, sin_a_ref, cos_b_ref, sin_b_ref,
                w_qkv_ref, w_cq_ref, w_ckv_ref, w_kr_ref, w_gate_ref, b_gate_ref,
                g_cq_ref, w_qup_ref, g_ckv_ref, w_kup_ref, w_vup_ref,
                qa_t_ref, ka_ref, va_t_ref, kmean_ref, gate_ref, qb_t_ref, kb_ref, vb_t_ref):
    a_width = A_HEADS * A_HEAD_DIM
    h = _rms(x_ref[...], gmix_ref[...]).astype(BF16)

    qkv = jnp.dot(h, w_qkv_ref[...], preferred_element_type=F32)
    cos_a, sin_a = cos_a_ref[...], sin_a_ref[...]
    q_groups = _rope_lanes(qkv[:, :a_width], cos_a, sin_a, A_HEAD_DIM // 2)
    k_groups = _rope_lanes(qkv[:, a_width:2 * a_width], cos_a, sin_a, A_HEAD_DIM // 2)
    sub = lax.broadcasted_iota(jnp.int32, (LANES, ROW_TILE), 0)
    for g, qg in enumerate(q_groups):
        t = (qg * (A_HEAD_DIM ** -0.5 * LOG2_E)).T
        _store_transposed(qa_t_ref, (2 * g) * HEAD_SLOT, jnp.where(sub < A_HEAD_DIM, t, 0.0))
        _store_transposed(qa_t_ref, (2 * g + 1) * HEAD_SLOT, jnp.where(sub >= A_HEAD_DIM, t, 0.0))
    for g, kg in enumerate(k_groups):
        ka_ref[:, g * LANES:(g + 1) * LANES] = kg.astype(BF16)
        for blk in range(ROW_TILE // MOBA_BLOCK):
            kmean_ref[0, blk:blk + 1, g * LANES:(g + 1) * LANES] = jnp.mean(
                kg[blk * MOBA_BLOCK:(blk + 1) * MOBA_BLOCK], axis=0, keepdims=True)
    for g in range(a_width // LANES):
        vg = qkv[:, 2 * a_width + g * LANES:2 * a_width + (g + 1) * LANES]
        _store_transposed(va_t_ref, g * LANES, vg.T)

    gate = jnp.dot(h, w_gate_ref[...], preferred_element_type=F32) + b_gate_ref[...]
    gate_ref[...] = jax.nn.sigmoid(gate).astype(BF16)

    cos_b, sin_b = cos_b_ref[...], sin_b_ref[...]
    cq = jnp.dot(h, w_cq_ref[...], preferred_element_type=F32)
    cqn = _rms(cq, g_cq_ref[...]).astype(BF16)
    qb = jnp.dot(cqn, w_qup_ref[...], preferred_element_type=F32)
    for hd, qg in enumerate(_rope_lanes(qb, cos_b, sin_b, B_ROPE // 2)):
        _store_transposed(qb_t_ref, hd * HEAD_SLOT, (qg * (B_QK ** -0.5 * LOG2_E)).T)
    ckv = jnp.dot(h, w_ckv_ref[...], preferred_element_type=F32)
    ckvn = _rms(ckv, g_ckv_ref[...]).astype(BF16)
    k_nope = jnp.dot(ckvn, w_kup_ref[...], preferred_element_type=F32)
    kr = jnp.dot(h, w_kr_ref[...], preferred_element_type=F32)
    kr = _rope_lanes(kr, cos_b, sin_b, B_ROPE // 2)[0]
    for hd in range(B_HEADS):
        kb_ref[:, hd * HEAD_SLOT:(hd + 1) * HEAD_SLOT] = (
            k_nope[:, hd * HEAD_SLOT:(hd + 1) * HEAD_SLOT] + kr).astype(BF16)
    vb = jnp.dot(ckvn, w_vup_ref[...], preferred_element_type=F32)
    for g in range(B_HEADS * B_V // LANES):
        _store_transposed(vb_t_ref, g * LANES, vb[:, g * LANES:(g + 1) * LANES].T)


def _attn_kernel(*refs, moba):
    if moba:
        q_t_ref, k_ref, v_t_ref, kmean_ref, o_t_ref = refs[:5]
        s_scr, smax_scr, m_scr, acc_scr, bias_scr, sbias_scr = refs[5:]
    else:
        q_t_ref, k_ref, v_t_ref, o_t_ref = refs[:4]
        s_scr, smax_scr, m_scr, acc_scr = refs[4:]
    i = pl.program_id(2)
    heads = HEADS_PER_STEP
    vdim = v_t_ref.shape[1] // heads

    def key_lanes(hd):
        g = hd // 2 if moba else hd
        return slice(g * LANES, (g + 1) * LANES)

    def q_rows(hd):
        return slice(hd * HEAD_SLOT, (hd + 1) * HEAD_SLOT)

    if moba:
        nblk = kmean_ref.shape[0]
        blk = lax.broadcasted_iota(jnp.int32, (nblk, Q_TILE), 0).astype(F32)
        past = blk < i.astype(F32)
        for hd in range(heads):
            gate = jnp.dot(kmean_ref[:, key_lanes(hd)].astype(BF16), q_t_ref[q_rows(hd), :],
                           preferred_element_type=F32)
            gate = jnp.where(past, gate, -jnp.inf)
            bias = jnp.full(gate.shape, MASK_VALUE, F32)
            for _ in range(MOBA_TOPK):
                best = jnp.max(gate, axis=0, keepdims=True)
                first = jnp.min(jnp.where(gate == best, blk, float(nblk)), axis=0, keepdims=True)
                pick = blk == first
                bias = jnp.where(pick, 0.0, bias)
                gate = jnp.where(pick, -jnp.inf, gate)
            bias_scr[hd] = jnp.where(past, bias, MASK_VALUE)

    def qk_stage(j, diagonal):
        start = pl.multiple_of(j * K_TILE, K_TILE)
        k_blk = k_ref[pl.ds(start, K_TILE), :]
        for hd in range(heads):
            s = jnp.dot(k_blk[:, key_lanes(hd)], q_t_ref[q_rows(hd), :],
                        preferred_element_type=F32)
            if diagonal:
                key_pos = lax.broadcasted_iota(jnp.int32, s.shape, 0)
                qry_pos = lax.broadcasted_iota(jnp.int32, s.shape, 1)
                s = jnp.where(key_pos <= qry_pos, s, MASK_VALUE)
            s_scr[hd] = s
            smax = jnp.max(s, axis=0, keepdims=True)
            if moba:
                bias = jnp.zeros_like(smax) if diagonal else bias_scr[hd, pl.ds(j, 1), :]
                sbias_scr[hd] = bias
                smax = smax + bias
            smax_scr[hd] = smax

    ones_rows = jnp.ones((SUM_ROWS, K_TILE), BF16)

    def softmax_pv_stage(j):
        v_t = v_t_ref[j]
        for hd in range(heads):
            m_old = m_scr[hd]
            m_new = jnp.maximum(m_old, smax_scr[hd])
            shift = m_new - sbias_scr[hd] if moba else m_new
            p = jnp.exp2(s_scr[hd] - shift).astype(BF16)
            v_ext = jnp.concatenate([v_t[hd * vdim:(hd + 1) * vdim], ones_rows], axis=0)
            acc_scr[hd] = jnp.exp2(m_old - m_new) * acc_scr[hd] + jnp.dot(
                v_ext, p, preferred_element_type=F32)
            m_scr[hd] = m_new

    m_scr[...] = jnp.full(m_scr.shape, MASK_VALUE, F32)
    acc_scr[...] = jnp.zeros(acc_scr.shape, F32)
    qk_stage(i, diagonal=True)

    def body(t, carry):
        softmax_pv_stage(jnp.where(t == 0, i, t - 1))
        qk_stage(t, diagonal=False)
        return carry

    lax.fori_loop(0, i, body, 0)
    softmax_pv_stage(jnp.maximum(i - 1, 0))

    for hd in range(heads):
        acc = acc_scr[hd]
        o_t_ref[hd * vdim:(hd + 1) * vdim, :] = (acc[:vdim] / acc[vdim:vdim + 1]).astype(o_t_ref.dtype)


def _post_kernel(x_ref, oa_t_ref, ob_t_ref, gate_ref, w_oa_ref, w_ob_ref, w_out_ref,
                 g_mlp_ref, w_ff1_ref, w_ff2_ref, g_final_ref, out_ref, *, ff_chunk, final_norm):
    d_model = x_ref.shape[1]
    contract_rows = (((0,), (0,)), ((), ()))
    ya, yb = [], []
    for qt in range(ROW_TILE // Q_TILE):
        ya.append(lax.dot_general(oa_t_ref[qt], w_oa_ref[...], contract_rows, preferred_element_type=F32))
        yb.append(lax.dot_general(ob_t_ref[qt], w_ob_ref[...], contract_rows, preferred_element_type=F32))
    ya = jnp.concatenate(ya, axis=0)
    yb = jnp.concatenate(yb, axis=0)
    mixed = gate_ref[:, :d_model].astype(F32) * ya + gate_ref[:, d_model:].astype(F32) * yb
    x1 = x_ref[...] + jnp.dot(mixed.astype(BF16), w_out_ref[...], preferred_element_type=F32)

    h2 = _rms(x1, g_mlp_ref[...]).astype(BF16)
    mlp = jnp.zeros_like(x1)
    for c in range(w_ff1_ref.shape[1] // ff_chunk):
        u = jnp.maximum(jnp.dot(h2, w_ff1_ref[:, c * ff_chunk:(c + 1) * ff_chunk],
                                preferred_element_type=F32), 0.0)
        mlp = mlp + jnp.dot((u * u).astype(BF16), w_ff2_ref[c * ff_chunk:(c + 1) * ff_chunk, :],
                            preferred_element_type=F32)
    x2 = x1 + mlp
    out_ref[...] = _rms(x2, g_final_ref[...]) if final_norm else x2


def _resident(shape):
    return pl.BlockSpec(shape, lambda *_: (0,) * len(shape), pipeline_mode=pl.Buffered(1))


def _rope_tables(seq, dim):
    half = dim // 2
    inv_freq = jnp.power(ROPE_THETA, -jnp.arange(half, dtype=F32) / half)
    ang = jnp.arange(seq, dtype=F32)[:, None] * inv_freq[None, :]
    cos = jnp.concatenate([jnp.cos(ang), jnp.cos(ang)], axis=1)
    sin = jnp.concatenate([-jnp.sin(ang), jnp.sin(ang)], axis=1)
    return cos, sin


def _params(*semantics):
    return pltpu.CompilerParams(dimension_semantics=semantics, vmem_limit_bytes=VMEM_LIMIT_BYTES)


def _pre_call(x2, seq, g_mix, w_in, b_gate, g_cq, w_q_up, g_ckv, w_kv_up):
    rows, d_model = x2.shape
    a_width = A_HEADS * A_HEAD_DIM
    o_q, o_k, o_v = 0, a_width, 2 * a_width
    o_cq = 3 * a_width
    o_ckv = o_cq + B_Q_RANK
    o_kr = o_ckv + B_KV_RANK
    o_gate = o_kr + B_ROPE
    w_in = w_in.astype(BF16)
    w_qkv = w_in[:, o_q:o_cq]
    w_cq = w_in[:, o_cq:o_ckv]
    w_ckv = w_in[:, o_ckv:o_kr]
    w_kr = jnp.pad(w_in[:, o_kr:o_gate], ((0, 0), (B_NOPE, HEAD_SLOT - B_QK)))
    w_gate = w_in[:, o_gate:]
    w_qup = jnp.pad(w_q_up.astype(BF16).reshape(B_Q_RANK, B_HEADS, B_QK),
                    ((0, 0), (0, 0), (0, HEAD_SLOT - B_QK))).reshape(B_Q_RANK, B_HEADS * HEAD_SLOT)
    w_kv = w_kv_up.astype(BF16).reshape(B_KV_RANK, B_HEADS, B_NOPE + B_V)
    w_kup = jnp.pad(w_kv[:, :, :B_NOPE], ((0, 0), (0, 0), (0, HEAD_SLOT - B_NOPE))).reshape(
        B_KV_RANK, B_HEADS * HEAD_SLOT)
    w_vup = w_kv[:, :, B_NOPE:].reshape(B_KV_RANK, B_HEADS * B_V)

    cos_a, sin_a = _rope_tables(seq, A_HEAD_DIM)
    cos_a, sin_a = jnp.tile(cos_a, (1, LANES // A_HEAD_DIM)), jnp.tile(sin_a, (1, LANES // A_HEAD_DIM))
    cos_b, sin_b = _rope_tables(seq, B_ROPE)
    pad = ((0, 0), (B_NOPE, HEAD_SLOT - B_QK))
    cos_b = jnp.pad(cos_b, pad, constant_values=1.0)
    sin_b = jnp.pad(sin_b, pad)

    n_tiles = rows // ROW_TILE
    seq_tiles = seq // ROW_TILE
    q_per_tile = ROW_TILE // Q_TILE
    n_q = rows // Q_TILE
    row_spec = lambda w: pl.BlockSpec((ROW_TILE, w), lambda r: (r, 0))
    table_spec = pl.BlockSpec((ROW_TILE, LANES), lambda r: (r % seq_tiles, 0))
    t_spec = lambda h: pl.BlockSpec((q_per_tile, h, Q_TILE), lambda r: (r, 0, 0))
    vec = lambda a: a.reshape(1, -1).astype(F32)
    operands = [
        (x2, row_spec(d_model)), (vec(g_mix), _resident((1, d_model))),
        (cos_a, table_spec), (sin_a, table_spec), (cos_b, table_spec), (sin_b, table_spec),
        (w_qkv, _resident(w_qkv.shape)), (w_cq, _resident(w_cq.shape)), (w_ckv, _resident(w_ckv.shape)),
        (w_kr, _resident(w_kr.shape)), (w_gate, _resident(w_gate.shape)),
        (vec(b_gate), _resident((1, 2 * d_model))),
        (vec(g_cq), _resident((1, B_Q_RANK))), (w_qup, _resident(w_qup.shape)),
        (vec(g_ckv), _resident((1, B_KV_RANK))), (w_kup, _resident(w_kup.shape)),
        (w_vup, _resident(w_vup.shape)),
    ]
    out_shape = [
        jax.ShapeDtypeStruct((n_q, A_HEADS * HEAD_SLOT, Q_TILE), BF16),
        jax.ShapeDtypeStruct((rows, a_width), BF16),
        jax.ShapeDtypeStruct((n_q, a_width, Q_TILE), BF16),
        jax.ShapeDtypeStruct((n_tiles, ROW_TILE // MOBA_BLOCK, a_width), F32),
        jax.ShapeDtypeStruct((rows, 2 * d_model), BF16),
        jax.ShapeDtypeStruct((n_q, B_HEADS * HEAD_SLOT, Q_TILE), BF16),
        jax.ShapeDtypeStruct((rows, B_HEADS * HEAD_SLOT), BF16),
        jax.ShapeDtypeStruct((n_q, B_HEADS * B_V, Q_TILE), BF16),
    ]
    out_specs = [
        t_spec(A_HEADS * HEAD_SLOT), row_spec(a_width), t_spec(a_width),
        pl.BlockSpec((1, ROW_TILE // MOBA_BLOCK, a_width), lambda r: (r, 0, 0)),
        row_spec(2 * d_model), t_spec(B_HEADS * HEAD_SLOT), row_spec(B_HEADS * HEAD_SLOT),
        t_spec(B_HEADS * B_V),
    ]
    return pl.pallas_call(
        _pre_kernel,
        grid=(n_tiles,),
        in_specs=[s for _, s in operands],
        out_specs=out_specs,
        out_shape=out_shape,
        compiler_params=_params("parallel"),
        name="pre_proj",
    )(*[a for a, _ in operands])


def _attn_call(q_t, k, v_t, kmean, *, batch, seq, moba):
    nblk = seq // Q_TILE
    heads = q_t.shape[1] // HEAD_SLOT
    groups = heads // HEADS_PER_STEP
    vdim = v_t.shape[2] // heads
    key_lanes = k.shape[2] // groups
    in_specs = [
        pl.BlockSpec((None, HEADS_PER_STEP * HEAD_SLOT, Q_TILE), lambda b, g, i: (b * nblk + i, g, 0)),
        pl.BlockSpec((None, seq, key_lanes), lambda b, g, i: (b, 0, g), pipeline_mode=pl.Buffered(1)),
        pl.BlockSpec((None, nblk, HEADS_PER_STEP * vdim, K_TILE), lambda b, g, i: (b, 0, g, 0),
                     pipeline_mode=pl.Buffered(1)),
    ]
    operands = [q_t, k, v_t]
    stat = pltpu.VMEM((HEADS_PER_STEP, 1, Q_TILE), F32)
    scratch = [
        pltpu.VMEM((HEADS_PER_STEP, K_TILE, Q_TILE), F32),
        stat,
        stat,
        pltpu.VMEM((HEADS_PER_STEP, vdim + SUM_ROWS, Q_TILE), F32),
    ]
    if moba:
        in_specs.append(pl.BlockSpec((None, nblk, key_lanes), lambda b, g, i: (b, 0, g)))
        operands.append(kmean)
        scratch.append(pltpu.VMEM((HEADS_PER_STEP, nblk, Q_TILE), F32))
        scratch.append(stat)
    return pl.pallas_call(
        functools.partial(_attn_kernel, moba=moba),
        grid=(batch, groups, nblk),
        in_specs=in_specs,
        out_specs=pl.BlockSpec((None, HEADS_PER_STEP * vdim, Q_TILE), lambda b, g, i: (b * nblk + i, g, 0)),
        out_shape=jax.ShapeDtypeStruct((batch * nblk, heads * vdim, Q_TILE), BF16),
        scratch_shapes=scratch,
        compiler_params=_params("parallel", "parallel", "arbitrary"),
        name="moba_attn" if moba else "mla_attn",
    )(*operands)


def _post_call(x2, oa_t, ob_t, gates, w_o_a, w_o_b, w_out, g_mlp, w_ff1, w_ff2, g_final, final_norm):
    rows, d_model = x2.shape
    q_per_tile = ROW_TILE // Q_TILE
    row_spec = lambda w: pl.BlockSpec((ROW_TILE, w), lambda r: (r, 0))
    t_spec = lambda a: pl.BlockSpec((q_per_tile,) + a.shape[1:], lambda r: (r, 0, 0))
    vec = lambda a: a.reshape(1, -1).astype(F32)
    weights = [w.astype(BF16) for w in (w_o_a, w_o_b, w_out)]
    operands = [(x2, row_spec(d_model)), (oa_t, t_spec(oa_t)), (ob_t, t_spec(ob_t)),
                (gates, row_spec(2 * d_model))]
    operands += [(w, _resident(w.shape)) for w in weights]
    operands += [(vec(g_mlp), _resident((1, d_model))),
                 (w_ff1.astype(BF16), _resident(w_ff1.shape)), (w_ff2.astype(BF16), _resident(w_ff2.shape)),
                 (vec(g_final), _resident((1, d_model)))]
    return pl.pallas_call(
        functools.partial(_post_kernel, ff_chunk=1024, final_norm=final_norm),
        grid=(rows // ROW_TILE,),
        in_specs=[s for _, s in operands],
        out_specs=row_spec(d_model),
        out_shape=jax.ShapeDtypeStruct((rows, d_model), F32),
        compiler_params=_params("parallel"),
        name="post_mlp",
    )(*[a for a, _ in operands])


def kernel(x, g_mix, w_in, b_gate, g_cq, w_q_up, g_ckv, w_kv_up, w_o_a, w_o_b, w_out, g_mlp, w_ff1,
           w_ff2, g_final):
    batch, seq, d_model = x.shape
    depth = w_in.shape[0]
    assert seq % ROW_TILE == 0 and ROW_TILE % MOBA_BLOCK == 0
    assert A_HEADS % HEADS_PER_STEP == 0 and B_HEADS % HEADS_PER_STEP == 0 and HEADS_PER_STEP % 2 == 0
    nblk = seq // Q_TILE
    x2 = x.reshape(batch * seq, d_model)
    for l in range(depth):
        qa_t, ka, va_t, kmean, gates, qb_t, kb, vb_t = _pre_call(
            x2, seq, g_mix[l], w_in[l], b_gate[l], g_cq[l], w_q_up[l], g_ckv[l], w_kv_up[l])
        oa_t = _attn_call(qa_t, ka.reshape(batch, seq, -1), va_t.reshape(batch, nblk, -1, K_TILE),
                          kmean.reshape(batch, nblk, -1), batch=batch, seq=seq, moba=True)
        ob_t = _attn_call(qb_t, kb.reshape(batch, seq, -1), vb_t.reshape(batch, nblk, -1, K_TILE),
                          None, batch=batch, seq=seq, moba=False)
        x2 = _post_call(x2, oa_t, ob_t, gates, w_o_a[l], w_o_b[l], w_out[l], g_mlp[l], w_ff1[l],
                        w_ff2[l], g_final, final_norm=(l == depth - 1))
    return x2.reshape(batch, seq, d_model)
```

```python
import functools

import jax
import jax.numpy as jnp
from jax import lax
from jax.experimental import pallas as pl
from jax.experimental.pallas import tpu as pltpu

EPS = 1e-6
ROPE_THETA = 10000.0

A_HEADS = 8
A_HEAD_DIM = 64
MOBA_BLOCK = 256
MOBA_TOPK = 3

B_HEADS = 8
B_NOPE = 64
B_ROPE = 32
B_QK = B_NOPE + B_ROPE
B_V = 64
B_Q_RANK = 384
B_KV_RANK = 256

LANES = 128
HEAD_SLOT = 128
ROW_TILE = 512
Q_TILE = MOBA_BLOCK
K_TILE = MOBA_BLOCK
HEADS_PER_STEP = 8
MASK_VALUE = -1e30
LOG2_E = 1.4426950408889634
SUM_ROWS = 16
VMEM_LIMIT_BYTES = 56 * 1024 * 1024

F32 = jnp.float32
BF16 = jnp.bfloat16


def _rms(x, g):
    return x * lax.rsqrt(jnp.mean(x * x, axis=-1, keepdims=True) + EPS) * g


def _swap_halves(x, lane, half):
    n = x.shape[-1]
    fwd = pltpu.roll(x, n - half, axis=1)
    bwd = pltpu.roll(x, half, axis=1)
    return jnp.where((lane % (2 * half)) < half, fwd, bwd)


def _rope_lanes(x, cos, sin_signed, half):
    lane = lax.broadcasted_iota(jnp.int32, (x.shape[0], LANES), 1)
    groups = []
    for g in range(x.shape[1] // LANES):
        xg = x[:, g * LANES:(g + 1) * LANES]
        groups.append(xg * cos + _swap_halves(xg, lane, half) * sin_signed)
    return groups


def _store_transposed(dst_ref, row0, t):
    rows = t.shape[0]
    for qt in range(ROW_TILE // Q_TILE):
        dst_ref[qt, row0:row0 + rows, :] = t[:, qt * Q_TILE:(qt + 1) * Q_TILE].astype(dst_ref.dtype)


def _pre_kernel(x_ref, gmix_ref, cos_a_ref, sin_a_ref, cos_b_ref, sin_b_ref,
                w_qkv_ref, w_cq_ref, w_ckv_ref, w_kr_ref, w_gate_ref, b_gate_ref,
                g_cq_ref, w_qup_ref, g_ckv_ref, w_kup_ref, w_vup_ref,
                qa_t_ref, ka_ref, va_t_ref, kmean_ref, gate_ref, qb_t_ref, kb_ref, vb_t_ref):
    a_width = A_HEADS * A_HEAD_DIM
    h = _rms(x_ref[...], gmix_ref[...]).astype(BF16)

    qkv = jnp.dot(h, w_qkv_ref[...], preferred_element_type=F32)
    cos_a, sin_a = cos_a_ref[...], sin_a_ref[...]
    q_groups = _rope_lanes(qkv[:, :a_width], cos_a, sin_a, A_HEAD_DIM // 2)
    k_groups = _rope_lanes(qkv[:, a_width:2 * a_width], cos_a, sin_a, A_HEAD_DIM // 2)
    sub = lax.broadcasted_iota(jnp.int32, (LANES, ROW_TILE), 0)
    for g, qg in enumerate(q_groups):
        t = (qg * (A_HEAD_DIM ** -0.5 * LOG2_E)).T
        _store_transposed(qa_t_ref, (2 * g) * HEAD_SLOT, jnp.where(sub < A_HEAD_DIM, t, 0.0))
        _store_transposed(qa_t_ref, (2 * g + 1) * HEAD_SLOT, jnp.where(sub >= A_HEAD_DIM, t, 0.0))
    for g, kg in enumerate(k_groups):
        ka_ref[:, g * LANES:(g + 1) * LANES] = kg.astype(BF16)
        for blk in range(ROW_TILE // MOBA_BLOCK):
            kmean_ref[0, blk:blk + 1, g * LANES:(g + 1) * LANES] = jnp.mean(
                kg[blk * MOBA_BLOCK:(blk + 1) * MOBA_BLOCK], axis=0, keepdims=True)
    for g in range(a_width // LANES):
        vg = qkv[:, 2 * a_width + g * LANES:2 * a_width + (g + 1) * LANES]
        _store_transposed(va_t_ref, g * LANES, vg.T)

    gate = jnp.dot(h, w_gate_ref[...], preferred_element_type=F32) + b_gate_ref[...]
    gate_ref[...] = jax.nn.sigmoid(gate).astype(BF16)

    cos_b, sin_b = cos_b_ref[...], sin_b_ref[...]
    cq = jnp.dot(h, w_cq_ref[...], preferred_element_type=F32)
    cqn = _rms(cq, g_cq_ref[...]).astype(BF16)
    qb = jnp.dot(cqn, w_qup_ref[...], preferred_element_type=F32)
    for hd, qg in enumerate(_rope_lanes(qb, cos_b, sin_b, B_ROPE // 2)):
        _store_transposed(qb_t_ref, hd * HEAD_SLOT, (qg * (B_QK ** -0.5 * LOG2_E)).T)
    ckv = jnp.dot(h, w_ckv_ref[...], preferred_element_type=F32)
    ckvn = _rms(ckv, g_ckv_ref[...]).astype(BF16)
    k_nope = jnp.dot(ckvn, w_kup_ref[...], preferred_element_type=F32)
    kr = jnp.dot(h, w_kr_ref[...], preferred_element_type=F32)
    kr = _rope_lanes(kr, cos_b, sin_b, B_ROPE // 2)[0]
    for hd in range(B_HEADS):
        kb_ref[:, hd * HEAD_SLOT:(hd + 1) * HEAD_SLOT] = (
            k_nope[:, hd * HEAD_SLOT:(hd + 1) * HEAD_SLOT] + kr).astype(BF16)
    vb = jnp.dot(ckvn, w_vup_ref[...], preferred_element_type=F32)
    for g in range(B_HEADS * B_V // LANES):
        _store_transposed(vb_t_ref, g * LANES, vb[:, g * LANES:(g + 1) * LANES].T)


def _attn_kernel(*refs, moba):
    if moba:
        q_t_ref, k_ref, v_t_ref, kmean_ref, o_t_ref = refs[:5]
        s_scr, smax_scr, p_scr, alpha_scr, m_scr, acc_scr, bias_scr, sbias_scr = refs[5:]
    else:
        q_t_ref, k_ref, v_t_ref, o_t_ref = refs[:4]
        s_scr, smax_scr, p_scr, alpha_scr, m_scr, acc_scr = refs[4:]
    i = pl.program_id(2)
    heads = HEADS_PER_STEP
    vdim = v_t_ref.shape[1] // heads

    def key_lanes(hd):
        g = hd // 2 if moba else hd
        return slice(g * LANES, (g + 1) * LANES)

    def q_rows(hd):
        return slice(hd * HEAD_SLOT, (hd + 1) * HEAD_SLOT)

    if moba:
        nblk = kmean_ref.shape[0]
        blk = lax.broadcasted_iota(jnp.int32, (nblk, Q_TILE), 0).astype(F32)
        past = blk < i.astype(F32)
        for hd in range(heads):
            gate = jnp.dot(kmean_ref[:, key_lanes(hd)].astype(BF16), q_t_ref[q_rows(hd), :],
                           preferred_element_type=F32)
            gate = jnp.where(past, gate, -jnp.inf)
            bias = jnp.full(gate.shape, MASK_VALUE, F32)
            for _ in range(MOBA_TOPK):
                best = jnp.max(gate, axis=0, keepdims=True)
                first = jnp.min(jnp.where(gate == best, blk, float(nblk)), axis=0, keepdims=True)
                pick = blk == first
                bias = jnp.where(pick, 0.0, bias)
                gate = jnp.where(pick, -jnp.inf, gate)
            bias_scr[hd] = jnp.where(past, bias, MASK_VALUE)

    def qk_stage(j, diagonal):
        start = pl.multiple_of(j * K_TILE, K_TILE)
        k_blk = k_ref[pl.ds(start, K_TILE), :]
        for hd in range(heads):
            s = jnp.dot(k_blk[:, key_lanes(hd)], q_t_ref[q_rows(hd), :],
                        preferred_element_type=F32)
            if diagonal:
                key_pos = lax.broadcasted_iota(jnp.int32, s.shape, 0)
                qry_pos = lax.broadcasted_iota(jnp.int32, s.shape, 1)
                s = jnp.where(key_pos <= qry_pos, s, MASK_VALUE)
            s_scr[hd] = s
            smax = jnp.max(s, axis=0, keepdims=True)
            if moba:
                bias = jnp.zeros_like(smax) if diagonal else bias_scr[hd, pl.ds(j, 1), :]
                sbias_scr[hd] = bias
                smax = smax + bias
            smax_scr[hd] = smax

    ones_rows = jnp.ones((SUM_ROWS, K_TILE), BF16)

    def softmax_stage():
        for hd in range(heads):
            m_old = m_scr[hd]
            m_new = jnp.maximum(m_old, smax_scr[hd])
            shift = m_new - sbias_scr[hd] if moba else m_new
            p_scr[hd] = jnp.exp2(s_scr[hd] - shift).astype(BF16)
            alpha_scr[hd] = jnp.exp2(m_old - m_new)
            m_scr[hd] = m_new

    def pv_stage(j):
        v_t = v_t_ref[j]
        for hd in range(heads):
            v_ext = jnp.concatenate([v_t[hd * vdim:(hd + 1) * vdim], ones_rows], axis=0)
            acc_scr[hd] = alpha_scr[hd] * acc_scr[hd] + jnp.dot(
                v_ext, p_scr[hd], preferred_element_type=F32)

    def previous_block(t):
        return jnp.where(t <= 1, i, t - 2)

    m_scr[...] = jnp.full(m_scr.shape, MASK_VALUE, F32)
    acc_scr[...] = jnp.zeros(acc_scr.shape, F32)
    alpha_scr[...] = jnp.ones(alpha_scr.shape, F32)
    p_scr[...] = jnp.zeros(p_scr.shape, BF16)
    qk_stage(i, diagonal=True)

    def body(t, carry):
        pv_stage(previous_block(t))
        softmax_stage()
        qk_stage(t, diagonal=False)
        return carry

    lax.fori_loop(0, i, body, 0)
    pv_stage(previous_block(i))
    softmax_stage()
    pv_stage(jnp.maximum(i - 1, 0))

    for hd in range(heads):
        acc = acc_scr[hd]
        o_t_ref[hd * vdim:(hd + 1) * vdim, :] = (acc[:vdim] / acc[vdim:vdim + 1]).astype(o_t_ref.dtype)


def _post_kernel(x_ref, oa_t_ref, ob_t_ref, gate_ref, w_oa_ref, w_ob_ref, w_out_ref,
                 g_mlp_ref, w_ff1_ref, w_ff2_ref, g_final_ref, out_ref, *, ff_chunk, final_norm):
    d_model = x_ref.shape[1]
    contract_rows = (((0,), (0,)), ((), ()))
    ya, yb = [], []
    for qt in range(ROW_TILE // Q_TILE):
        ya.append(lax.dot_general(oa_t_ref[qt], w_oa_ref[...], contract_rows, preferred_element_type=F32))
        yb.append(lax.dot_general(ob_t_ref[qt], w_ob_ref[...], contract_rows, preferred_element_type=F32))
    ya = jnp.concatenate(ya, axis=0)
    yb = jnp.concatenate(yb, axis=0)
    mixed = gate_ref[:, :d_model].astype(F32) * ya + gate_ref[:, d_model:].astype(F32) * yb
    x1 = x_ref[...] + jnp.dot(mixed.astype(BF16), w_out_ref[...], preferred_element_type=F32)

    h2 = _rms(x1, g_mlp_ref[...]).astype(BF16)
    mlp = jnp.zeros_like(x1)
    for c in range(w_ff1_ref.shape[1] // ff_chunk):
        u = jnp.maximum(jnp.dot(h2, w_ff1_ref[:, c * ff_chunk:(c + 1) * ff_chunk],
                                preferred_element_type=F32), 0.0)
        mlp = mlp + jnp.dot((u * u).astype(BF16), w_ff2_ref[c * ff_chunk:(c + 1) * ff_chunk, :],
                            preferred_element_type=F32)
    x2 = x1 + mlp
    out_ref[...] = _rms(x2, g_final_ref[...]) if final_norm else x2


def _resident(shape):
    return pl.BlockSpec(shape, lambda *_: (0,) * len(shape), pipeline_mode=pl.Buffered(1))


def _rope_tables(seq, dim):
    half = dim // 2
    inv_freq = jnp.power(ROPE_THETA, -jnp.arange(half, dtype=F32) / half)
    ang = jnp.arange(seq, dtype=F32)[:, None] * inv_freq[None, :]
    cos = jnp.concatenate([jnp.cos(ang), jnp.cos(ang)], axis=1)
    sin = jnp.concatenate([-jnp.sin(ang), jnp.sin(ang)], axis=1)
    return cos, sin


def _params(*semantics):
    return pltpu.CompilerParams(dimension_semantics=semantics, vmem_limit_bytes=VMEM_LIMIT_BYTES)


def _pre_call(x2, seq, g_mix, w_in, b_gate, g_cq, w_q_up, g_ckv, w_kv_up):
    rows, d_model = x2.shape
    a_width = A_HEADS * A_HEAD_DIM
    o_q, o_k, o_v = 0, a_width, 2 * a_width
    o_cq = 3 * a_width
    o_ckv = o_cq + B_Q_RANK
    o_kr = o_ckv + B_KV_RANK
    o_gate = o_kr + B_ROPE
    w_in = w_in.astype(BF16)
    w_qkv = w_in[:, o_q:o_cq]
    w_cq = w_in[:, o_cq:o_ckv]
    w_ckv = w_in[:, o_ckv:o_kr]
    w_kr = jnp.pad(w_in[:, o_kr:o_gate], ((0, 0), (B_NOPE, HEAD_SLOT - B_QK)))
    w_gate = w_in[:, o_gate:]
    w_qup = jnp.pad(w_q_up.astype(BF16).reshape(B_Q_RANK, B_HEADS, B_QK),
                    ((0, 0), (0, 0), (0, HEAD_SLOT - B_QK))).reshape(B_Q_RANK, B_HEADS * HEAD_SLOT)
    w_kv = w_kv_up.astype(BF16).reshape(B_KV_RANK, B_HEADS, B_NOPE + B_V)
    w_kup = jnp.pad(w_kv[:, :, :B_NOPE], ((0, 0), (0, 0), (0, HEAD_SLOT - B_NOPE))).reshape(
        B_KV_RANK, B_HEADS * HEAD_SLOT)
    w_vup = w_kv[:, :, B_NOPE:].reshape(B_KV_RANK, B_HEADS * B_V)

    cos_a, sin_a = _rope_tables(seq, A_HEAD_DIM)
    cos_a, sin_a = jnp.tile(cos_a, (1, LANES // A_HEAD_DIM)), jnp.tile(sin_a, (1, LANES // A_HEAD_DIM))
    cos_b, sin_b = _rope_tables(seq, B_ROPE)
    pad = ((0, 0), (B_NOPE, HEAD_SLOT - B_QK))
    cos_b = jnp.pad(cos_b, pad, constant_values=1.0)
    sin_b = jnp.pad(sin_b, pad)

    n_tiles = rows // ROW_TILE
    seq_tiles = seq // ROW_TILE
    q_per_tile = ROW_TILE // Q_TILE
    n_q = rows // Q_TILE
    row_spec = lambda w: pl.BlockSpec((ROW_TILE, w), lambda r: (r, 0))
    table_spec = pl.BlockSpec((ROW_TILE, LANES), lambda r: (r % seq_tiles, 0))
    t_spec = lambda h: pl.BlockSpec((q_per_tile, h, Q_TILE), lambda r: (r, 0, 0))
    vec = lambda a: a.reshape(1, -1).astype(F32)
    operands = [
        (x2, row_spec(d_model)), (vec(g_mix), _resident((1, d_model))),
        (cos_a, table_spec), (sin_a, table_spec), (cos_b, table_spec), (sin_b, table_spec),
        (w_qkv, _resident(w_qkv.shape)), (w_cq, _resident(w_cq.shape)), (w_ckv, _resident(w_ckv.shape)),
        (w_kr, _resident(w_kr.shape)), (w_gate, _resident(w_gate.shape)),
        (vec(b_gate), _resident((1, 2 * d_model))),
        (vec(g_cq), _resident((1, B_Q_RANK))), (w_qup, _resident(w_qup.shape)),
        (vec(g_ckv), _resident((1, B_KV_RANK))), (w_kup, _resident(w_kup.shape)),
        (w_vup, _resident(w_vup.shape)),
    ]
    out_shape = [
        jax.ShapeDtypeStruct((n_q, A_HEADS * HEAD_SLOT, Q_TILE), BF16),
        jax.ShapeDtypeStruct((rows, a_width), BF16),
        jax.ShapeDtypeStruct((n_q, a_width, Q_TILE), BF16),
        jax.ShapeDtypeStruct((n_tiles, ROW_TILE // MOBA_BLOCK, a_width), F32),
        jax.ShapeDtypeStruct((rows, 2 * d_model), BF16),
        jax.ShapeDtypeStruct((n_q, B_HEADS * HEAD_SLOT, Q_TILE), BF16),
        jax.ShapeDtypeStruct((rows, B_HEADS * HEAD_SLOT), BF16),
        jax.ShapeDtypeStruct((n_q, B_HEADS * B_V, Q_TILE), BF16),
    ]
    out_specs = [
        t_spec(A_HEADS * HEAD_SLOT), row_spec(a_width), t_spec(a_width),
        pl.BlockSpec((1, ROW_TILE // MOBA_BLOCK, a_width), lambda r: (r, 0, 0)),
        row_spec(2 * d_model), t_spec(B_HEADS * HEAD_SLOT), row_spec(B_HEADS * HEAD_SLOT),
        t_spec(B_HEADS * B_V),
    ]
    return pl.pallas_call(
        _pre_kernel,
        grid=(n_tiles,),
        in_specs=[s for _, s in operands],
        out_specs=out_specs,
        out_shape=out_shape,
        compiler_params=_params("parallel"),
        name="pre_proj",
    )(*[a for a, _ in operands])


def _attn_call(q_t, k, v_t, kmean, *, batch, seq, moba):
    nblk = seq // Q_TILE
    heads = q_t.shape[1] // HEAD_SLOT
    groups = heads // HEADS_PER_STEP
    vdim = v_t.shape[2] // heads
    key_lanes = k.shape[2] // groups
    in_specs = [
        pl.BlockSpec((None, HEADS_PER_STEP * HEAD_SLOT, Q_TILE), lambda b, g, i: (b * nblk + i, g, 0)),
        pl.BlockSpec((None, seq, key_lanes), lambda b, g, i: (b, 0, g), pipeline_mode=pl.Buffered(1)),
        pl.BlockSpec((None, nblk, HEADS_PER_STEP * vdim, K_TILE), lambda b, g, i: (b, 0, g, 0),
                     pipeline_mode=pl.Buffered(1)),
    ]
    operands = [q_t, k, v_t]
    stat = pltpu.VMEM((HEADS_PER_STEP, 1, Q_TILE), F32)
    scratch = [
        pltpu.VMEM((HEADS_PER_STEP, K_TILE, Q_TILE), F32),
        stat,
        pltpu.VMEM((HEADS_PER_STEP, K_TILE, Q_TILE), BF16),
        stat,
        stat,
        pltpu.VMEM((HEADS_PER_STEP, vdim + SUM_ROWS, Q_TILE), F32),
    ]
    if moba:
        in_specs.append(pl.BlockSpec((None, nblk, key_lanes), lambda b, g, i: (b, 0, g)))
        operands.append(kmean)
        scratch.append(pltpu.VMEM((HEADS_PER_STEP, nblk, Q_TILE), F32))
        scratch.append(stat)
    return pl.pallas_call(
        functools.partial(_attn_kernel, moba=moba),
        grid=(batch, groups, nblk),
        in_specs=in_specs,
        out_specs=pl.BlockSpec((None, HEADS_PER_STEP * vdim, Q_TILE), lambda b, g, i: (b * nblk + i, g, 0)),
        out_shape=jax.ShapeDtypeStruct((batch * nblk, heads * vdim, Q_TILE), BF16),
        scratch_shapes=scratch,
        compiler_params=_params("parallel", "parallel", "arbitrary"),
        name="moba_attn" if moba else "mla_attn",
    )(*operands)


def _post_call(x2, oa_t, ob_t, gates, w_o_a, w_o_b, w_out, g_mlp, w_ff1, w_ff2, g_final, final_norm):
    rows, d_model = x2.shape
    q_per_tile = ROW_TILE // Q_TILE
    row_spec = lambda w: pl.BlockSpec((ROW_TILE, w), lambda r: (r, 0))
    t_spec = lambda a: pl.BlockSpec((q_per_tile,) + a.shape[1:], lambda r: (r, 0, 0))
    vec = lambda a: a.reshape(1, -1).astype(F32)
    weights = [w.astype(BF16) for w in (w_o_a, w_o_b, w_out)]
    operands = [(x2, row_spec(d_model)), (oa_t, t_spec(oa_t)), (ob_t, t_spec(ob_t)),
                (gates, row_spec(2 * d_model))]
    operands += [(w, _resident(w.shape)) for w in weights]
    operands += [(vec(g_mlp), _resident((1, d_model))),
                 (w_ff1.astype(BF16), _resident(w_ff1.shape)), (w_ff2.astype(BF16), _resident(w_ff2.shape)),
                 (vec(g_final), _resident((1, d_model)))]
    return pl.pallas_call(
        functools.partial(_post_kernel, ff_chunk=1024, final_norm=final_norm),
        grid=(rows // ROW_TILE,),
        in_specs=[s for _, s in operands],
        out_specs=row_spec(d_model),
        out_shape=jax.ShapeDtypeStruct((rows, d_model), F32),
        compiler_params=_params("parallel"),
        name="post_mlp",
    )(*[a for a, _ in operands])


def kernel(x, g_mix, w_in, b_gate, g_cq, w_q_up, g_ckv, w_kv_up, w_o_a, w_o_b, w_out, g_mlp, w_ff1,
           w_ff2, g_final):
    batch, seq, d_model = x.shape
    depth = w_in.shape[0]
    assert seq % ROW_TILE == 0 and ROW_TILE % MOBA_BLOCK == 0
    assert A_HEADS % HEADS_PER_STEP == 0 and B_HEADS % HEADS_PER_STEP == 0 and HEADS_PER_STEP % 2 == 0
    nblk = seq // Q_TILE
    x2 = x.reshape(batch * seq, d_model)
    for l in range(depth):
        qa_t, ka, va_t, kmean, gates, qb_t, kb, vb_t = _pre_call(
            x2, seq, g_mix[l], w_in[l], b_gate[l], g_cq[l], w_q_up[l], g_ckv[l], w_kv_up[l])
        oa_t = _attn_call(qa_t, ka.reshape(batch, seq, -1), va_t.reshape(batch, nblk, -1, K_TILE),
                          kmean.reshape(batch, nblk, -1), batch=batch, seq=seq, moba=True)
        ob_t = _attn_call(qb_t, kb.reshape(batch, seq, -1), vb_t.reshape(batch, nblk, -1, K_TILE),
                          None, batch=batch, seq=seq, moba=False)
        x2 = _post_call(x2, oa_t, ob_t, gates, w_o_a[l], w_o_b[l], w_out[l], g_mlp[l], w_ff1[l],
                        w_ff2[l], g_final, final_norm=(l == depth - 1))
    return x2.reshape(batch, seq, d_model)
```

```python
import functools

import jax
import jax.numpy as jnp
from jax import lax
from jax.experimental import pallas as pl
from jax.experimental.pallas import tpu as pltpu

EPS = 1e-6
ROPE_THETA = 10000.0

A_HEADS = 8
A_HEAD_DIM = 64
MOBA_BLOCK = 256
MOBA_TOPK = 3

B_HEADS = 8
B_NOPE = 64
B_ROPE = 32
B_QK = B_NOPE + B_ROPE
B_V = 64
B_Q_RANK = 384
B_KV_RANK = 256

LANES = 128
HEAD_SLOT = 128
ROW_TILE = 512
Q_TILE = MOBA_BLOCK
K_TILE = MOBA_BLOCK
HEADS_PER_STEP = 8
MASK_VALUE = -1e30
LOG2_E = 1.4426950408889634
SUM_ROWS = 16
VMEM_LIMIT_BYTES = 56 * 1024 * 1024

F32 = jnp.float32
BF16 = jnp.bfloat16


def _rms(x, g):
    return x * lax.rsqrt(jnp.mean(x * x, axis=-1, keepdims=True) + EPS) * g


def _rope_lanes(x, cos, sin_signed):
    groups = []
    for g in range(x.shape[1] // LANES):
        xg = x[:, g * LANES:(g + 1) * LANES]
        groups.append(xg * cos + pltpu.roll(xg, LANES // 2, axis=1) * sin_signed)
    return groups


def _store_transposed(dst_ref, row0, t):
    rows = t.shape[0]
    for qt in range(ROW_TILE // Q_TILE):
        dst_ref[qt, row0:row0 + rows, :] = t[:, qt * Q_TILE:(qt + 1) * Q_TILE].astype(dst_ref.dtype)


def _pre_kernel(x_ref, gmix_ref, cos_a_ref, sin_a_ref, cos_b_ref, sin_b_ref,
                w_qkv_ref, w_cq_ref, w_ckv_ref, w_kr_ref, w_gate_ref, b_gate_ref,
                g_cq_ref, w_qup_ref, g_ckv_ref, w_kup_ref, w_vup_ref,
                qa_t_ref, ka_ref, va_t_ref, kmean_ref, gate_ref, qb_t_ref, kb_ref, vb_t_ref):
    a_width = A_HEADS * A_HEAD_DIM
    h = _rms(x_ref[...], gmix_ref[...]).astype(BF16)

    qkv = jnp.dot(h, w_qkv_ref[...], preferred_element_type=F32)
    cos_a, sin_a = cos_a_ref[...], sin_a_ref[...]
    q_groups = _rope_lanes(qkv[:, :a_width], cos_a, sin_a)
    k_groups = _rope_lanes(qkv[:, a_width:2 * a_width], cos_a, sin_a)
    first_head = (lax.broadcasted_iota(jnp.int32, (LANES, ROW_TILE), 0) % A_HEAD_DIM) < A_HEAD_DIM // 2
    for g, qg in enumerate(q_groups):
        t = (qg * (A_HEAD_DIM ** -0.5 * LOG2_E)).T
        _store_transposed(qa_t_ref, (2 * g) * HEAD_SLOT, jnp.where(first_head, t, 0.0))
        _store_transposed(qa_t_ref, (2 * g + 1) * HEAD_SLOT, jnp.where(first_head, 0.0, t))
    for g, kg in enumerate(k_groups):
        ka_ref[:, g * LANES:(g + 1) * LANES] = kg.astype(BF16)
        for blk in range(ROW_TILE // MOBA_BLOCK):
            kmean_ref[0, blk:blk + 1, g * LANES:(g + 1) * LANES] = jnp.mean(
                kg[blk * MOBA_BLOCK:(blk + 1) * MOBA_BLOCK], axis=0, keepdims=True)
    for g in range(a_width // LANES):
        vg = qkv[:, 2 * a_width + g * LANES:2 * a_width + (g + 1) * LANES]
        _store_transposed(va_t_ref, g * LANES, vg.T)

    gate = jnp.dot(h, w_gate_ref[...], preferred_element_type=F32) + b_gate_ref[...]
    gate_ref[...] = jax.nn.sigmoid(gate).astype(BF16)

    cos_b, sin_b = cos_b_ref[...], sin_b_ref[...]
    cq = jnp.dot(h, w_cq_ref[...], preferred_element_type=F32)
    cqn = _rms(cq, g_cq_ref[...]).astype(BF16)
    qb = jnp.dot(cqn, w_qup_ref[...], preferred_element_type=F32)
    for hd, qg in enumerate(_rope_lanes(qb, cos_b, sin_b)):
        _store_transposed(qb_t_ref, hd * HEAD_SLOT, (qg * (B_QK ** -0.5 * LOG2_E)).T)
    ckv = jnp.dot(h, w_ckv_ref[...], preferred_element_type=F32)
    ckvn = _rms(ckv, g_ckv_ref[...]).astype(BF16)
    k_nope = jnp.dot(ckvn, w_kup_ref[...], preferred_element_type=F32)
    kr = jnp.dot(h, w_kr_ref[...], preferred_element_type=F32)
    kr = _rope_lanes(kr, cos_b, sin_b)[0]
    for hd in range(B_HEADS):
        kb_ref[:, hd * HEAD_SLOT:(hd + 1) * HEAD_SLOT] = (
            k_nope[:, hd * HEAD_SLOT:(hd + 1) * HEAD_SLOT] + kr).astype(BF16)
    vb = jnp.dot(ckvn, w_vup_ref[...], preferred_element_type=F32)
    for g in range(B_HEADS * B_V // LANES):
        _store_transposed(vb_t_ref, g * LANES, vb[:, g * LANES:(g + 1) * LANES].T)


def _attn_kernel(*refs, moba):
    if moba:
        q_t_ref, k_ref, v_t_ref, kmean_ref, o_t_ref = refs[:5]
        s_scr, smax_scr, p_scr, alpha_scr, m_scr, acc_scr, bias_scr, sbias_scr = refs[5:]
    else:
        q_t_ref, k_ref, v_t_ref, o_t_ref = refs[:4]
        s_scr, smax_scr, p_scr, alpha_scr, m_scr, acc_scr = refs[4:]
    i = pl.program_id(2)
    heads = HEADS_PER_STEP
    vdim = v_t_ref.shape[1] // heads

    def key_lanes(hd):
        g = hd // 2 if moba else hd
        return slice(g * LANES, (g + 1) * LANES)

    def q_rows(hd):
        return slice(hd * HEAD_SLOT, (hd + 1) * HEAD_SLOT)

    if moba:
        nblk = kmean_ref.shape[0]
        blk = lax.broadcasted_iota(jnp.int32, (nblk, Q_TILE), 0).astype(F32)
        past = blk < i.astype(F32)
        for hd in range(heads):
            gate = jnp.dot(kmean_ref[:, key_lanes(hd)].astype(BF16), q_t_ref[q_rows(hd), :],
                           preferred_element_type=F32)
            gate = jnp.where(past, gate, -jnp.inf)
            bias = jnp.full(gate.shape, MASK_VALUE, F32)
            for _ in range(MOBA_TOPK):
                best = jnp.max(gate, axis=0, keepdims=True)
                first = jnp.min(jnp.where(gate == best, blk, float(nblk)), axis=0, keepdims=True)
                pick = blk == first
                bias = jnp.where(pick, 0.0, bias)
                gate = jnp.where(pick, -jnp.inf, gate)
            bias_scr[hd] = jnp.where(past, bias, MASK_VALUE)

    def qk_stage(j, diagonal):
        start = pl.multiple_of(j * K_TILE, K_TILE)
        k_blk = k_ref[pl.ds(start, K_TILE), :]
        for hd in range(heads):
            s = jnp.dot(k_blk[:, key_lanes(hd)], q_t_ref[q_rows(hd), :],
                        preferred_element_type=F32)
            if diagonal:
                key_pos = lax.broadcasted_iota(jnp.int32, s.shape, 0)
                qry_pos = lax.broadcasted_iota(jnp.int32, s.shape, 1)
                s = jnp.where(key_pos <= qry_pos, s, MASK_VALUE)
            s_scr[hd] = s
            smax = jnp.max(s, axis=0, keepdims=True)
            if moba:
                bias = jnp.zeros_like(smax) if diagonal else bias_scr[hd, pl.ds(j, 1), :]
                sbias_scr[hd] = bias
                smax = smax + bias
            smax_scr[hd] = smax

    ones_rows = jnp.ones((SUM_ROWS, K_TILE), BF16)

    def softmax_stage():
        for hd in range(heads):
            m_old = m_scr[hd]
            m_new = jnp.maximum(m_old, smax_scr[hd])
            shift = m_new - sbias_scr[hd] if moba else m_new
            p_scr[hd] = jnp.exp2(s_scr[hd] - shift).astype(BF16)
            alpha_scr[hd] = jnp.exp2(m_old - m_new)
            m_scr[hd] = m_new

    def pv_stage(j):
        v_t = v_t_ref[j]
        for hd in range(heads):
            v_ext = jnp.concatenate([v_t[hd * vdim:(hd + 1) * vdim], ones_rows], axis=0)
            acc_scr[hd] = alpha_scr[hd] * acc_scr[hd] + jnp.dot(
                v_ext, p_scr[hd], preferred_element_type=F32)

    def previous_block(t):
        return jnp.where(t <= 1, i, t - 2)

    m_scr[...] = jnp.full(m_scr.shape, MASK_VALUE, F32)
    acc_scr[...] = jnp.zeros(acc_scr.shape, F32)
    alpha_scr[...] = jnp.ones(alpha_scr.shape, F32)
    p_scr[...] = jnp.zeros(p_scr.shape, BF16)
    qk_stage(i, diagonal=True)

    def body(t, carry):
        pv_stage(previous_block(t))
        softmax_stage()
        qk_stage(t, diagonal=False)
        return carry

    lax.fori_loop(0, i, body, 0)
    pv_stage(previous_block(i))
    softmax_stage()
    pv_stage(jnp.maximum(i - 1, 0))

    for hd in range(heads):
        acc = acc_scr[hd]
        o_t_ref[hd * vdim:(hd + 1) * vdim, :] = (acc[:vdim] / acc[vdim:vdim + 1]).astype(o_t_ref.dtype)


def _post_kernel(x_ref, oa_t_ref, ob_t_ref, gate_ref, w_oa_ref, w_ob_ref, w_out_ref,
                 g_mlp_ref, w_ff1_ref, w_ff2_ref, g_final_ref, out_ref, *, ff_chunk, final_norm):
    d_model = x_ref.shape[1]
    contract_rows = (((0,), (0,)), ((), ()))
    ya, yb = [], []
    for qt in range(ROW_TILE // Q_TILE):
        ya.append(lax.dot_general(oa_t_ref[qt], w_oa_ref[...], contract_rows, preferred_element_type=F32))
        yb.append(lax.dot_general(ob_t_ref[qt], w_ob_ref[...], contract_rows, preferred_element_type=F32))
    ya = jnp.concatenate(ya, axis=0)
    yb = jnp.concatenate(yb, axis=0)
    mixed = gate_ref[:, :d_model].astype(F32) * ya + gate_ref[:, d_model:].astype(F32) * yb
    x1 = x_ref[...] + jnp.dot(mixed.astype(BF16), w_out_ref[...], preferred_element_type=F32)

    h2 = _rms(x1, g_mlp_ref[...]).astype(BF16)
    mlp = jnp.zeros_like(x1)
    for c in range(w_ff1_ref.shape[1] // ff_chunk):
        u = jnp.maximum(jnp.dot(h2, w_ff1_ref[:, c * ff_chunk:(c + 1) * ff_chunk],
                                preferred_element_type=F32), 0.0)
        mlp = mlp + jnp.dot((u * u).astype(BF16), w_ff2_ref[c * ff_chunk:(c + 1) * ff_chunk, :],
                            preferred_element_type=F32)
    x2 = x1 + mlp
    out_ref[...] = _rms(x2, g_final_ref[...]) if final_norm else x2


def _resident(shape):
    return pl.BlockSpec(shape, lambda *_: (0,) * len(shape), pipeline_mode=pl.Buffered(1))


def _rope_tables(seq, dim):
    half = dim // 2
    inv_freq = jnp.power(ROPE_THETA, -jnp.arange(half, dtype=F32) / half)
    ang = jnp.arange(seq, dtype=F32)[:, None] * inv_freq[None, :]
    return jnp.cos(ang), jnp.sin(ang)


def _moba_group_columns():
    half = A_HEAD_DIM // 2
    order = []
    for pair in range(A_HEADS // 2):
        h0, h1 = 2 * pair * A_HEAD_DIM, (2 * pair + 1) * A_HEAD_DIM
        for start in (h0, h1, h0 + half, h1 + half):
            order.extend(range(start, start + half))
    return jnp.asarray(order, jnp.int32)


def _mla_slot_columns(nope, rope):
    r = B_ROPE // 2
    split = LANES // 2 - r
    pad = jnp.zeros(nope.shape[:-1] + (HEAD_SLOT - B_QK,), nope.dtype)
    return jnp.concatenate([rope[..., :r], nope[..., :split], rope[..., r:], nope[..., split:], pad], axis=-1)


def _params(*semantics):
    return pltpu.CompilerParams(dimension_semantics=semantics, vmem_limit_bytes=VMEM_LIMIT_BYTES)


def _pre_call(x2, seq, g_mix, w_in, b_gate, g_cq, w_q_up, g_ckv, w_kv_up):
    rows, d_model = x2.shape
    a_width = A_HEADS * A_HEAD_DIM
    o_q, o_k, o_v = 0, a_width, 2 * a_width
    o_cq = 3 * a_width
    o_ckv = o_cq + B_Q_RANK
    o_kr = o_ckv + B_KV_RANK
    o_gate = o_kr + B_ROPE
    w_in = w_in.astype(BF16)
    cols = _moba_group_columns()
    w_qkv = jnp.concatenate([w_in[:, o_q:o_k][:, cols], w_in[:, o_k:o_v][:, cols], w_in[:, o_v:o_cq]], axis=1)
    w_cq = w_in[:, o_cq:o_ckv]
    w_ckv = w_in[:, o_ckv:o_kr]
    w_kr = _mla_slot_columns(jnp.zeros((d_model, B_NOPE), BF16), w_in[:, o_kr:o_gate])
    w_gate = w_in[:, o_gate:]
    w_q = w_q_up.astype(BF16).reshape(B_Q_RANK, B_HEADS, B_QK)
    w_qup = _mla_slot_columns(w_q[..., :B_NOPE], w_q[..., B_NOPE:]).reshape(B_Q_RANK, B_HEADS * HEAD_SLOT)
    w_kv = w_kv_up.astype(BF16).reshape(B_KV_RANK, B_HEADS, B_NOPE + B_V)
    w_kup = _mla_slot_columns(w_kv[..., :B_NOPE], jnp.zeros((B_KV_RANK, B_HEADS, B_ROPE), BF16)).reshape(
        B_KV_RANK, B_HEADS * HEAD_SLOT)
    w_vup = w_kv[:, :, B_NOPE:].reshape(B_KV_RANK, B_HEADS * B_V)

    cos_a, sin_a = _rope_tables(seq, A_HEAD_DIM)
    cos_a = jnp.tile(cos_a, (1, LANES // (A_HEAD_DIM // 2)))
    sin_a = jnp.concatenate([-sin_a, -sin_a, sin_a, sin_a], axis=1)
    cos_b, sin_b = _rope_tables(seq, B_ROPE)
    cos_b = _mla_slot_columns(jnp.ones((seq, B_NOPE), F32), jnp.concatenate([cos_b, cos_b], axis=1))
    sin_b = _mla_slot_columns(jnp.zeros((seq, B_NOPE), F32), jnp.concatenate([-sin_b, sin_b], axis=1))

    n_tiles = rows // ROW_TILE
    seq_tiles = seq // ROW_TILE
    q_per_tile = ROW_TILE // Q_TILE
    n_q = rows // Q_TILE
    row_spec = lambda w: pl.BlockSpec((ROW_TILE, w), lambda r: (r, 0))
    table_spec = pl.BlockSpec((ROW_TILE, LANES), lambda r: (r % seq_tiles, 0))
    t_spec = lambda h: pl.BlockSpec((q_per_tile, h, Q_TILE), lambda r: (r, 0, 0))
    vec = lambda a: a.reshape(1, -1).astype(F32)
    operands = [
        (x2, row_spec(d_model)), (vec(g_mix), _resident((1, d_model))),
        (cos_a, table_spec), (sin_a, table_spec), (cos_b, table_spec), (sin_b, table_spec),
        (w_qkv, _resident(w_qkv.shape)), (w_cq, _resident(w_cq.shape)), (w_ckv, _resident(w_ckv.shape)),
        (w_kr, _resident(w_kr.shape)), (w_gate, _resident(w_gate.shape)),
        (vec(b_gate), _resident((1, 2 * d_model))),
        (vec(g_cq), _resident((1, B_Q_RANK))), (w_qup, _resident(w_qup.shape)),
        (vec(g_ckv), _resident((1, B_KV_RANK))), (w_kup, _resident(w_kup.shape)),
        (w_vup, _resident(w_vup.shape)),
    ]
    out_shape = [
        jax.ShapeDtypeStruct((n_q, A_HEADS * HEAD_SLOT, Q_TILE), BF16),
        jax.ShapeDtypeStruct((rows, a_width), BF16),
        jax.ShapeDtypeStruct((n_q, a_width, Q_TILE), BF16),
        jax.ShapeDtypeStruct((n_tiles, ROW_TILE // MOBA_BLOCK, a_width), F32),
        jax.ShapeDtypeStruct((rows, 2 * d_model), BF16),
        jax.ShapeDtypeStruct((n_q, B_HEADS * HEAD_SLOT, Q_TILE), BF16),
        jax.ShapeDtypeStruct((rows, B_HEADS * HEAD_SLOT), BF16),
        jax.ShapeDtypeStruct((n_q, B_HEADS * B_V, Q_TILE), BF16),
    ]
    out_specs = [
        t_spec(A_HEADS * HEAD_SLOT), row_spec(a_width), t_spec(a_width),
        pl.BlockSpec((1, ROW_TILE // MOBA_BLOCK, a_width), lambda r: (r, 0, 0)),
        row_spec(2 * d_model), t_spec(B_HEADS * HEAD_SLOT), row_spec(B_HEADS * HEAD_SLOT),
        t_spec(B_HEADS * B_V),
    ]
    return pl.pallas_call(
        _pre_kernel,
        grid=(n_tiles,),
        in_specs=[s for _, s in operands],
        out_specs=out_specs,
        out_shape=out_shape,
        compiler_params=_params("parallel"),
        name="pre_proj",
    )(*[a for a, _ in operands])


def _attn_call(q_t, k, v_t, kmean, *, batch, seq, moba):
    nblk = seq // Q_TILE
    heads = q_t.shape[1] // HEAD_SLOT
    groups = heads // HEADS_PER_STEP
    vdim = v_t.shape[2] // heads
    key_lanes = k.shape[2] // groups
    kv_mode = None if moba else pl.Buffered(1)
    in_specs = [
        pl.BlockSpec((None, HEADS_PER_STEP * HEAD_SLOT, Q_TILE), lambda b, g, i: (b * nblk + i, g, 0)),
        pl.BlockSpec((None, seq, key_lanes), lambda b, g, i: (b, 0, g), pipeline_mode=kv_mode),
        pl.BlockSpec((None, nblk, HEADS_PER_STEP * vdim, K_TILE), lambda b, g, i: (b, 0, g, 0),
                     pipeline_mode=kv_mode),
    ]
    operands = [q_t, k, v_t]
    stat = pltpu.VMEM((HEADS_PER_STEP, 1, Q_TILE), F32)
    scratch = [
        pltpu.VMEM((HEADS_PER_STEP, K_TILE, Q_TILE), F32),
        stat,
        pltpu.VMEM((HEADS_PER_STEP, K_TILE, Q_TILE), BF16),
        stat,
        stat,
        pltpu.VMEM((HEADS_PER_STEP, vdim + SUM_ROWS, Q_TILE), F32),
    ]
    if moba:
        in_specs.append(pl.BlockSpec((None, nblk, key_lanes), lambda b, g, i: (b, 0, g)))
        operands.append(kmean)
        scratch.append(pltpu.VMEM((HEADS_PER_STEP, nblk, Q_TILE), F32))
        scratch.append(stat)
    return pl.pallas_call(
        functools.partial(_attn_kernel, moba=moba),
        grid=(batch, groups, nblk),
        in_specs=in_specs,
        out_specs=pl.BlockSpec((None, HEADS_PER_STEP * vdim, Q_TILE), lambda b, g, i: (b * nblk + i, g, 0)),
        out_shape=jax.ShapeDtypeStruct((batch * nblk, heads * vdim, Q_TILE), BF16),
        scratch_shapes=scratch,
        compiler_params=_params("parallel", "parallel", "arbitrary"),
        name="moba_attn" if moba else "mla_attn",
    )(*operands)


def _post_call(x2, oa_t, ob_t, gates, w_o_a, w_o_b, w_out, g_mlp, w_ff1, w_ff2, g_final, final_norm):
    rows, d_model = x2.shape
    q_per_tile = ROW_TILE // Q_TILE
    row_spec = lambda w: pl.BlockSpec((ROW_TILE, w), lambda r: (r, 0))
    t_spec = lambda a: pl.BlockSpec((q_per_tile,) + a.shape[1:], lambda r: (r, 0, 0))
    vec = lambda a: a.reshape(1, -1).astype(F32)
    weights = [w.astype(BF16) for w in (w_o_a, w_o_b, w_out)]
    operands = [(x2, row_spec(d_model)), (oa_t, t_spec(oa_t)), (ob_t, t_spec(ob_t)),
                (gates, row_spec(2 * d_model))]
    operands += [(w, _resident(w.shape)) for w in weights]
    operands += [(vec(g_mlp), _resident((1, d_model))),
                 (w_ff1.astype(BF16), _resident(w_ff1.shape)), (w_ff2.astype(BF16), _resident(w_ff2.shape)),
                 (vec(g_final), _resident((1, d_model)))]
    return pl.pallas_call(
        functools.partial(_post_kernel, ff_chunk=1024, final_norm=final_norm),
        grid=(rows // ROW_TILE,),
        in_specs=[s for _, s in operands],
        out_specs=row_spec(d_model),
        out_shape=jax.ShapeDtypeStruct((rows, d_model), F32),
        compiler_params=_params("parallel"),
        name="post_mlp",
    )(*[a for a, _ in operands])


def kernel(x, g_mix, w_in, b_gate, g_cq, w_q_up, g_ckv, w_kv_up, w_o_a, w_o_b, w_out, g_mlp, w_ff1,
           w_ff2, g_final):
    batch, seq, d_model = x.shape
    depth = w_in.shape[0]
    assert seq % ROW_TILE == 0 and ROW_TILE % MOBA_BLOCK == 0
    assert A_HEADS % HEADS_PER_STEP == 0 and B_HEADS % HEADS_PER_STEP == 0 and HEADS_PER_STEP % 2 == 0
    nblk = seq // Q_TILE
    x2 = x.reshape(batch * seq, d_model)
    for l in range(depth):
        qa_t, ka, va_t, kmean, gates, qb_t, kb, vb_t = _pre_call(
            x2, seq, g_mix[l], w_in[l], b_gate[l], g_cq[l], w_q_up[l], g_ckv[l], w_kv_up[l])
        oa_t = _attn_call(qa_t, ka.reshape(batch, seq, -1), va_t.reshape(batch, nblk, -1, K_TILE),
                          kmean.reshape(batch, nblk, -1), batch=batch, seq=seq, moba=True)
        ob_t = _attn_call(qb_t, kb.reshape(batch, seq, -1), vb_t.reshape(batch, nblk, -1, K_TILE),
                          None, batch=batch, seq=seq, moba=False)
        x2 = _post_call(x2, oa_t, ob_t, gates, w_o_a[l], w_o_b[l], w_out[l], g_mlp[l], w_ff1[l],
                        w_ff2[l], g_final, final_norm=(l == depth - 1))
    return x2.reshape(batch, seq, d_model)
```

```python
import functools

import jax
import jax.numpy as jnp
import numpy as np
from jax import lax
from jax.experimental import pallas as pl
from jax.experimental.pallas import tpu as pltpu

EPS = 1e-6
ROPE_THETA = 10000.0

A_HEADS = 8
A_HEAD_DIM = 64
MOBA_BLOCK = 256
MOBA_TOPK = 3

B_HEADS = 8
B_NOPE = 64
B_ROPE = 32
B_QK = B_NOPE + B_ROPE
B_V = 64
B_Q_RANK = 384
B_KV_RANK = 256

LANES = 128
HEAD_SLOT = 128
ROW_TILE = 512
Q_TILE = MOBA_BLOCK
K_TILE = MOBA_BLOCK
HEADS_PER_STEP = 8
MASK_VALUE = -1e30
LOG2_E = 1.4426950408889634
SUM_ROWS = 16
VMEM_LIMIT_BYTES = 56 * 1024 * 1024

F32 = jnp.float32
BF16 = jnp.bfloat16


def _rms(x, g):
    return x * lax.rsqrt(jnp.mean(x * x, axis=-1, keepdims=True) + EPS) * g


def _rope_lanes(x, cos, sin_signed):
    groups = []
    for g in range(x.shape[1] // LANES):
        xg = x[:, g * LANES:(g + 1) * LANES]
        groups.append(xg * cos + pltpu.roll(xg, LANES // 2, axis=1) * sin_signed)
    return groups


def _store_transposed(dst_ref, row0, t):
    rows = t.shape[0]
    for qt in range(ROW_TILE // Q_TILE):
        dst_ref[qt, row0:row0 + rows, :] = t[:, qt * Q_TILE:(qt + 1) * Q_TILE].astype(dst_ref.dtype)


def _pre_kernel(x_ref, gmix_ref, cos_a_ref, sin_a_ref, cos_b_ref, sin_b_ref,
                w_qkv_ref, w_cq_ref, w_ckv_ref, w_kr_ref, w_gate_ref, b_gate_ref,
                g_cq_ref, w_qup_ref, g_ckv_ref, w_kup_ref, w_vup_ref,
                qa_t_ref, ka_ref, va_t_ref, kmean_ref, gate_ref, qb_t_ref, kb_ref, vb_t_ref):
    a_width = A_HEADS * A_HEAD_DIM
    h = _rms(x_ref[...], gmix_ref[...]).astype(BF16)

    qkv = jnp.dot(h, w_qkv_ref[...], preferred_element_type=F32)
    cos_a, sin_a = cos_a_ref[...], sin_a_ref[...]
    q_groups = _rope_lanes(qkv[:, :a_width], cos_a, sin_a)
    k_groups = _rope_lanes(qkv[:, a_width:2 * a_width], cos_a, sin_a)
    first_head = (lax.broadcasted_iota(jnp.int32, (LANES, ROW_TILE), 0) % A_HEAD_DIM) < A_HEAD_DIM // 2
    for g, qg in enumerate(q_groups):
        t = (qg * (A_HEAD_DIM ** -0.5 * LOG2_E)).T
        _store_transposed(qa_t_ref, (2 * g) * HEAD_SLOT, jnp.where(first_head, t, 0.0))
        _store_transposed(qa_t_ref, (2 * g + 1) * HEAD_SLOT, jnp.where(first_head, 0.0, t))
    for g, kg in enumerate(k_groups):
        ka_ref[:, g * LANES:(g + 1) * LANES] = kg.astype(BF16)
        for blk in range(ROW_TILE // MOBA_BLOCK):
            kmean_ref[0, blk:blk + 1, g * LANES:(g + 1) * LANES] = jnp.mean(
                kg[blk * MOBA_BLOCK:(blk + 1) * MOBA_BLOCK], axis=0, keepdims=True)
    for g in range(a_width // LANES):
        vg = qkv[:, 2 * a_width + g * LANES:2 * a_width + (g + 1) * LANES]
        _store_transposed(va_t_ref, g * LANES, vg.T)

    gate = jnp.dot(h, w_gate_ref[...], preferred_element_type=F32) + b_gate_ref[...]
    gate_ref[...] = jax.nn.sigmoid(gate).astype(BF16)

    cos_b, sin_b = cos_b_ref[...], sin_b_ref[...]
    cq = jnp.dot(h, w_cq_ref[...], preferred_element_type=F32)
    cqn = _rms(cq, g_cq_ref[...]).astype(BF16)
    qb = jnp.dot(cqn, w_qup_ref[...], preferred_element_type=F32)
    for hd, qg in enumerate(_rope_lanes(qb, cos_b, sin_b)):
        _store_transposed(qb_t_ref, hd * HEAD_SLOT, (qg * (B_QK ** -0.5 * LOG2_E)).T)
    ckv = jnp.dot(h, w_ckv_ref[...], preferred_element_type=F32)
    ckvn = _rms(ckv, g_ckv_ref[...]).astype(BF16)
    k_nope = jnp.dot(ckvn, w_kup_ref[...], preferred_element_type=F32)
    kr = jnp.dot(h, w_kr_ref[...], preferred_element_type=F32)
    kr = _rope_lanes(kr, cos_b, sin_b)[0]
    for hd in range(B_HEADS):
        kb_ref[:, hd * HEAD_SLOT:(hd + 1) * HEAD_SLOT] = (
            k_nope[:, hd * HEAD_SLOT:(hd + 1) * HEAD_SLOT] + kr).astype(BF16)
    vb = jnp.dot(ckvn, w_vup_ref[...], preferred_element_type=F32)
    for g in range(B_HEADS * B_V // LANES):
        _store_transposed(vb_t_ref, g * LANES, vb[:, g * LANES:(g + 1) * LANES].T)


def _attn_kernel(*refs, moba):
    if moba:
        q_t_ref, k_ref, v_t_ref, kmean_ref, o_t_ref = refs[:5]
        s_scr, smax_scr, p_scr, alpha_scr, m_scr, acc_scr, bias_scr, sbias_scr = refs[5:]
    else:
        q_t_ref, k_ref, v_t_ref, o_t_ref = refs[:4]
        s_scr, smax_scr, p_scr, alpha_scr, m_scr, acc_scr = refs[4:]
    i = pl.program_id(2)
    heads = HEADS_PER_STEP
    vdim = v_t_ref.shape[1] // heads

    def key_lanes(hd):
        g = hd // 2 if moba else hd
        return slice(g * LANES, (g + 1) * LANES)

    def q_rows(hd):
        return slice(hd * HEAD_SLOT, (hd + 1) * HEAD_SLOT)

    if moba:
        nblk = kmean_ref.shape[0]
        blk = lax.broadcasted_iota(jnp.int32, (nblk, Q_TILE), 0).astype(F32)
        past = blk < i.astype(F32)
        for hd in range(heads):
            gate = jnp.dot(kmean_ref[:, key_lanes(hd)].astype(BF16), q_t_ref[q_rows(hd), :],
                           preferred_element_type=F32)
            gate = jnp.where(past, gate, -jnp.inf)
            bias = jnp.full(gate.shape, MASK_VALUE, F32)
            for _ in range(MOBA_TOPK):
                best = jnp.max(gate, axis=0, keepdims=True)
                first = jnp.min(jnp.where(gate == best, blk, float(nblk)), axis=0, keepdims=True)
                pick = blk == first
                bias = jnp.where(pick, 0.0, bias)
                gate = jnp.where(pick, -jnp.inf, gate)
            bias_scr[hd] = jnp.where(past, bias, MASK_VALUE)

    def qk_stage(j, diagonal):
        start = pl.multiple_of(j * K_TILE, K_TILE)
        k_blk = k_ref[pl.ds(start, K_TILE), :]
        for hd in range(heads):
            s = jnp.dot(k_blk[:, key_lanes(hd)], q_t_ref[q_rows(hd), :],
                        preferred_element_type=F32)
            if diagonal:
                key_pos = lax.broadcasted_iota(jnp.int32, s.shape, 0)
                qry_pos = lax.broadcasted_iota(jnp.int32, s.shape, 1)
                s = jnp.where(key_pos <= qry_pos, s, MASK_VALUE)
            s_scr[hd] = s
            smax = jnp.max(s, axis=0, keepdims=True)
            if moba:
                bias = jnp.zeros_like(smax) if diagonal else bias_scr[hd, pl.ds(j, 1), :]
                sbias_scr[hd] = bias
                smax = smax + bias
            smax_scr[hd] = smax

    ones_rows = jnp.ones((SUM_ROWS, K_TILE), BF16)

    def softmax_stage():
        for hd in range(heads):
            m_old = m_scr[hd]
            m_new = jnp.maximum(m_old, smax_scr[hd])
            shift = m_new - sbias_scr[hd] if moba else m_new
            p_scr[hd] = jnp.exp2(s_scr[hd] - shift).astype(BF16)
            alpha_scr[hd] = jnp.exp2(m_old - m_new)
            m_scr[hd] = m_new

    def pv_stage(j):
        v_t = v_t_ref[j]
        for hd in range(heads):
            v_ext = jnp.concatenate([v_t[hd * vdim:(hd + 1) * vdim], ones_rows], axis=0)
            acc_scr[hd] = alpha_scr[hd] * acc_scr[hd] + jnp.dot(
                v_ext, p_scr[hd], preferred_element_type=F32)

    def previous_block(t):
        return jnp.where(t <= 1, i, t - 2)

    m_scr[...] = jnp.full(m_scr.shape, MASK_VALUE, F32)
    acc_scr[...] = jnp.zeros(acc_scr.shape, F32)
    alpha_scr[...] = jnp.ones(alpha_scr.shape, F32)
    p_scr[...] = jnp.zeros(p_scr.shape, BF16)
    qk_stage(i, diagonal=True)

    def body(t, carry):
        pv_stage(previous_block(t))
        softmax_stage()
        qk_stage(t, diagonal=False)
        return carry

    lax.fori_loop(0, i, body, 0)
    pv_stage(previous_block(i))
    softmax_stage()
    pv_stage(jnp.maximum(i - 1, 0))

    for hd in range(heads):
        acc = acc_scr[hd]
        o_t_ref[hd * vdim:(hd + 1) * vdim, :] = (acc[:vdim] / acc[vdim:vdim + 1]).astype(o_t_ref.dtype)


def _post_kernel(x_ref, oa_t_ref, ob_t_ref, gate_ref, w_oa_ref, w_ob_ref, w_out_ref,
                 g_mlp_ref, w_ff1_ref, w_ff2_ref, g_final_ref, out_ref, *, ff_chunk, final_norm):
    d_model = x_ref.shape[1]
    contract_rows = (((0,), (0,)), ((), ()))
    ya, yb = [], []
    for qt in range(ROW_TILE // Q_TILE):
        ya.append(lax.dot_general(oa_t_ref[qt], w_oa_ref[...], contract_rows, preferred_element_type=F32))
        yb.append(lax.dot_general(ob_t_ref[qt], w_ob_ref[...], contract_rows, preferred_element_type=F32))
    ya = jnp.concatenate(ya, axis=0)
    yb = jnp.concatenate(yb, axis=0)
    mixed = gate_ref[:, :d_model].astype(F32) * ya + gate_ref[:, d_model:].astype(F32) * yb
    x1 = x_ref[...] + jnp.dot(mixed.astype(BF16), w_out_ref[...], preferred_element_type=F32)

    h2 = _rms(x1, g_mlp_ref[...]).astype(BF16)
    mlp = jnp.zeros_like(x1)
    for c in range(w_ff1_ref.shape[1] // ff_chunk):
        u = jnp.maximum(jnp.dot(h2, w_ff1_ref[:, c * ff_chunk:(c + 1) * ff_chunk],
                                preferred_element_type=F32), 0.0)
        mlp = mlp + jnp.dot((u * u).astype(BF16), w_ff2_ref[c * ff_chunk:(c + 1) * ff_chunk, :],
                            preferred_element_type=F32)
    x2 = x1 + mlp
    out_ref[...] = _rms(x2, g_final_ref[...]) if final_norm else x2


def _resident(shape):
    return pl.BlockSpec(shape, lambda *_: (0,) * len(shape), pipeline_mode=pl.Buffered(1))


def _rope_tables(seq, dim):
    half = dim // 2
    inv_freq = np.power(ROPE_THETA, -np.arange(half, dtype=np.float64) / half)
    ang = np.arange(seq, dtype=np.float64)[:, None] * inv_freq[None, :]
    return np.cos(ang).astype(np.float32), np.sin(ang).astype(np.float32)


def _moba_group_columns():
    half = A_HEAD_DIM // 2
    order = []
    for pair in range(A_HEADS // 2):
        h0, h1 = 2 * pair * A_HEAD_DIM, (2 * pair + 1) * A_HEAD_DIM
        for start in (h0, h1, h0 + half, h1 + half):
            order.extend(range(start, start + half))
    return jnp.asarray(order, jnp.int32)


def _mla_slot_columns(nope, rope, xp=jnp):
    r = B_ROPE // 2
    split = LANES // 2 - r
    pad = xp.zeros(nope.shape[:-1] + (HEAD_SLOT - B_QK,), nope.dtype)
    return xp.concatenate([rope[..., :r], nope[..., :split], rope[..., r:], nope[..., split:], pad], axis=-1)


def _params(*semantics):
    return pltpu.CompilerParams(dimension_semantics=semantics, vmem_limit_bytes=VMEM_LIMIT_BYTES)


def _pre_call(x2, seq, g_mix, w_in, b_gate, g_cq, w_q_up, g_ckv, w_kv_up):
    rows, d_model = x2.shape
    a_width = A_HEADS * A_HEAD_DIM
    o_q, o_k, o_v = 0, a_width, 2 * a_width
    o_cq = 3 * a_width
    o_ckv = o_cq + B_Q_RANK
    o_kr = o_ckv + B_KV_RANK
    o_gate = o_kr + B_ROPE
    w_in = w_in.astype(BF16)
    cols = _moba_group_columns()
    w_qkv = jnp.concatenate([w_in[:, o_q:o_k][:, cols], w_in[:, o_k:o_v][:, cols], w_in[:, o_v:o_cq]], axis=1)
    w_cq = w_in[:, o_cq:o_ckv]
    w_ckv = w_in[:, o_ckv:o_kr]
    w_kr = _mla_slot_columns(jnp.zeros((d_model, B_NOPE), BF16), w_in[:, o_kr:o_gate])
    w_gate = w_in[:, o_gate:]
    w_q = w_q_up.astype(BF16).reshape(B_Q_RANK, B_HEADS, B_QK)
    w_qup = _mla_slot_columns(w_q[..., :B_NOPE], w_q[..., B_NOPE:]).reshape(B_Q_RANK, B_HEADS * HEAD_SLOT)
    w_kv = w_kv_up.astype(BF16).reshape(B_KV_RANK, B_HEADS, B_NOPE + B_V)
    w_kup = _mla_slot_columns(w_kv[..., :B_NOPE], jnp.zeros((B_KV_RANK, B_HEADS, B_ROPE), BF16)).reshape(
        B_KV_RANK, B_HEADS * HEAD_SLOT)
    w_vup = w_kv[:, :, B_NOPE:].reshape(B_KV_RANK, B_HEADS * B_V)

    cos_a, sin_a = _rope_tables(seq, A_HEAD_DIM)
    cos_a = np.tile(cos_a, (1, LANES // (A_HEAD_DIM // 2)))
    sin_a = np.concatenate([-sin_a, -sin_a, sin_a, sin_a], axis=1)
    cos_b, sin_b = _rope_tables(seq, B_ROPE)
    cos_b = _mla_slot_columns(np.ones((seq, B_NOPE), np.float32), np.concatenate([cos_b, cos_b], axis=1), np)
    sin_b = _mla_slot_columns(np.zeros((seq, B_NOPE), np.float32), np.concatenate([-sin_b, sin_b], axis=1), np)

    n_tiles = rows // ROW_TILE
    seq_tiles = seq // ROW_TILE
    q_per_tile = ROW_TILE // Q_TILE
    n_q = rows // Q_TILE
    row_spec = lambda w: pl.BlockSpec((ROW_TILE, w), lambda r: (r, 0))
    table_spec = pl.BlockSpec((ROW_TILE, LANES), lambda r: (r % seq_tiles, 0))
    t_spec = lambda h: pl.BlockSpec((q_per_tile, h, Q_TILE), lambda r: (r, 0, 0))
    vec = lambda a: a.reshape(1, -1).astype(F32)
    operands = [
        (x2, row_spec(d_model)), (vec(g_mix), _resident((1, d_model))),
        (cos_a, table_spec), (sin_a, table_spec), (cos_b, table_spec), (sin_b, table_spec),
        (w_qkv, _resident(w_qkv.shape)), (w_cq, _resident(w_cq.shape)), (w_ckv, _resident(w_ckv.shape)),
        (w_kr, _resident(w_kr.shape)), (w_gate, _resident(w_gate.shape)),
        (vec(b_gate), _resident((1, 2 * d_model))),
        (vec(g_cq), _resident((1, B_Q_RANK))), (w_qup, _resident(w_qup.shape)),
        (vec(g_ckv), _resident((1, B_KV_RANK))), (w_kup, _resident(w_kup.shape)),
        (w_vup, _resident(w_vup.shape)),
    ]
    out_shape = [
        jax.ShapeDtypeStruct((n_q, A_HEADS * HEAD_SLOT, Q_TILE), BF16),
        jax.ShapeDtypeStruct((rows, a_width), BF16),
        jax.ShapeDtypeStruct((n_q, a_width, Q_TILE), BF16),
        jax.ShapeDtypeStruct((n_tiles, ROW_TILE // MOBA_BLOCK, a_width), F32),
        jax.ShapeDtypeStruct((rows, 2 * d_model), BF16),
        jax.ShapeDtypeStruct((n_q, B_HEADS * HEAD_SLOT, Q_TILE), BF16),
        jax.ShapeDtypeStruct((rows, B_HEADS * HEAD_SLOT), BF16),
        jax.ShapeDtypeStruct((n_q, B_HEADS * B_V, Q_TILE), BF16),
    ]
    out_specs = [
        t_spec(A_HEADS * HEAD_SLOT), row_spec(a_width), t_spec(a_width),
        pl.BlockSpec((1, ROW_TILE // MOBA_BLOCK, a_width), lambda r: (r, 0, 0)),
        row_spec(2 * d_model), t_spec(B_HEADS * HEAD_SLOT), row_spec(B_HEADS * HEAD_SLOT),
        t_spec(B_HEADS * B_V),
    ]
    return pl.pallas_call(
        _pre_kernel,
        grid=(n_tiles,),
        in_specs=[s for _, s in operands],
        out_specs=out_specs,
        out_shape=out_shape,
        compiler_params=_params("parallel"),
        name="pre_proj",
    )(*[a for a, _ in operands])


def _attn_call(q_t, k, v_t, kmean, *, batch, seq, moba):
    nblk = seq // Q_TILE
    heads = q_t.shape[1] // HEAD_SLOT
    groups = heads // HEADS_PER_STEP
    vdim = v_t.shape[2] // heads
    key_lanes = k.shape[2] // groups
    kv_mode = None if moba else pl.Buffered(1)
    in_specs = [
        pl.BlockSpec((None, HEADS_PER_STEP * HEAD_SLOT, Q_TILE), lambda b, g, i: (b * nblk + i, g, 0)),
        pl.BlockSpec((None, seq, key_lanes), lambda b, g, i: (b, 0, g), pipeline_mode=kv_mode),
        pl.BlockSpec((None, nblk, HEADS_PER_STEP * vdim, K_TILE), lambda b, g, i: (b, 0, g, 0),
                     pipeline_mode=kv_mode),
    ]
    operands = [q_t, k, v_t]
    stat = pltpu.VMEM((HEADS_PER_STEP, 1, Q_TILE), F32)
    scratch = [
        pltpu.VMEM((HEADS_PER_STEP, K_TILE, Q_TILE), F32),
        stat,
        pltpu.VMEM((HEADS_PER_STEP, K_TILE, Q_TILE), BF16),
        stat,
        stat,
        pltpu.VMEM((HEADS_PER_STEP, vdim + SUM_ROWS, Q_TILE), F32),
    ]
    if moba:
        in_specs.append(pl.BlockSpec((None, nblk, key_lanes), lambda b, g, i: (b, 0, g)))
        operands.append(kmean)
        scratch.append(pltpu.VMEM((HEADS_PER_STEP, nblk, Q_TILE), F32))
        scratch.append(stat)
    return pl.pallas_call(
        functools.partial(_attn_kernel, moba=moba),
        grid=(batch, groups, nblk),
        in_specs=in_specs,
        out_specs=pl.BlockSpec((None, HEADS_PER_STEP * vdim, Q_TILE), lambda b, g, i: (b * nblk + i, g, 0)),
        out_shape=jax.ShapeDtypeStruct((batch * nblk, heads * vdim, Q_TILE), BF16),
        scratch_shapes=scratch,
        compiler_params=_params("parallel", "parallel", "arbitrary"),
        name="moba_attn" if moba else "mla_attn",
    )(*operands)


def _post_call(x2, oa_t, ob_t, gates, w_o_a, w_o_b, w_out, g_mlp, w_ff1, w_ff2, g_final, final_norm):
    rows, d_model = x2.shape
    q_per_tile = ROW_TILE // Q_TILE
    row_spec = lambda w: pl.BlockSpec((ROW_TILE, w), lambda r: (r, 0))
    t_spec = lambda a: pl.BlockSpec((q_per_tile,) + a.shape[1:], lambda r: (r, 0, 0))
    vec = lambda a: a.reshape(1, -1).astype(F32)
    weights = [w.astype(BF16) for w in (w_o_a, w_o_b, w_out)]
    operands = [(x2, row_spec(d_model)), (oa_t, t_spec(oa_t)), (ob_t, t_spec(ob_t)),
                (gates, row_spec(2 * d_model))]
    operands += [(w, _resident(w.shape)) for w in weights]
    operands += [(vec(g_mlp), _resident((1, d_model))),
                 (w_ff1.astype(BF16), _resident(w_ff1.shape)), (w_ff2.astype(BF16), _resident(w_ff2.shape)),
                 (vec(g_final), _resident((1, d_model)))]
    return pl.pallas_call(
        functools.partial(_post_kernel, ff_chunk=1024, final_norm=final_norm),
        grid=(rows // ROW_TILE,),
        in_specs=[s for _, s in operands],
        out_specs=row_spec(d_model),
        out_shape=jax.ShapeDtypeStruct((rows, d_model), F32),
        compiler_params=_params("parallel"),
        name="post_mlp",
    )(*[a for a, _ in operands])


def kernel(x, g_mix, w_in, b_gate, g_cq, w_q_up, g_ckv, w_kv_up, w_o_a, w_o_b, w_out, g_mlp, w_ff1,
           w_ff2, g_final):
    batch, seq, d_model = x.shape
    depth = w_in.shape[0]
    assert seq % ROW_TILE == 0 and ROW_TILE % MOBA_BLOCK == 0
    assert A_HEADS % HEADS_PER_STEP == 0 and B_HEADS % HEADS_PER_STEP == 0 and HEADS_PER_STEP % 2 == 0
    nblk = seq // Q_TILE
    x2 = x.reshape(batch * seq, d_model)
    for l in range(depth):
        qa_t, ka, va_t, kmean, gates, qb_t, kb, vb_t = _pre_call(
            x2, seq, g_mix[l], w_in[l], b_gate[l], g_cq[l], w_q_up[l], g_ckv[l], w_kv_up[l])
        oa_t = _attn_call(qa_t, ka.reshape(batch, seq, -1), va_t.reshape(batch, nblk, -1, K_TILE),
                          kmean.reshape(batch, nblk, -1), batch=batch, seq=seq, moba=True)
        ob_t = _attn_call(qb_t, kb.reshape(batch, seq, -1), vb_t.reshape(batch, nblk, -1, K_TILE),
                          None, batch=batch, seq=seq, moba=False)
        x2 = _post_call(x2, oa_t, ob_t, gates, w_o_a[l], w_o_b[l], w_out[l], g_mlp[l], w_ff1[l],
                        w_ff2[l], g_final, final_norm=(l == depth - 1))
    return x2.reshape(batch, seq, d_model)
```

```python
import functools

import jax
import jax.numpy as jnp
import numpy as np
from jax import lax
from jax.experimental import pallas as pl
from jax.experimental.pallas import tpu as pltpu

EPS = 1e-6
ROPE_THETA = 10000.0

A_HEADS = 8
A_HEAD_DIM = 64
MOBA_BLOCK = 256
MOBA_TOPK = 3

B_HEADS = 8
B_NOPE = 64
B_ROPE = 32
B_QK = B_NOPE + B_ROPE
B_V = 64
B_Q_RANK = 384
B_KV_RANK = 256

LANES = 128
HEAD_SLOT = 128
ROW_TILE = 512
Q_TILE = MOBA_BLOCK
K_TILE = MOBA_BLOCK
HEADS_PER_STEP = 8
MASK_VALUE = -1e30
LOG2_E = 1.4426950408889634
VMEM_LIMIT_BYTES = 56 * 1024 * 1024

F32 = jnp.float32
BF16 = jnp.bfloat16


def _rms(x, g):
    return x * lax.rsqrt(jnp.mean(x * x, axis=-1, keepdims=True) + EPS) * g


def _rope_lanes(x, cos, sin_signed):
    groups = []
    for g in range(x.shape[1] // LANES):
        xg = x[:, g * LANES:(g + 1) * LANES]
        groups.append(xg * cos + pltpu.roll(xg, LANES // 2, axis=1) * sin_signed)
    return groups


def _store_transposed(dst_ref, row0, t):
    rows = t.shape[0]
    for qt in range(ROW_TILE // Q_TILE):
        dst_ref[qt, row0:row0 + rows, :] = t[:, qt * Q_TILE:(qt + 1) * Q_TILE].astype(dst_ref.dtype)


def _pre_kernel(x_ref, gmix_ref, cos_a_ref, sin_a_ref, cos_b_ref, sin_b_ref,
                w_qkv_ref, w_cq_ref, w_ckv_ref, w_kr_ref, w_gate_ref, b_gate_ref,
                g_cq_ref, w_qup_ref, g_ckv_ref, w_kup_ref, w_vup_ref,
                qa_t_ref, ka_ref, va_t_ref, kmean_ref, gate_ref, qb_t_ref, kb_ref, vb_t_ref):
    a_width = A_HEADS * A_HEAD_DIM
    h = _rms(x_ref[...], gmix_ref[...]).astype(BF16)

    qkv = jnp.dot(h, w_qkv_ref[...], preferred_element_type=F32)
    cos_a, sin_a = cos_a_ref[...], sin_a_ref[...]
    q_groups = _rope_lanes(qkv[:, :a_width], cos_a, sin_a)
    k_groups = _rope_lanes(qkv[:, a_width:2 * a_width], cos_a, sin_a)
    first_head = (lax.broadcasted_iota(jnp.int32, (LANES, ROW_TILE), 0) % A_HEAD_DIM) < A_HEAD_DIM // 2
    for g, qg in enumerate(q_groups):
        t = (qg * (A_HEAD_DIM ** -0.5 * LOG2_E)).T
        _store_transposed(qa_t_ref, (2 * g) * HEAD_SLOT, jnp.where(first_head, t, 0.0))
        _store_transposed(qa_t_ref, (2 * g + 1) * HEAD_SLOT, jnp.where(first_head, 0.0, t))
    for g, kg in enumerate(k_groups):
        ka_ref[:, g * LANES:(g + 1) * LANES] = kg.astype(BF16)
        for blk in range(ROW_TILE // MOBA_BLOCK):
            kmean_ref[0, blk:blk + 1, g * LANES:(g + 1) * LANES] = jnp.mean(
                kg[blk * MOBA_BLOCK:(blk + 1) * MOBA_BLOCK], axis=0, keepdims=True)
    for g in range(a_width // LANES):
        vg = qkv[:, 2 * a_width + g * LANES:2 * a_width + (g + 1) * LANES]
        _store_transposed(va_t_ref, g * LANES, vg.T)

    gate = jnp.dot(h, w_gate_ref[...], preferred_element_type=F32) + b_gate_ref[...]
    gate_ref[...] = jax.nn.sigmoid(gate).astype(BF16)

    cos_b, sin_b = cos_b_ref[...], sin_b_ref[...]
    cq = jnp.dot(h, w_cq_ref[...], preferred_element_type=F32)
    cqn = _rms(cq, g_cq_ref[...]).astype(BF16)
    qb = jnp.dot(cqn, w_qup_ref[...], preferred_element_type=F32)
    for hd, qg in enumerate(_rope_lanes(qb, cos_b, sin_b)):
        _store_transposed(qb_t_ref, hd * HEAD_SLOT, (qg * (B_QK ** -0.5 * LOG2_E)).T)
    ckv = jnp.dot(h, w_ckv_ref[...], preferred_element_type=F32)
    ckvn = _rms(ckv, g_ckv_ref[...]).astype(BF16)
    k_nope = jnp.dot(ckvn, w_kup_ref[...], preferred_element_type=F32)
    kr = jnp.dot(h, w_kr_ref[...], preferred_element_type=F32)
    kr = _rope_lanes(kr, cos_b, sin_b)[0]
    for hd in range(B_HEADS):
        kb_ref[:, hd * HEAD_SLOT:(hd + 1) * HEAD_SLOT] = (
            k_nope[:, hd * HEAD_SLOT:(hd + 1) * HEAD_SLOT] + kr).astype(BF16)
    vb = jnp.dot(ckvn, w_vup_ref[...], preferred_element_type=F32)
    for g in range(B_HEADS * B_V // LANES):
        _store_transposed(vb_t_ref, g * LANES, vb[:, g * LANES:(g + 1) * LANES].T)


def _attn_kernel(*refs, moba):
    if moba:
        q_t_ref, k_ref, v_t_ref, kmean_ref, o_t_ref = refs[:5]
        s_scr, smax_scr, p_scr, alpha_scr, m_scr, l_scr, acc_scr, bias_scr, sbias_scr = refs[5:]
    else:
        q_t_ref, k_ref, v_t_ref, o_t_ref = refs[:4]
        s_scr, smax_scr, p_scr, alpha_scr, m_scr, l_scr, acc_scr = refs[4:]
    i = pl.program_id(2)
    heads = HEADS_PER_STEP
    vdim = v_t_ref.shape[1] // heads

    def key_lanes(hd):
        g = hd // 2 if moba else hd
        return slice(g * LANES, (g + 1) * LANES)

    def q_rows(hd):
        return slice(hd * HEAD_SLOT, (hd + 1) * HEAD_SLOT)

    if moba:
        nblk = kmean_ref.shape[0]
        blk = lax.broadcasted_iota(jnp.int32, (nblk, Q_TILE), 0).astype(F32)
        past = blk < i.astype(F32)
        for hd in range(heads):
            gate = jnp.dot(kmean_ref[:, key_lanes(hd)].astype(BF16), q_t_ref[q_rows(hd), :],
                           preferred_element_type=F32)
            gate = jnp.where(past, gate, -jnp.inf)
            bias = jnp.full(gate.shape, MASK_VALUE, F32)
            for _ in range(MOBA_TOPK):
                best = jnp.max(gate, axis=0, keepdims=True)
                first = jnp.min(jnp.where(gate == best, blk, float(nblk)), axis=0, keepdims=True)
                pick = blk == first
                bias = jnp.where(pick, 0.0, bias)
                gate = jnp.where(pick, -jnp.inf, gate)
            bias_scr[hd] = jnp.where(past, bias, MASK_VALUE)

    def qk_stage(j, diagonal):
        start = pl.multiple_of(j * K_TILE, K_TILE)
        k_blk = k_ref[pl.ds(start, K_TILE), :]
        for hd in range(heads):
            s = jnp.dot(k_blk[:, key_lanes(hd)], q_t_ref[q_rows(hd), :],
                        preferred_element_type=F32)
            if diagonal:
                key_pos = lax.broadcasted_iota(jnp.int32, s.shape, 0)
                qry_pos = lax.broadcasted_iota(jnp.int32, s.shape, 1)
                s = jnp.where(key_pos <= qry_pos, s, MASK_VALUE)
            s_scr[hd] = s
            smax = jnp.max(s, axis=0, keepdims=True)
            if moba:
                bias = jnp.zeros_like(smax) if diagonal else bias_scr[hd, pl.ds(j, 1), :]
                sbias_scr[hd] = bias
                smax = smax + bias
            smax_scr[hd] = smax

    def softmax_stage():
        for hd in range(heads):
            m_old = m_scr[hd]
            m_new = jnp.maximum(m_old, smax_scr[hd])
            shift = m_new - sbias_scr[hd] if moba else m_new
            alpha = jnp.exp2(m_old - m_new)
            p = jnp.exp2(s_scr[hd] - shift)
            l_scr[hd] = alpha * l_scr[hd] + jnp.sum(p, axis=0, keepdims=True)
            p_scr[hd] = p.astype(BF16)
            alpha_scr[hd] = alpha
            m_scr[hd] = m_new

    def pv_stage(j):
        v_t = v_t_ref[j]
        for hd in range(heads):
            acc_scr[hd] = alpha_scr[hd] * acc_scr[hd] + jnp.dot(
                v_t[hd * vdim:(hd + 1) * vdim], p_scr[hd], preferred_element_type=F32)

    def previous_block(t):
        return jnp.where(t <= 1, i, t - 2)

    m_scr[...] = jnp.full(m_scr.shape, MASK_VALUE, F32)
    l_scr[...] = jnp.zeros(l_scr.shape, F32)
    acc_scr[...] = jnp.zeros(acc_scr.shape, F32)
    alpha_scr[...] = jnp.ones(alpha_scr.shape, F32)
    p_scr[...] = jnp.zeros(p_scr.shape, BF16)
    qk_stage(i, diagonal=True)

    def body(t, carry):
        pv_stage(previous_block(t))
        softmax_stage()
        qk_stage(t, diagonal=False)
        return carry

    lax.fori_loop(0, i, body, 0)
    pv_stage(previous_block(i))
    softmax_stage()
    pv_stage(jnp.maximum(i - 1, 0))

    for hd in range(heads):
        o_t_ref[hd * vdim:(hd + 1) * vdim, :] = (acc_scr[hd] / l_scr[hd]).astype(o_t_ref.dtype)


def _post_kernel(x_ref, oa_t_ref, ob_t_ref, gate_ref, w_oa_ref, w_ob_ref, w_out_ref,
                 g_mlp_ref, w_ff1_ref, w_ff2_ref, g_final_ref, out_ref, *, ff_chunk, final_norm):
    d_model = x_ref.shape[1]
    contract_rows = (((0,), (0,)), ((), ()))
    ya, yb = [], []
    for qt in range(ROW_TILE // Q_TILE):
        ya.append(lax.dot_general(oa_t_ref[qt], w_oa_ref[...], contract_rows, preferred_element_type=F32))
        yb.append(lax.dot_general(ob_t_ref[qt], w_ob_ref[...], contract_rows, preferred_element_type=F32))
    ya = jnp.concatenate(ya, axis=0)
    yb = jnp.concatenate(yb, axis=0)
    mixed = gate_ref[:, :d_model].astype(F32) * ya + gate_ref[:, d_model:].astype(F32) * yb
    x1 = x_ref[...] + jnp.dot(mixed.astype(BF16), w_out_ref[...], preferred_element_type=F32)

    h2 = _rms(x1, g_mlp_ref[...]).astype(BF16)
    mlp = jnp.zeros_like(x1)
    for c in range(w_ff1_ref.shape[1] // ff_chunk):
        u = jnp.maximum(jnp.dot(h2, w_ff1_ref[:, c * ff_chunk:(c + 1) * ff_chunk],
                                preferred_element_type=F32), 0.0)
        mlp = mlp + jnp.dot((u * u).astype(BF16), w_ff2_ref[c * ff_chunk:(c + 1) * ff_chunk, :],
                            preferred_element_type=F32)
    x2 = x1 + mlp
    out_ref[...] = _rms(x2, g_final_ref[...]) if final_norm else x2


def _resident(shape):
    return pl.BlockSpec(shape, lambda *_: (0,) * len(shape), pipeline_mode=pl.Buffered(1))


def _rope_tables(seq, dim):
    half = dim // 2
    inv_freq = np.power(ROPE_THETA, -np.arange(half, dtype=np.float64) / half)
    ang = np.arange(seq, dtype=np.float64)[:, None] * inv_freq[None, :]
    return np.cos(ang).astype(np.float32), np.sin(ang).astype(np.float32)


def _moba_group_columns():
    half = A_HEAD_DIM // 2
    order = []
    for pair in range(A_HEADS // 2):
        h0, h1 = 2 * pair * A_HEAD_DIM, (2 * pair + 1) * A_HEAD_DIM
        for start in (h0, h1, h0 + half, h1 + half):
            order.extend(range(start, start + half))
    return jnp.asarray(order, jnp.int32)


def _mla_slot_columns(nope, rope, xp=jnp):
    r = B_ROPE // 2
    split = LANES // 2 - r
    pad = xp.zeros(nope.shape[:-1] + (HEAD_SLOT - B_QK,), nope.dtype)
    return xp.concatenate([rope[..., :r], nope[..., :split], rope[..., r:], nope[..., split:], pad], axis=-1)


def _params(*semantics):
    return pltpu.CompilerParams(dimension_semantics=semantics, vmem_limit_bytes=VMEM_LIMIT_BYTES)


def _pre_call(x2, seq, g_mix, w_in, b_gate, g_cq, w_q_up, g_ckv, w_kv_up):
    rows, d_model = x2.shape
    a_width = A_HEADS * A_HEAD_DIM
    o_q, o_k, o_v = 0, a_width, 2 * a_width
    o_cq = 3 * a_width
    o_ckv = o_cq + B_Q_RANK
    o_kr = o_ckv + B_KV_RANK
    o_gate = o_kr + B_ROPE
    w_in = w_in.astype(BF16)
    cols = _moba_group_columns()
    w_qkv = jnp.concatenate([w_in[:, o_q:o_k][:, cols], w_in[:, o_k:o_v][:, cols], w_in[:, o_v:o_cq]], axis=1)
    w_cq = w_in[:, o_cq:o_ckv]
    w_ckv = w_in[:, o_ckv:o_kr]
    w_kr = _mla_slot_columns(jnp.zeros((d_model, B_NOPE), BF16), w_in[:, o_kr:o_gate])
    w_gate = w_in[:, o_gate:]
    w_q = w_q_up.astype(BF16).reshape(B_Q_RANK, B_HEADS, B_QK)
    w_qup = _mla_slot_columns(w_q[..., :B_NOPE], w_q[..., B_NOPE:]).reshape(B_Q_RANK, B_HEADS * HEAD_SLOT)
    w_kv = w_kv_up.astype(BF16).reshape(B_KV_RANK, B_HEADS, B_NOPE + B_V)
    w_kup = _mla_slot_columns(w_kv[..., :B_NOPE], jnp.zeros((B_KV_RANK, B_HEADS, B_ROPE), BF16)).reshape(
        B_KV_RANK, B_HEADS * HEAD_SLOT)
    w_vup = w_kv[:, :, B_NOPE:].reshape(B_KV_RANK, B_HEADS * B_V)

    cos_a, sin_a = _rope_tables(seq, A_HEAD_DIM)
    cos_a = np.tile(cos_a, (1, LANES // (A_HEAD_DIM // 2)))
    sin_a = np.concatenate([-sin_a, -sin_a, sin_a, sin_a], axis=1)
    cos_b, sin_b = _rope_tables(seq, B_ROPE)
    cos_b = _mla_slot_columns(np.ones((seq, B_NOPE), np.float32), np.concatenate([cos_b, cos_b], axis=1), np)
    sin_b = _mla_slot_columns(np.zeros((seq, B_NOPE), np.float32), np.concatenate([-sin_b, sin_b], axis=1), np)

    n_tiles = rows // ROW_TILE
    seq_tiles = seq // ROW_TILE
    q_per_tile = ROW_TILE // Q_TILE
    n_q = rows // Q_TILE
    row_spec = lambda w: pl.BlockSpec((ROW_TILE, w), lambda r: (r, 0))
    table_spec = pl.BlockSpec((ROW_TILE, LANES), lambda r: (r % seq_tiles, 0))
    t_spec = lambda h: pl.BlockSpec((q_per_tile, h, Q_TILE), lambda r: (r, 0, 0))
    vec = lambda a: a.reshape(1, -1).astype(F32)
    operands = [
        (x2, row_spec(d_model)), (vec(g_mix), _resident((1, d_model))),
        (cos_a, table_spec), (sin_a, table_spec), (cos_b, table_spec), (sin_b, table_spec),
        (w_qkv, _resident(w_qkv.shape)), (w_cq, _resident(w_cq.shape)), (w_ckv, _resident(w_ckv.shape)),
        (w_kr, _resident(w_kr.shape)), (w_gate, _resident(w_gate.shape)),
        (vec(b_gate), _resident((1, 2 * d_model))),
        (vec(g_cq), _resident((1, B_Q_RANK))), (w_qup, _resident(w_qup.shape)),
        (vec(g_ckv), _resident((1, B_KV_RANK))), (w_kup, _resident(w_kup.shape)),
        (w_vup, _resident(w_vup.shape)),
    ]
    out_shape = [
        jax.ShapeDtypeStruct((n_q, A_HEADS * HEAD_SLOT, Q_TILE), BF16),
        jax.ShapeDtypeStruct((rows, a_width), BF16),
        jax.ShapeDtypeStruct((n_q, a_width, Q_TILE), BF16),
        jax.ShapeDtypeStruct((n_tiles, ROW_TILE // MOBA_BLOCK, a_width), F32),
        jax.ShapeDtypeStruct((rows, 2 * d_model), BF16),
        jax.ShapeDtypeStruct((n_q, B_HEADS * HEAD_SLOT, Q_TILE), BF16),
        jax.ShapeDtypeStruct((rows, B_HEADS * HEAD_SLOT), BF16),
        jax.ShapeDtypeStruct((n_q, B_HEADS * B_V, Q_TILE), BF16),
    ]
    out_specs = [
        t_spec(A_HEADS * HEAD_SLOT), row_spec(a_width), t_spec(a_width),
        pl.BlockSpec((1, ROW_TILE // MOBA_BLOCK, a_width), lambda r: (r, 0, 0)),
        row_spec(2 * d_model), t_spec(B_HEADS * HEAD_SLOT), row_spec(B_HEADS * HEAD_SLOT),
        t_spec(B_HEADS * B_V),
    ]
    return pl.pallas_call(
        _pre_kernel,
        grid=(n_tiles,),
        in_specs=[s for _, s in operands],
        out_specs=out_specs,
        out_shape=out_shape,
        compiler_params=_params("parallel"),
        name="pre_proj",
    )(*[a for a, _ in operands])


def _attn_call(q_t, k, v_t, kmean, *, batch, seq, moba):
    nblk = seq // Q_TILE
    heads = q_t.shape[1] // HEAD_SLOT
    groups = heads // HEADS_PER_STEP
    vdim = v_t.shape[2] // heads
    key_lanes = k.shape[2] // groups
    kv_mode = None if moba else pl.Buffered(1)
    in_specs = [
        pl.BlockSpec((None, HEADS_PER_STEP * HEAD_SLOT, Q_TILE), lambda b, g, i: (b * nblk + i, g, 0)),
        pl.BlockSpec((None, seq, key_lanes), lambda b, g, i: (b, 0, g), pipeline_mode=kv_mode),
        pl.BlockSpec((None, nblk, HEADS_PER_STEP * vdim, K_TILE), lambda b, g, i: (b, 0, g, 0),
                     pipeline_mode=kv_mode),
    ]
    operands = [q_t, k, v_t]
    stat = pltpu.VMEM((HEADS_PER_STEP, 1, Q_TILE), F32)
    scratch = [
        pltpu.VMEM((HEADS_PER_STEP, K_TILE, Q_TILE), F32),
        stat,
        pltpu.VMEM((HEADS_PER_STEP, K_TILE, Q_TILE), BF16),
        stat,
        stat,
        stat,
        pltpu.VMEM((HEADS_PER_STEP, vdim, Q_TILE), F32),
    ]
    if moba:
        in_specs.append(pl.BlockSpec((None, nblk, key_lanes), lambda b, g, i: (b, 0, g)))
        operands.append(kmean)
        scratch.append(pltpu.VMEM((HEADS_PER_STEP, nblk, Q_TILE), F32))
        scratch.append(stat)
    return pl.pallas_call(
        functools.partial(_attn_kernel, moba=moba),
        grid=(batch, groups, nblk),
        in_specs=in_specs,
        out_specs=pl.BlockSpec((None, HEADS_PER_STEP * vdim, Q_TILE), lambda b, g, i: (b * nblk + i, g, 0)),
        out_shape=jax.ShapeDtypeStruct((batch * nblk, heads * vdim, Q_TILE), BF16),
        scratch_shapes=scratch,
        compiler_params=_params("parallel", "parallel", "arbitrary"),
        name="moba_attn" if moba else "mla_attn",
    )(*operands)


def _post_call(x2, oa_t, ob_t, gates, w_o_a, w_o_b, w_out, g_mlp, w_ff1, w_ff2, g_final, final_norm):
    rows, d_model = x2.shape
    q_per_tile = ROW_TILE // Q_TILE
    row_spec = lambda w: pl.BlockSpec((ROW_TILE, w), lambda r: (r, 0))
    t_spec = lambda a: pl.BlockSpec((q_per_tile,) + a.shape[1:], lambda r: (r, 0, 0))
    vec = lambda a: a.reshape(1, -1).astype(F32)
    weights = [w.astype(BF16) for w in (w_o_a, w_o_b, w_out)]
    operands = [(x2, row_spec(d_model)), (oa_t, t_spec(oa_t)), (ob_t, t_spec(ob_t)),
                (gates, row_spec(2 * d_model))]
    operands += [(w, _resident(w.shape)) for w in weights]
    operands += [(vec(g_mlp), _resident((1, d_model))),
                 (w_ff1.astype(BF16), _resident(w_ff1.shape)), (w_ff2.astype(BF16), _resident(w_ff2.shape)),
                 (vec(g_final), _resident((1, d_model)))]
    return pl.pallas_call(
        functools.partial(_post_kernel, ff_chunk=1024, final_norm=final_norm),
        grid=(rows // ROW_TILE,),
        in_specs=[s for _, s in operands],
        out_specs=row_spec(d_model),
        out_shape=jax.ShapeDtypeStruct((rows, d_model), F32),
        compiler_params=_params("parallel"),
        name="post_mlp",
    )(*[a for a, _ in operands])


def kernel(x, g_mix, w_in, b_gate, g_cq, w_q_up, g_ckv, w_kv_up, w_o_a, w_o_b, w_out, g_mlp, w_ff1,
           w_ff2, g_final):
    batch, seq, d_model = x.shape
    depth = w_in.shape[0]
    assert seq % ROW_TILE == 0 and ROW_TILE % MOBA_BLOCK == 0
    assert A_HEADS % HEADS_PER_STEP == 0 and B_HEADS % HEADS_PER_STEP == 0 and HEADS_PER_STEP % 2 == 0
    nblk = seq // Q_TILE
    x2 = x.reshape(batch * seq, d_model)
    for l in range(depth):
        qa_t, ka, va_t, kmean, gates, qb_t, kb, vb_t = _pre_call(
            x2, seq, g_mix[l], w_in[l], b_gate[l], g_cq[l], w_q_up[l], g_ckv[l], w_kv_up[l])
        oa_t = _attn_call(qa_t, ka.reshape(batch, seq, -1), va_t.reshape(batch, nblk, -1, K_TILE),
                          kmean.reshape(batch, nblk, -1), batch=batch, seq=seq, moba=True)
        ob_t = _attn_call(qb_t, kb.reshape(batch, seq, -1), vb_t.reshape(batch, nblk, -1, K_TILE),
                          None, batch=batch, seq=seq, moba=False)
        x2 = _post_call(x2, oa_t, ob_t, gates, w_o_a[l], w_o_b[l], w_out[l], g_mlp[l], w_ff1[l],
                        w_ff2[l], g_final, final_norm=(l == depth - 1))
    return x2.reshape(batch, seq, d_model)
```

```python
import functools

import jax
import jax.numpy as jnp
import numpy as np
from jax import lax
from jax.experimental import pallas as pl
from jax.experimental.pallas import tpu as pltpu

EPS = 1e-6
ROPE_THETA = 10000.0

A_HEADS = 8
A_HEAD_DIM = 64
MOBA_BLOCK = 256
MOBA_TOPK = 3

B_HEADS = 8
B_NOPE = 64
B_ROPE = 32
B_QK = B_NOPE + B_ROPE
B_V = 64
B_Q_RANK = 384
B_KV_RANK = 256

LANES = 128
HEAD_SLOT = 128
ROW_TILE = 512
K_TILE = MOBA_BLOCK
Q_TILE = 2 * MOBA_BLOCK
HEADS_PER_STEP = 8
MASK_VALUE = -1e30
LOG2_E = 1.4426950408889634
SUM_ROWS = 16
VMEM_LIMIT_BYTES = 56 * 1024 * 1024

F32 = jnp.float32
BF16 = jnp.bfloat16


def _rms(x, g):
    return x * lax.rsqrt(jnp.mean(x * x, axis=-1, keepdims=True) + EPS) * g


def _rope_lanes(x, cos, sin_signed):
    groups = []
    for g in range(x.shape[1] // LANES):
        xg = x[:, g * LANES:(g + 1) * LANES]
        groups.append(xg * cos + pltpu.roll(xg, LANES // 2, axis=1) * sin_signed)
    return groups


def _store_transposed(dst_ref, row0, t):
    rows, width = t.shape[0], dst_ref.shape[2]
    for tile in range(ROW_TILE // width):
        dst_ref[tile, row0:row0 + rows, :] = t[:, tile * width:(tile + 1) * width].astype(dst_ref.dtype)


def _pre_kernel(x_ref, gmix_ref, cos_a_ref, sin_a_ref, cos_b_ref, sin_b_ref,
                w_qkv_ref, w_cq_ref, w_ckv_ref, w_kr_ref, w_gate_ref, b_gate_ref,
                g_cq_ref, w_qup_ref, g_ckv_ref, w_kup_ref, w_vup_ref,
                qa_t_ref, ka_ref, va_t_ref, kmean_ref, gate_ref, qb_t_ref, kb_ref, vb_t_ref):
    a_width = A_HEADS * A_HEAD_DIM
    h = _rms(x_ref[...], gmix_ref[...]).astype(BF16)

    qkv = jnp.dot(h, w_qkv_ref[...], preferred_element_type=F32)
    cos_a, sin_a = cos_a_ref[...], sin_a_ref[...]
    q_groups = _rope_lanes(qkv[:, :a_width], cos_a, sin_a)
    k_groups = _rope_lanes(qkv[:, a_width:2 * a_width], cos_a, sin_a)
    first_head = (lax.broadcasted_iota(jnp.int32, (LANES, ROW_TILE), 0) % A_HEAD_DIM) < A_HEAD_DIM // 2
    for g, qg in enumerate(q_groups):
        t = (qg * (A_HEAD_DIM ** -0.5 * LOG2_E)).T
        _store_transposed(qa_t_ref, (2 * g) * HEAD_SLOT, jnp.where(first_head, t, 0.0))
        _store_transposed(qa_t_ref, (2 * g + 1) * HEAD_SLOT, jnp.where(first_head, 0.0, t))
    for g, kg in enumerate(k_groups):
        ka_ref[:, g * LANES:(g + 1) * LANES] = kg.astype(BF16)
        for blk in range(ROW_TILE // MOBA_BLOCK):
            kmean_ref[0, blk:blk + 1, g * LANES:(g + 1) * LANES] = jnp.mean(
                kg[blk * MOBA_BLOCK:(blk + 1) * MOBA_BLOCK], axis=0, keepdims=True)
    for g in range(a_width // LANES):
        vg = qkv[:, 2 * a_width + g * LANES:2 * a_width + (g + 1) * LANES]
        _store_transposed(va_t_ref, g * LANES, vg.T)

    gate = jnp.dot(h, w_gate_ref[...], preferred_element_type=F32) + b_gate_ref[...]
    gate_ref[...] = jax.nn.sigmoid(gate).astype(BF16)

    cos_b, sin_b = cos_b_ref[...], sin_b_ref[...]
    cq = jnp.dot(h, w_cq_ref[...], preferred_element_type=F32)
    cqn = _rms(cq, g_cq_ref[...]).astype(BF16)
    qb = jnp.dot(cqn, w_qup_ref[...], preferred_element_type=F32)
    for hd, qg in enumerate(_rope_lanes(qb, cos_b, sin_b)):
        _store_transposed(qb_t_ref, hd * HEAD_SLOT, (qg * (B_QK ** -0.5 * LOG2_E)).T)
    ckv = jnp.dot(h, w_ckv_ref[...], preferred_element_type=F32)
    ckvn = _rms(ckv, g_ckv_ref[...]).astype(BF16)
    k_nope = jnp.dot(ckvn, w_kup_ref[...], preferred_element_type=F32)
    kr = jnp.dot(h, w_kr_ref[...], preferred_element_type=F32)
    kr = _rope_lanes(kr, cos_b, sin_b)[0]
    for hd in range(B_HEADS):
        kb_ref[:, hd * HEAD_SLOT:(hd + 1) * HEAD_SLOT] = (
            k_nope[:, hd * HEAD_SLOT:(hd + 1) * HEAD_SLOT] + kr).astype(BF16)
    vb = jnp.dot(ckvn, w_vup_ref[...], preferred_element_type=F32)
    for g in range(B_HEADS * B_V // LANES):
        _store_transposed(vb_t_ref, g * LANES, vb[:, g * LANES:(g + 1) * LANES].T)


def _attn_kernel(*refs, moba):
    if moba:
        q_t_ref, k_ref, v_t_ref, kmean_ref, o_t_ref = refs[:5]
        s_scr, smax_scr, p_scr, alpha_scr, m_scr, acc_scr, bias_scr, sbias_scr = refs[5:]
    else:
        q_t_ref, k_ref, v_t_ref, o_t_ref = refs[:4]
        s_scr, smax_scr, p_scr, alpha_scr, m_scr, acc_scr = refs[4:]
    tile = pl.program_id(2)
    own_lo = 2 * tile
    heads = HEADS_PER_STEP
    vdim = v_t_ref.shape[1] // heads
    second_half = lax.broadcasted_iota(jnp.int32, (1, Q_TILE), 1) >= MOBA_BLOCK

    def key_lanes(hd):
        g = hd // 2 if moba else hd
        return slice(g * LANES, (g + 1) * LANES)

    def q_rows(hd):
        return slice(hd * HEAD_SLOT, (hd + 1) * HEAD_SLOT)

    if moba:
        nblk = kmean_ref.shape[0]
        blk = lax.broadcasted_iota(jnp.int32, (nblk, Q_TILE), 0).astype(F32)
        own = own_lo.astype(F32) + jnp.where(second_half, 1.0, 0.0)
        past = blk < own
        for hd in range(heads):
            gate = jnp.dot(kmean_ref[:, key_lanes(hd)].astype(BF16), q_t_ref[q_rows(hd), :],
                           preferred_element_type=F32)
            gate = jnp.where(past, gate, -jnp.inf)
            bias = jnp.full(gate.shape, MASK_VALUE, F32)
            for _ in range(MOBA_TOPK):
                best = jnp.max(gate, axis=0, keepdims=True)
                first = jnp.min(jnp.where(gate == best, blk, float(nblk)), axis=0, keepdims=True)
                pick = blk == first
                bias = jnp.where(pick, 0.0, bias)
                gate = jnp.where(pick, -jnp.inf, gate)
            bias_scr[hd] = jnp.where(past, bias, MASK_VALUE)

    def qk_stage(j, own_offset=None):
        start = pl.multiple_of(j * K_TILE, K_TILE)
        k_blk = k_ref[pl.ds(start, K_TILE), :]
        for hd in range(heads):
            s = jnp.dot(k_blk[:, key_lanes(hd)], q_t_ref[q_rows(hd), :],
                        preferred_element_type=F32)
            if own_offset is not None:
                key_pos = lax.broadcasted_iota(jnp.int32, s.shape, 0) + own_offset
                qry_pos = lax.broadcasted_iota(jnp.int32, s.shape, 1)
                s = jnp.where(key_pos <= qry_pos, s, MASK_VALUE)
            s_scr[hd] = s
            smax = jnp.max(s, axis=0, keepdims=True)
            if moba:
                if own_offset is None:
                    bias = bias_scr[hd, pl.ds(j, 1), :]
                elif own_offset == 0:
                    bias = jnp.where(second_half, bias_scr[hd, pl.ds(j, 1), :], 0.0)
                else:
                    bias = jnp.zeros_like(smax)
                sbias_scr[hd] = bias
                smax = smax + bias
            smax_scr[hd] = smax

    ones_rows = jnp.ones((SUM_ROWS, K_TILE), BF16)

    def softmax_stage():
        for hd in range(heads):
            m_old = m_scr[hd]
            m_new = jnp.maximum(m_old, smax_scr[hd])
            shift = m_new - sbias_scr[hd] if moba else m_new
            p_scr[hd] = jnp.exp2(s_scr[hd] - shift).astype(BF16)
            alpha_scr[hd] = jnp.exp2(m_old - m_new)
            m_scr[hd] = m_new

    def pv_stage(j):
        v_t = v_t_ref[j]
        for hd in range(heads):
            v_ext = jnp.concatenate([v_t[hd * vdim:(hd + 1) * vdim], ones_rows], axis=0)
            acc_scr[hd] = alpha_scr[hd] * acc_scr[hd] + jnp.dot(
                v_ext, p_scr[hd], preferred_element_type=F32)

    def visited_block(v):
        return jnp.where(v == 0, own_lo + 1, jnp.where(v == 1, own_lo, v - 2))

    m_scr[...] = jnp.full(m_scr.shape, MASK_VALUE, F32)
    acc_scr[...] = jnp.zeros(acc_scr.shape, F32)
    qk_stage(own_lo + 1, own_offset=MOBA_BLOCK)
    softmax_stage()
    qk_stage(own_lo, own_offset=0)

    def body(t, carry):
        pv_stage(visited_block(t - 1))
        softmax_stage()
        qk_stage(t - 1)
        return carry

    last = own_lo + 1
    lax.fori_loop(1, last, body, 0)
    pv_stage(visited_block(last - 1))
    softmax_stage()
    pv_stage(visited_block(last))

    for hd in range(heads):
        acc = acc_scr[hd]
        o_t_ref[hd * vdim:(hd + 1) * vdim, :] = (acc[:vdim] / acc[vdim:vdim + 1]).astype(o_t_ref.dtype)


def _post_kernel(x_ref, oa_t_ref, ob_t_ref, gate_ref, w_oa_ref, w_ob_ref, w_out_ref,
                 g_mlp_ref, w_ff1_ref, w_ff2_ref, g_final_ref, out_ref, *, ff_chunk, final_norm):
    d_model = x_ref.shape[1]
    contract_rows = (((0,), (0,)), ((), ()))
    ya, yb = [], []
    for qt in range(ROW_TILE // Q_TILE):
        ya.append(lax.dot_general(oa_t_ref[qt], w_oa_ref[...], contract_rows, preferred_element_type=F32))
        yb.append(lax.dot_general(ob_t_ref[qt], w_ob_ref[...], contract_rows, preferred_element_type=F32))
    ya = ya[0] if len(ya) == 1 else jnp.concatenate(ya, axis=0)
    yb = yb[0] if len(yb) == 1 else jnp.concatenate(yb, axis=0)
    mixed = gate_ref[:, :d_model].astype(F32) * ya + gate_ref[:, d_model:].astype(F32) * yb
    x1 = x_ref[...] + jnp.dot(mixed.astype(BF16), w_out_ref[...], preferred_element_type=F32)

    h2 = _rms(x1, g_mlp_ref[...]).astype(BF16)
    mlp = jnp.zeros_like(x1)
    for c in range(w_ff1_ref.shape[1] // ff_chunk):
        u = jnp.maximum(jnp.dot(h2, w_ff1_ref[:, c * ff_chunk:(c + 1) * ff_chunk],
                                preferred_element_type=F32), 0.0)
        mlp = mlp + jnp.dot((u * u).astype(BF16), w_ff2_ref[c * ff_chunk:(c + 1) * ff_chunk, :],
                            preferred_element_type=F32)
    x2 = x1 + mlp
    out_ref[...] = _rms(x2, g_final_ref[...]) if final_norm else x2


def _resident(shape):
    return pl.BlockSpec(shape, lambda *_: (0,) * len(shape), pipeline_mode=pl.Buffered(1))


def _rope_tables(seq, dim):
    half = dim // 2
    inv_freq = np.power(ROPE_THETA, -np.arange(half, dtype=np.float64) / half)
    ang = np.arange(seq, dtype=np.float64)[:, None] * inv_freq[None, :]
    return np.cos(ang).astype(np.float32), np.sin(ang).astype(np.float32)


def _moba_group_columns():
    half = A_HEAD_DIM // 2
    order = []
    for pair in range(A_HEADS // 2):
        h0, h1 = 2 * pair * A_HEAD_DIM, (2 * pair + 1) * A_HEAD_DIM
        for start in (h0, h1, h0 + half, h1 + half):
            order.extend(range(start, start + half))
    return jnp.asarray(order, jnp.int32)


def _mla_slot_columns(nope, rope, xp=jnp):
    r = B_ROPE // 2
    split = LANES // 2 - r
    pad = xp.zeros(nope.shape[:-1] + (HEAD_SLOT - B_QK,), nope.dtype)
    return xp.concatenate([rope[..., :r], nope[..., :split], rope[..., r:], nope[..., split:], pad], axis=-1)


def _params(*semantics):
    return pltpu.CompilerParams(dimension_semantics=semantics, vmem_limit_bytes=VMEM_LIMIT_BYTES)


def _pre_call(x2, seq, g_mix, w_in, b_gate, g_cq, w_q_up, g_ckv, w_kv_up):
    rows, d_model = x2.shape
    a_width = A_HEADS * A_HEAD_DIM
    o_q, o_k, o_v = 0, a_width, 2 * a_width
    o_cq = 3 * a_width
    o_ckv = o_cq + B_Q_RANK
    o_kr = o_ckv + B_KV_RANK
    o_gate = o_kr + B_ROPE
    w_in = w_in.astype(BF16)
    cols = _moba_group_columns()
    w_qkv = jnp.concatenate([w_in[:, o_q:o_k][:, cols], w_in[:, o_k:o_v][:, cols], w_in[:, o_v:o_cq]], axis=1)
    w_cq = w_in[:, o_cq:o_ckv]
    w_ckv = w_in[:, o_ckv:o_kr]
    w_kr = _mla_slot_columns(jnp.zeros((d_model, B_NOPE), BF16), w_in[:, o_kr:o_gate])
    w_gate = w_in[:, o_gate:]
    w_q = w_q_up.astype(BF16).reshape(B_Q_RANK, B_HEADS, B_QK)
    w_qup = _mla_slot_columns(w_q[..., :B_NOPE], w_q[..., B_NOPE:]).reshape(B_Q_RANK, B_HEADS * HEAD_SLOT)
    w_kv = w_kv_up.astype(BF16).reshape(B_KV_RANK, B_HEADS, B_NOPE + B_V)
    w_kup = _mla_slot_columns(w_kv[..., :B_NOPE], jnp.zeros((B_KV_RANK, B_HEADS, B_ROPE), BF16)).reshape(
        B_KV_RANK, B_HEADS * HEAD_SLOT)
    w_vup = w_kv[:, :, B_NOPE:].reshape(B_KV_RANK, B_HEADS * B_V)

    cos_a, sin_a = _rope_tables(seq, A_HEAD_DIM)
    cos_a = np.tile(cos_a, (1, LANES // (A_HEAD_DIM // 2)))
    sin_a = np.concatenate([-sin_a, -sin_a, sin_a, sin_a], axis=1)
    cos_b, sin_b = _rope_tables(seq, B_ROPE)
    cos_b = _mla_slot_columns(np.ones((seq, B_NOPE), np.float32), np.concatenate([cos_b, cos_b], axis=1), np)
    sin_b = _mla_slot_columns(np.zeros((seq, B_NOPE), np.float32), np.concatenate([-sin_b, sin_b], axis=1), np)

    n_tiles = rows // ROW_TILE
    seq_tiles = seq // ROW_TILE
    row_spec = lambda w: pl.BlockSpec((ROW_TILE, w), lambda r: (r, 0))
    table_spec = pl.BlockSpec((ROW_TILE, LANES), lambda r: (r % seq_tiles, 0))
    t_spec = lambda h, width: pl.BlockSpec((ROW_TILE // width, h, width), lambda r: (r, 0, 0))
    t_shape = lambda h, width: jax.ShapeDtypeStruct((rows // width, h, width), BF16)
    vec = lambda a: a.reshape(1, -1).astype(F32)
    operands = [
        (x2, row_spec(d_model)), (vec(g_mix), _resident((1, d_model))),
        (cos_a, table_spec), (sin_a, table_spec), (cos_b, table_spec), (sin_b, table_spec),
        (w_qkv, _resident(w_qkv.shape)), (w_cq, _resident(w_cq.shape)), (w_ckv, _resident(w_ckv.shape)),
        (w_kr, _resident(w_kr.shape)), (w_gate, _resident(w_gate.shape)),
        (vec(b_gate), _resident((1, 2 * d_model))),
        (vec(g_cq), _resident((1, B_Q_RANK))), (w_qup, _resident(w_qup.shape)),
        (vec(g_ckv), _resident((1, B_KV_RANK))), (w_kup, _resident(w_kup.shape)),
        (w_vup, _resident(w_vup.shape)),
    ]
    out_shape = [
        t_shape(A_HEADS * HEAD_SLOT, Q_TILE),
        jax.ShapeDtypeStruct((rows, a_width), BF16),
        t_shape(a_width, K_TILE),
        jax.ShapeDtypeStruct((n_tiles, ROW_TILE // MOBA_BLOCK, a_width), F32),
        jax.ShapeDtypeStruct((rows, 2 * d_model), BF16),
        t_shape(B_HEADS * HEAD_SLOT, Q_TILE),
        jax.ShapeDtypeStruct((rows, B_HEADS * HEAD_SLOT), BF16),
        t_shape(B_HEADS * B_V, K_TILE),
    ]
    out_specs = [
        t_spec(A_HEADS * HEAD_SLOT, Q_TILE), row_spec(a_width), t_spec(a_width, K_TILE),
        pl.BlockSpec((1, ROW_TILE // MOBA_BLOCK, a_width), lambda r: (r, 0, 0)),
        row_spec(2 * d_model), t_spec(B_HEADS * HEAD_SLOT, Q_TILE), row_spec(B_HEADS * HEAD_SLOT),
        t_spec(B_HEADS * B_V, K_TILE),
    ]
    return pl.pallas_call(
        _pre_kernel,
        grid=(n_tiles,),
        in_specs=[s for _, s in operands],
        out_specs=out_specs,
        out_shape=out_shape,
        compiler_params=_params("parallel"),
        name="pre_proj",
    )(*[a for a, _ in operands])


def _attn_call(q_t, k, v_t, kmean, *, batch, seq, moba):
    nblk = seq // K_TILE
    n_q = seq // Q_TILE
    heads = q_t.shape[1] // HEAD_SLOT
    groups = heads // HEADS_PER_STEP
    vdim = v_t.shape[2] // heads
    key_lanes = k.shape[2] // groups
    kv_mode = None if moba else pl.Buffered(1)
    in_specs = [
        pl.BlockSpec((None, HEADS_PER_STEP * HEAD_SLOT, Q_TILE), lambda b, g, i: (b * n_q + i, g, 0)),
        pl.BlockSpec((None, seq, key_lanes), lambda b, g, i: (b, 0, g), pipeline_mode=kv_mode),
        pl.BlockSpec((None, nblk, HEADS_PER_STEP * vdim, K_TILE), lambda b, g, i: (b, 0, g, 0),
                     pipeline_mode=kv_mode),
    ]
    operands = [q_t, k, v_t]
    stat = pltpu.VMEM((HEADS_PER_STEP, 1, Q_TILE), F32)
    scratch = [
        pltpu.VMEM((HEADS_PER_STEP, K_TILE, Q_TILE), F32),
        stat,
        pltpu.VMEM((HEADS_PER_STEP, K_TILE, Q_TILE), BF16),
        stat,
        stat,
        pltpu.VMEM((HEADS_PER_STEP, vdim + SUM_ROWS, Q_TILE), F32),
    ]
    if moba:
        in_specs.append(pl.BlockSpec((None, nblk, key_lanes), lambda b, g, i: (b, 0, g)))
        operands.append(kmean)
        scratch.append(pltpu.VMEM((HEADS_PER_STEP, nblk, Q_TILE), F32))
        scratch.append(stat)
    return pl.pallas_call(
        functools.partial(_attn_kernel, moba=moba),
        grid=(batch, groups, n_q),
        in_specs=in_specs,
        out_specs=pl.BlockSpec((None, HEADS_PER_STEP * vdim, Q_TILE), lambda b, g, i: (b * n_q + i, g, 0)),
        out_shape=jax.ShapeDtypeStruct((batch * n_q, heads * vdim, Q_TILE), BF16),
        scratch_shapes=scratch,
        compiler_params=_params("parallel", "parallel", "arbitrary"),
        name="moba_attn" if moba else "mla_attn",
    )(*operands)


def _post_call(x2, oa_t, ob_t, gates, w_o_a, w_o_b, w_out, g_mlp, w_ff1, w_ff2, g_final, final_norm):
    rows, d_model = x2.shape
    q_per_tile = ROW_TILE // Q_TILE
    row_spec = lambda w: pl.BlockSpec((ROW_TILE, w), lambda r: (r, 0))
    t_spec = lambda a: pl.BlockSpec((q_per_tile,) + a.shape[1:], lambda r: (r, 0, 0))
    vec = lambda a: a.reshape(1, -1).astype(F32)
    weights = [w.astype(BF16) for w in (w_o_a, w_o_b, w_out)]
    operands = [(x2, row_spec(d_model)), (oa_t, t_spec(oa_t)), (ob_t, t_spec(ob_t)),
                (gates, row_spec(2 * d_model))]
    operands += [(w, _resident(w.shape)) for w in weights]
    operands += [(vec(g_mlp), _resident((1, d_model))),
                 (w_ff1.astype(BF16), _resident(w_ff1.shape)), (w_ff2.astype(BF16), _resident(w_ff2.shape)),
                 (vec(g_final), _resident((1, d_model)))]
    return pl.pallas_call(
        functools.partial(_post_kernel, ff_chunk=1024, final_norm=final_norm),
        grid=(rows // ROW_TILE,),
        in_specs=[s for _, s in operands],
        out_specs=row_spec(d_model),
        out_shape=jax.ShapeDtypeStruct((rows, d_model), F32),
        compiler_params=_params("parallel"),
        name="post_mlp",
    )(*[a for a, _ in operands])


def kernel(x, g_mix, w_in, b_gate, g_cq, w_q_up, g_ckv, w_kv_up, w_o_a, w_o_b, w_out, g_mlp, w_ff1,
           w_ff2, g_final):
    batch, seq, d_model = x.shape
    depth = w_in.shape[0]
    assert seq % ROW_TILE == 0 and ROW_TILE % MOBA_BLOCK == 0 and ROW_TILE % Q_TILE == 0
    assert A_HEADS % HEADS_PER_STEP == 0 and B_HEADS % HEADS_PER_STEP == 0 and HEADS_PER_STEP % 2 == 0
    nblk = seq // K_TILE
    x2 = x.reshape(batch * seq, d_model)
    for l in range(depth):
        qa_t, ka, va_t, kmean, gates, qb_t, kb, vb_t = _pre_call(
            x2, seq, g_mix[l], w_in[l], b_gate[l], g_cq[l], w_q_up[l], g_ckv[l], w_kv_up[l])
        oa_t = _attn_call(qa_t, ka.reshape(batch, seq, -1), va_t.reshape(batch, nblk, -1, K_TILE),
                          kmean.reshape(batch, nblk, -1), batch=batch, seq=seq, moba=True)
        ob_t = _attn_call(qb_t, kb.reshape(batch, seq, -1), vb_t.reshape(batch, nblk, -1, K_TILE),
                          None, batch=batch, seq=seq, moba=False)
        x2 = _post_call(x2, oa_t, ob_t, gates, w_o_a[l], w_o_b[l], w_out[l], g_mlp[l], w_ff1[l],
                        w_ff2[l], g_final, final_norm=(l == depth - 1))
    return x2.reshape(batch, seq, d_model)
```

```python
import functools

import jax
import jax.numpy as jnp
import numpy as np
from jax import lax
from jax.experimental import pallas as pl
from jax.experimental.pallas import tpu as pltpu

EPS = 1e-6
ROPE_THETA = 10000.0

A_HEADS = 8
A_HEAD_DIM = 64
MOBA_BLOCK = 256
MOBA_TOPK = 3

B_HEADS = 8
B_NOPE = 64
B_ROPE = 32
B_QK = B_NOPE + B_ROPE
B_V = 64
B_Q_RANK = 384
B_KV_RANK = 256

LANES = 128
HEAD_SLOT = 128
ROW_TILE = 512
K_TILE = MOBA_BLOCK
OWN_BLOCKS = 4
Q_TILE = OWN_BLOCKS * MOBA_BLOCK
HEADS_PER_STEP = 8
MASK_VALUE = -1e30
LOG2_E = 1.4426950408889634
SUM_ROWS = 16
VMEM_LIMIT_BYTES = 56 * 1024 * 1024

F32 = jnp.float32
BF16 = jnp.bfloat16


def _rms(x, g):
    return x * lax.rsqrt(jnp.mean(x * x, axis=-1, keepdims=True) + EPS) * g


def _rope_lanes(x, cos, sin_signed):
    groups = []
    for g in range(x.shape[1] // LANES):
        xg = x[:, g * LANES:(g + 1) * LANES]
        groups.append(xg * cos + pltpu.roll(xg, LANES // 2, axis=1) * sin_signed)
    return groups


def _store_transposed(dst_ref, row0, t):
    rows, width = t.shape[0], dst_ref.shape[2]
    for tile in range(ROW_TILE // width):
        dst_ref[tile, row0:row0 + rows, :] = t[:, tile * width:(tile + 1) * width].astype(dst_ref.dtype)


def _pre_kernel(x_ref, gmix_ref, cos_a_ref, sin_a_ref, cos_b_ref, sin_b_ref,
                w_qkv_ref, w_cq_ref, w_ckv_ref, w_kr_ref, w_gate_ref, b_gate_ref,
                g_cq_ref, w_qup_ref, g_ckv_ref, w_kup_ref, w_vup_ref,
                qa_t_ref, ka_ref, va_t_ref, kmean_ref, gate_ref, qb_t_ref, kb_ref, vb_t_ref):
    a_width = A_HEADS * A_HEAD_DIM
    h = _rms(x_ref[...], gmix_ref[...]).astype(BF16)

    qkv = jnp.dot(h, w_qkv_ref[...], preferred_element_type=F32)
    cos_a, sin_a = cos_a_ref[...], sin_a_ref[...]
    q_groups = _rope_lanes(qkv[:, :a_width], cos_a, sin_a)
    k_groups = _rope_lanes(qkv[:, a_width:2 * a_width], cos_a, sin_a)
    first_head = (lax.broadcasted_iota(jnp.int32, (LANES, ROW_TILE), 0) % A_HEAD_DIM) < A_HEAD_DIM // 2
    for g, qg in enumerate(q_groups):
        t = (qg * (A_HEAD_DIM ** -0.5 * LOG2_E)).T
        _store_transposed(qa_t_ref, (2 * g) * HEAD_SLOT, jnp.where(first_head, t, 0.0))
        _store_transposed(qa_t_ref, (2 * g + 1) * HEAD_SLOT, jnp.where(first_head, 0.0, t))
    for g, kg in enumerate(k_groups):
        ka_ref[:, g * LANES:(g + 1) * LANES] = kg.astype(BF16)
        for blk in range(ROW_TILE // MOBA_BLOCK):
            kmean_ref[0, blk:blk + 1, g * LANES:(g + 1) * LANES] = jnp.mean(
                kg[blk * MOBA_BLOCK:(blk + 1) * MOBA_BLOCK], axis=0, keepdims=True)
    for g in range(a_width // LANES):
        vg = qkv[:, 2 * a_width + g * LANES:2 * a_width + (g + 1) * LANES]
        _store_transposed(va_t_ref, g * LANES, vg.T)

    gate = jnp.dot(h, w_gate_ref[...], preferred_element_type=F32) + b_gate_ref[...]
    gate_ref[...] = jax.nn.sigmoid(gate).astype(BF16)

    cos_b, sin_b = cos_b_ref[...], sin_b_ref[...]
    cq = jnp.dot(h, w_cq_ref[...], preferred_element_type=F32)
    cqn = _rms(cq, g_cq_ref[...]).astype(BF16)
    qb = jnp.dot(cqn, w_qup_ref[...], preferred_element_type=F32)
    for hd, qg in enumerate(_rope_lanes(qb, cos_b, sin_b)):
        _store_transposed(qb_t_ref, hd * HEAD_SLOT, (qg * (B_QK ** -0.5 * LOG2_E)).T)
    ckv = jnp.dot(h, w_ckv_ref[...], preferred_element_type=F32)
    ckvn = _rms(ckv, g_ckv_ref[...]).astype(BF16)
    k_nope = jnp.dot(ckvn, w_kup_ref[...], preferred_element_type=F32)
    kr = jnp.dot(h, w_kr_ref[...], preferred_element_type=F32)
    kr = _rope_lanes(kr, cos_b, sin_b)[0]
    for hd in range(B_HEADS):
        kb_ref[:, hd * HEAD_SLOT:(hd + 1) * HEAD_SLOT] = (
            k_nope[:, hd * HEAD_SLOT:(hd + 1) * HEAD_SLOT] + kr).astype(BF16)
    vb = jnp.dot(ckvn, w_vup_ref[...], preferred_element_type=F32)
    for g in range(B_HEADS * B_V // LANES):
        _store_transposed(vb_t_ref, g * LANES, vb[:, g * LANES:(g + 1) * LANES].T)


def _attn_kernel(*refs, moba):
    if moba:
        q_t_ref, k_ref, v_t_ref, kmean_ref, o_t_ref = refs[:5]
        s_scr, smax_scr, p_scr, alpha_scr, m_scr, acc_scr, bias_scr, sbias_scr = refs[5:]
    else:
        q_t_ref, k_ref, v_t_ref, o_t_ref = refs[:4]
        s_scr, smax_scr, p_scr, alpha_scr, m_scr, acc_scr = refs[4:]
    own_lo = OWN_BLOCKS * pl.program_id(2)
    heads = HEADS_PER_STEP
    vdim = v_t_ref.shape[1] // heads
    qry_pos = lax.broadcasted_iota(jnp.int32, (1, Q_TILE), 1)

    def key_lanes(hd):
        g = hd // 2 if moba else hd
        return slice(g * LANES, (g + 1) * LANES)

    def q_head(hd):
        rows = slice(hd * HEAD_SLOT, (hd + 1) * HEAD_SLOT)
        return jnp.concatenate([q_t_ref[u, rows, :] for u in range(OWN_BLOCKS)], axis=1)

    if moba:
        nblk = kmean_ref.shape[0]
        blk = lax.broadcasted_iota(jnp.int32, (nblk, Q_TILE), 0).astype(F32)
        own = (own_lo + qry_pos // MOBA_BLOCK).astype(F32)
        past = blk < own
        for hd in range(heads):
            gate = jnp.dot(kmean_ref[:, key_lanes(hd)].astype(BF16), q_head(hd),
                           preferred_element_type=F32)
            gate = jnp.where(past, gate, -jnp.inf)
            bias = jnp.full(gate.shape, MASK_VALUE, F32)
            for _ in range(MOBA_TOPK):
                best = jnp.max(gate, axis=0, keepdims=True)
                first = jnp.min(jnp.where(gate == best, blk, float(nblk)), axis=0, keepdims=True)
                pick = blk == first
                bias = jnp.where(pick, 0.0, bias)
                gate = jnp.where(pick, -jnp.inf, gate)
            bias_scr[hd] = jnp.where(past, bias, MASK_VALUE)

    def qk_stage(j, own_offset=None):
        start = pl.multiple_of(j * K_TILE, K_TILE)
        k_blk = k_ref[pl.ds(start, K_TILE), :]
        for hd in range(heads):
            s = jnp.dot(k_blk[:, key_lanes(hd)], q_head(hd), preferred_element_type=F32)
            if own_offset is not None:
                key_pos = lax.broadcasted_iota(jnp.int32, s.shape, 0) + own_offset
                s = jnp.where(key_pos <= qry_pos, s, MASK_VALUE)
            s_scr[hd] = s
            smax = jnp.max(s, axis=0, keepdims=True)
            if moba:
                bias = bias_scr[hd, pl.ds(j, 1), :]
                if own_offset is not None:
                    bias = jnp.where(qry_pos >= own_offset + MOBA_BLOCK, bias, 0.0)
                sbias_scr[hd] = bias
                smax = smax + bias
            smax_scr[hd] = smax

    ones_rows = jnp.ones((SUM_ROWS, K_TILE), BF16)

    def softmax_stage():
        for hd in range(heads):
            m_old = m_scr[hd]
            m_new = jnp.maximum(m_old, smax_scr[hd])
            shift = m_new - sbias_scr[hd] if moba else m_new
            p_scr[hd] = jnp.exp2(s_scr[hd] - shift).astype(BF16)
            alpha_scr[hd] = jnp.exp2(m_old - m_new)
            m_scr[hd] = m_new

    def pv_stage(j):
        v_t = v_t_ref[j]
        for hd in range(heads):
            v_ext = jnp.concatenate([v_t[hd * vdim:(hd + 1) * vdim], ones_rows], axis=0)
            acc_scr[hd] = alpha_scr[hd] * acc_scr[hd] + jnp.dot(
                v_ext, p_scr[hd], preferred_element_type=F32)

    def visited_block(v):
        return jnp.where(v < OWN_BLOCKS, own_lo + (OWN_BLOCKS - 1) - v, v - OWN_BLOCKS)

    m_scr[...] = jnp.full(m_scr.shape, MASK_VALUE, F32)
    acc_scr[...] = jnp.zeros(acc_scr.shape, F32)
    qk_stage(own_lo + OWN_BLOCKS - 1, own_offset=(OWN_BLOCKS - 1) * MOBA_BLOCK)
    for v in range(OWN_BLOCKS - 1):
        if v > 0:
            pv_stage(own_lo + OWN_BLOCKS - v)
        softmax_stage()
        qk_stage(own_lo + OWN_BLOCKS - 2 - v, own_offset=(OWN_BLOCKS - 2 - v) * MOBA_BLOCK)

    def body(t, carry):
        pv_stage(visited_block(t - 1))
        softmax_stage()
        qk_stage(t + 1 - OWN_BLOCKS)
        return carry

    last = own_lo + OWN_BLOCKS - 1
    if OWN_BLOCKS == 1:
        alpha_scr[...] = jnp.ones(alpha_scr.shape, F32)
        p_scr[...] = jnp.zeros(p_scr.shape, BF16)
    lax.fori_loop(OWN_BLOCKS - 1, last, body, 0)
    pv_stage(visited_block(last - 1))
    softmax_stage()
    pv_stage(visited_block(last))

    for hd in range(heads):
        acc = acc_scr[hd]
        o = (acc[:vdim] / acc[vdim:vdim + 1]).astype(o_t_ref.dtype)
        for u in range(OWN_BLOCKS):
            o_t_ref[u, hd * vdim:(hd + 1) * vdim, :] = o[:, u * MOBA_BLOCK:(u + 1) * MOBA_BLOCK]


def _post_kernel(x_ref, oa_t_ref, ob_t_ref, gate_ref, w_oa_ref, w_ob_ref, w_out_ref,
                 g_mlp_ref, w_ff1_ref, w_ff2_ref, g_final_ref, out_ref, *, ff_chunk, final_norm):
    d_model = x_ref.shape[1]
    contract_rows = (((0,), (0,)), ((), ()))
    ya, yb = [], []
    for qt in range(ROW_TILE // K_TILE):
        ya.append(lax.dot_general(oa_t_ref[qt], w_oa_ref[...], contract_rows, preferred_element_type=F32))
        yb.append(lax.dot_general(ob_t_ref[qt], w_ob_ref[...], contract_rows, preferred_element_type=F32))
    ya = ya[0] if len(ya) == 1 else jnp.concatenate(ya, axis=0)
    yb = yb[0] if len(yb) == 1 else jnp.concatenate(yb, axis=0)
    mixed = gate_ref[:, :d_model].astype(F32) * ya + gate_ref[:, d_model:].astype(F32) * yb
    x1 = x_ref[...] + jnp.dot(mixed.astype(BF16), w_out_ref[...], preferred_element_type=F32)

    h2 = _rms(x1, g_mlp_ref[...]).astype(BF16)
    mlp = jnp.zeros_like(x1)
    for c in range(w_ff1_ref.shape[1] // ff_chunk):
        u = jnp.maximum(jnp.dot(h2, w_ff1_ref[:, c * ff_chunk:(c + 1) * ff_chunk],
                                preferred_element_type=F32), 0.0)
        mlp = mlp + jnp.dot((u * u).astype(BF16), w_ff2_ref[c * ff_chunk:(c + 1) * ff_chunk, :],
                            preferred_element_type=F32)
    x2 = x1 + mlp
    out_ref[...] = _rms(x2, g_final_ref[...]) if final_norm else x2


def _resident(shape):
    return pl.BlockSpec(shape, lambda *_: (0,) * len(shape), pipeline_mode=pl.Buffered(1))


def _rope_tables(seq, dim):
    half = dim // 2
    inv_freq = np.power(ROPE_THETA, -np.arange(half, dtype=np.float64) / half)
    ang = np.arange(seq, dtype=np.float64)[:, None] * inv_freq[None, :]
    return np.cos(ang).astype(np.float32), np.sin(ang).astype(np.float32)


def _moba_group_columns():
    half = A_HEAD_DIM // 2
    order = []
    for pair in range(A_HEADS // 2):
        h0, h1 = 2 * pair * A_HEAD_DIM, (2 * pair + 1) * A_HEAD_DIM
        for start in (h0, h1, h0 + half, h1 + half):
            order.extend(range(start, start + half))
    return jnp.asarray(order, jnp.int32)


def _mla_slot_columns(nope, rope, xp=jnp):
    r = B_ROPE // 2
    split = LANES // 2 - r
    pad = xp.zeros(nope.shape[:-1] + (HEAD_SLOT - B_QK,), nope.dtype)
    return xp.concatenate([rope[..., :r], nope[..., :split], rope[..., r:], nope[..., split:], pad], axis=-1)


def _params(*semantics):
    return pltpu.CompilerParams(dimension_semantics=semantics, vmem_limit_bytes=VMEM_LIMIT_BYTES)


def _pre_call(x2, seq, g_mix, w_in, b_gate, g_cq, w_q_up, g_ckv, w_kv_up):
    rows, d_model = x2.shape
    a_width = A_HEADS * A_HEAD_DIM
    o_q, o_k, o_v = 0, a_width, 2 * a_width
    o_cq = 3 * a_width
    o_ckv = o_cq + B_Q_RANK
    o_kr = o_ckv + B_KV_RANK
    o_gate = o_kr + B_ROPE
    w_in = w_in.astype(BF16)
    cols = _moba_group_columns()
    w_qkv = jnp.concatenate([w_in[:, o_q:o_k][:, cols], w_in[:, o_k:o_v][:, cols], w_in[:, o_v:o_cq]], axis=1)
    w_cq = w_in[:, o_cq:o_ckv]
    w_ckv = w_in[:, o_ckv:o_kr]
    w_kr = _mla_slot_columns(jnp.zeros((d_model, B_NOPE), BF16), w_in[:, o_kr:o_gate])
    w_gate = w_in[:, o_gate:]
    w_q = w_q_up.astype(BF16).reshape(B_Q_RANK, B_HEADS, B_QK)
    w_qup = _mla_slot_columns(w_q[..., :B_NOPE], w_q[..., B_NOPE:]).reshape(B_Q_RANK, B_HEADS * HEAD_SLOT)
    w_kv = w_kv_up.astype(BF16).reshape(B_KV_RANK, B_HEADS, B_NOPE + B_V)
    w_kup = _mla_slot_columns(w_kv[..., :B_NOPE], jnp.zeros((B_KV_RANK, B_HEADS, B_ROPE), BF16)).reshape(
        B_KV_RANK, B_HEADS * HEAD_SLOT)
    w_vup = w_kv[:, :, B_NOPE:].reshape(B_KV_RANK, B_HEADS * B_V)

    cos_a, sin_a = _rope_tables(seq, A_HEAD_DIM)
    cos_a = np.tile(cos_a, (1, LANES // (A_HEAD_DIM // 2)))
    sin_a = np.concatenate([-sin_a, -sin_a, sin_a, sin_a], axis=1)
    cos_b, sin_b = _rope_tables(seq, B_ROPE)
    cos_b = _mla_slot_columns(np.ones((seq, B_NOPE), np.float32), np.concatenate([cos_b, cos_b], axis=1), np)
    sin_b = _mla_slot_columns(np.zeros((seq, B_NOPE), np.float32), np.concatenate([-sin_b, sin_b], axis=1), np)

    n_tiles = rows // ROW_TILE
    seq_tiles = seq // ROW_TILE
    row_spec = lambda w: pl.BlockSpec((ROW_TILE, w), lambda r: (r, 0))
    table_spec = pl.BlockSpec((ROW_TILE, LANES), lambda r: (r % seq_tiles, 0))
    t_spec = lambda h, width: pl.BlockSpec((ROW_TILE // width, h, width), lambda r: (r, 0, 0))
    t_shape = lambda h, width: jax.ShapeDtypeStruct((rows // width, h, width), BF16)
    vec = lambda a: a.reshape(1, -1).astype(F32)
    operands = [
        (x2, row_spec(d_model)), (vec(g_mix), _resident((1, d_model))),
        (cos_a, table_spec), (sin_a, table_spec), (cos_b, table_spec), (sin_b, table_spec),
        (w_qkv, _resident(w_qkv.shape)), (w_cq, _resident(w_cq.shape)), (w_ckv, _resident(w_ckv.shape)),
        (w_kr, _resident(w_kr.shape)), (w_gate, _resident(w_gate.shape)),
        (vec(b_gate), _resident((1, 2 * d_model))),
        (vec(g_cq), _resident((1, B_Q_RANK))), (w_qup, _resident(w_qup.shape)),
        (vec(g_ckv), _resident((1, B_KV_RANK))), (w_kup, _resident(w_kup.shape)),
        (w_vup, _resident(w_vup.shape)),
    ]
    out_shape = [
        t_shape(A_HEADS * HEAD_SLOT, K_TILE),
        jax.ShapeDtypeStruct((rows, a_width), BF16),
        t_shape(a_width, K_TILE),
        jax.ShapeDtypeStruct((n_tiles, ROW_TILE // MOBA_BLOCK, a_width), F32),
        jax.ShapeDtypeStruct((rows, 2 * d_model), BF16),
        t_shape(B_HEADS * HEAD_SLOT, K_TILE),
        jax.ShapeDtypeStruct((rows, B_HEADS * HEAD_SLOT), BF16),
        t_shape(B_HEADS * B_V, K_TILE),
    ]
    out_specs = [
        t_spec(A_HEADS * HEAD_SLOT, K_TILE), row_spec(a_width), t_spec(a_width, K_TILE),
        pl.BlockSpec((1, ROW_TILE // MOBA_BLOCK, a_width), lambda r: (r, 0, 0)),
        row_spec(2 * d_model), t_spec(B_HEADS * HEAD_SLOT, K_TILE), row_spec(B_HEADS * HEAD_SLOT),
        t_spec(B_HEADS * B_V, K_TILE),
    ]
    return pl.pallas_call(
        _pre_kernel,
        grid=(n_tiles,),
        in_specs=[s for _, s in operands],
        out_specs=out_specs,
        out_shape=out_shape,
        compiler_params=_params("parallel"),
        name="pre_proj",
    )(*[a for a, _ in operands])


def _attn_call(q_t, k, v_t, kmean, *, batch, seq, moba):
    nblk = seq // K_TILE
    n_q = seq // Q_TILE
    heads = q_t.shape[1] // HEAD_SLOT
    groups = heads // HEADS_PER_STEP
    vdim = v_t.shape[2] // heads
    key_lanes = k.shape[2] // groups
    kv_mode = None if (moba and OWN_BLOCKS <= 2) else pl.Buffered(1)
    in_specs = [
        pl.BlockSpec((OWN_BLOCKS, HEADS_PER_STEP * HEAD_SLOT, K_TILE), lambda b, g, i: (b * n_q + i, g, 0)),
        pl.BlockSpec((None, seq, key_lanes), lambda b, g, i: (b, 0, g), pipeline_mode=kv_mode),
        pl.BlockSpec((None, nblk, HEADS_PER_STEP * vdim, K_TILE), lambda b, g, i: (b, 0, g, 0),
                     pipeline_mode=kv_mode),
    ]
    operands = [q_t, k, v_t]
    stat = pltpu.VMEM((HEADS_PER_STEP, 1, Q_TILE), F32)
    scratch = [
        pltpu.VMEM((HEADS_PER_STEP, K_TILE, Q_TILE), F32),
        stat,
        pltpu.VMEM((HEADS_PER_STEP, K_TILE, Q_TILE), BF16),
        stat,
        stat,
        pltpu.VMEM((HEADS_PER_STEP, vdim + SUM_ROWS, Q_TILE), F32),
    ]
    if moba:
        in_specs.append(pl.BlockSpec((None, nblk, key_lanes), lambda b, g, i: (b, 0, g)))
        operands.append(kmean)
        scratch.append(pltpu.VMEM((HEADS_PER_STEP, nblk, Q_TILE), F32))
        scratch.append(stat)
    return pl.pallas_call(
        functools.partial(_attn_kernel, moba=moba),
        grid=(batch, groups, n_q),
        in_specs=in_specs,
        out_specs=pl.BlockSpec((OWN_BLOCKS, HEADS_PER_STEP * vdim, K_TILE), lambda b, g, i: (b * n_q + i, g, 0)),
        out_shape=jax.ShapeDtypeStruct((batch * nblk, heads * vdim, K_TILE), BF16),
        scratch_shapes=scratch,
        compiler_params=_params("parallel", "parallel", "arbitrary"),
        name="moba_attn" if moba else "mla_attn",
    )(*operands)


def _post_call(x2, oa_t, ob_t, gates, w_o_a, w_o_b, w_out, g_mlp, w_ff1, w_ff2, g_final, final_norm):
    rows, d_model = x2.shape
    row_spec = lambda w: pl.BlockSpec((ROW_TILE, w), lambda r: (r, 0))
    t_spec = lambda a: pl.BlockSpec((ROW_TILE // K_TILE,) + a.shape[1:], lambda r: (r, 0, 0))
    vec = lambda a: a.reshape(1, -1).astype(F32)
    weights = [w.astype(BF16) for w in (w_o_a, w_o_b, w_out)]
    operands = [(x2, row_spec(d_model)), (oa_t, t_spec(oa_t)), (ob_t, t_spec(ob_t)),
                (gates, row_spec(2 * d_model))]
    operands += [(w, _resident(w.shape)) for w in weights]
    operands += [(vec(g_mlp), _resident((1, d_model))),
                 (w_ff1.astype(BF16), _resident(w_ff1.shape)), (w_ff2.astype(BF16), _resident(w_ff2.shape)),
                 (vec(g_final), _resident((1, d_model)))]
    return pl.pallas_call(
        functools.partial(_post_kernel, ff_chunk=1024, final_norm=final_norm),
        grid=(rows // ROW_TILE,),
        in_specs=[s for _, s in operands],
        out_specs=row_spec(d_model),
        out_shape=jax.ShapeDtypeStruct((rows, d_model), F32),
        compiler_params=_params("parallel"),
        name="post_mlp",
    )(*[a for a, _ in operands])


def kernel(x, g_mix, w_in, b_gate, g_cq, w_q_up, g_ckv, w_kv_up, w_o_a, w_o_b, w_out, g_mlp, w_ff1,
           w_ff2, g_final):
    batch, seq, d_model = x.shape
    depth = w_in.shape[0]
    assert seq % ROW_TILE == 0 and ROW_TILE % MOBA_BLOCK == 0 and seq % Q_TILE == 0
    assert A_HEADS % HEADS_PER_STEP == 0 and B_HEADS % HEADS_PER_STEP == 0 and HEADS_PER_STEP % 2 == 0
    nblk = seq // K_TILE
    x2 = x.reshape(batch * seq, d_model)
    for l in range(depth):
        qa_t, ka, va_t, kmean, gates, qb_t, kb, vb_t = _pre_call(
            x2, seq, g_mix[l], w_in[l], b_gate[l], g_cq[l], w_q_up[l], g_ckv[l], w_kv_up[l])
        oa_t = _attn_call(qa_t, ka.reshape(batch, seq, -1), va_t.reshape(batch, nblk, -1, K_TILE),
                          kmean.reshape(batch, nblk, -1), batch=batch, seq=seq, moba=True)
        ob_t = _attn_call(qb_t, kb.reshape(batch, seq, -1), vb_t.reshape(batch, nblk, -1, K_TILE),
                          None, batch=batch, seq=seq, moba=False)
        x2 = _post_call(x2, oa_t, ob_t, gates, w_o_a[l], w_o_b[l], w_out[l], g_mlp[l], w_ff1[l],
                        w_ff2[l], g_final, final_norm=(l == depth - 1))
    return x2.reshape(batch, seq, d_model)
```

```python
import functools

import jax
import jax.numpy as jnp
import numpy as np
from jax import lax
from jax.experimental import pallas as pl
from jax.experimental.pallas import tpu as pltpu

EPS = 1e-6
ROPE_THETA = 10000.0

A_HEADS = 8
A_HEAD_DIM = 64
MOBA_BLOCK = 256
MOBA_TOPK = 3

B_HEADS = 8
B_NOPE = 64
B_ROPE = 32
B_QK = B_NOPE + B_ROPE
B_V = 64
B_Q_RANK = 384
B_KV_RANK = 256

LANES = 128
HEAD_SLOT = 128
ROW_TILE = 512
K_TILE = MOBA_BLOCK
OWN_BLOCKS = 4
Q_TILE = OWN_BLOCKS * MOBA_BLOCK
HEADS_PER_STEP = 8
MASK_VALUE = -1e30
LOG2_E = 1.4426950408889634
SUM_ROWS = 16
VMEM_LIMIT_BYTES = 56 * 1024 * 1024

F32 = jnp.float32
BF16 = jnp.bfloat16


def _rms(x, g):
    return x * lax.rsqrt(jnp.mean(x * x, axis=-1, keepdims=True) + EPS) * g


def _rope_lanes(x, cos, sin_signed):
    groups = []
    for g in range(x.shape[1] // LANES):
        xg = x[:, g * LANES:(g + 1) * LANES]
        groups.append(xg * cos + pltpu.roll(xg, LANES // 2, axis=1) * sin_signed)
    return groups


def _store_transposed(dst_ref, row0, t):
    rows, width = t.shape[0], dst_ref.shape[2]
    for tile in range(ROW_TILE // width):
        dst_ref[tile, row0:row0 + rows, :] = t[:, tile * width:(tile + 1) * width].astype(dst_ref.dtype)


def _pre_kernel(x_ref, gmix_ref, cos_a_ref, sin_a_ref, cos_b_ref, sin_b_ref,
                w_qkv_ref, w_cq_ref, w_ckv_ref, w_kr_ref, w_gate_ref, b_gate_ref,
                g_cq_ref, w_qup_ref, g_ckv_ref, w_kup_ref, w_vup_ref,
                qa_t_ref, ka_ref, va_t_ref, kmean_ref, gate_ref, qb_t_ref, kb_ref, vb_t_ref):
    a_width = A_HEADS * A_HEAD_DIM
    h = _rms(x_ref[...], gmix_ref[...]).astype(BF16)

    qkv = jnp.dot(h, w_qkv_ref[...], preferred_element_type=F32)
    cos_a, sin_a = cos_a_ref[...], sin_a_ref[...]
    q_groups = _rope_lanes(qkv[:, :a_width], cos_a, sin_a)
    k_groups = _rope_lanes(qkv[:, a_width:2 * a_width], cos_a, sin_a)
    first_head = (lax.broadcasted_iota(jnp.int32, (LANES, ROW_TILE), 0) % A_HEAD_DIM) < A_HEAD_DIM // 2
    for g, qg in enumerate(q_groups):
        t = (qg * (A_HEAD_DIM ** -0.5 * LOG2_E)).T
        _store_transposed(qa_t_ref, (2 * g) * HEAD_SLOT, jnp.where(first_head, t, 0.0))
        _store_transposed(qa_t_ref, (2 * g + 1) * HEAD_SLOT, jnp.where(first_head, 0.0, t))
    for g, kg in enumerate(k_groups):
        ka_ref[:, g * LANES:(g + 1) * LANES] = kg.astype(BF16)
        for blk in range(ROW_TILE // MOBA_BLOCK):
            kmean_ref[0, blk:blk + 1, g * LANES:(g + 1) * LANES] = jnp.mean(
                kg[blk * MOBA_BLOCK:(blk + 1) * MOBA_BLOCK], axis=0, keepdims=True)
    for g in range(a_width // LANES):
        vg = qkv[:, 2 * a_width + g * LANES:2 * a_width + (g + 1) * LANES]
        _store_transposed(va_t_ref, g * LANES, vg.T)

    gate = jnp.dot(h, w_gate_ref[...], preferred_element_type=F32) + b_gate_ref[...]
    gate_ref[...] = jax.nn.sigmoid(gate).astype(BF16)

    cos_b, sin_b = cos_b_ref[...], sin_b_ref[...]
    cq = jnp.dot(h, w_cq_ref[...], preferred_element_type=F32)
    cqn = _rms(cq, g_cq_ref[...]).astype(BF16)
    qb = jnp.dot(cqn, w_qup_ref[...], preferred_element_type=F32)
    for hd, qg in enumerate(_rope_lanes(qb, cos_b, sin_b)):
        _store_transposed(qb_t_ref, hd * HEAD_SLOT, (qg * (B_QK ** -0.5 * LOG2_E)).T)
    ckv = jnp.dot(h, w_ckv_ref[...], preferred_element_type=F32)
    ckvn = _rms(ckv, g_ckv_ref[...]).astype(BF16)
    k_nope = jnp.dot(ckvn, w_kup_ref[...], preferred_element_type=F32)
    kr = jnp.dot(h, w_kr_ref[...], preferred_element_type=F32)
    kr = _rope_lanes(kr, cos_b, sin_b)[0]
    for hd in range(B_HEADS):
        kb_ref[:, hd * HEAD_SLOT:(hd + 1) * HEAD_SLOT] = (
            k_nope[:, hd * HEAD_SLOT:(hd + 1) * HEAD_SLOT] + kr).astype(BF16)
    vb = jnp.dot(ckvn, w_vup_ref[...], preferred_element_type=F32)
    for g in range(B_HEADS * B_V // LANES):
        _store_transposed(vb_t_ref, g * LANES, vb[:, g * LANES:(g + 1) * LANES].T)


def _attn_kernel(*refs, moba):
    if moba:
        q_t_ref, k_ref, v_t_ref, kmean_ref, o_t_ref = refs[:5]
        s_scr, smax_scr, p_scr, alpha_scr, m_scr, acc_scr, bias_scr, sbias_scr = refs[5:]
    else:
        q_t_ref, k_ref, v_t_ref, o_t_ref = refs[:4]
        s_scr, smax_scr, p_scr, alpha_scr, m_scr, acc_scr = refs[4:]
    own_lo = OWN_BLOCKS * pl.program_id(2)
    heads = HEADS_PER_STEP
    vdim = v_t_ref.shape[1] // heads
    qry_pos = lax.broadcasted_iota(jnp.int32, (1, Q_TILE), 1)

    def key_lanes(hd):
        g = hd // 2 if moba else hd
        return slice(g * LANES, (g + 1) * LANES)

    def q_head(hd, first=0):
        rows = slice(hd * HEAD_SLOT, (hd + 1) * HEAD_SLOT)
        return jnp.concatenate([q_t_ref[u, rows, :] for u in range(first, OWN_BLOCKS)], axis=1)

    if moba:
        nblk = kmean_ref.shape[0]
        blk = lax.broadcasted_iota(jnp.int32, (nblk, Q_TILE), 0).astype(F32)
        own = (own_lo + qry_pos // MOBA_BLOCK).astype(F32)
        past = blk < own
        for hd in range(heads):
            gate = jnp.dot(kmean_ref[:, key_lanes(hd)].astype(BF16), q_head(hd),
                           preferred_element_type=F32)
            gate = jnp.where(past, gate, -jnp.inf)
            bias = jnp.full(gate.shape, MASK_VALUE, F32)
            for _ in range(MOBA_TOPK):
                best = jnp.max(gate, axis=0, keepdims=True)
                first = jnp.min(jnp.where(gate == best, blk, float(nblk)), axis=0, keepdims=True)
                pick = blk == first
                bias = jnp.where(pick, 0.0, bias)
                gate = jnp.where(pick, -jnp.inf, gate)
            bias_scr[hd] = jnp.where(past, bias, MASK_VALUE)


    def qk_stage(j, first=0, own=False):
        start = pl.multiple_of(j * K_TILE, K_TILE)
        k_blk = k_ref[pl.ds(start, K_TILE), :]
        cols = slice(first * MOBA_BLOCK, Q_TILE)
        for hd in range(heads):
            s = jnp.dot(k_blk[:, key_lanes(hd)], q_head(hd, first), preferred_element_type=F32)
            if own:
                key_pos = lax.broadcasted_iota(jnp.int32, s.shape, 0)
                s = jnp.where(key_pos <= qry_pos[:, :s.shape[1]], s, MASK_VALUE)
            s_scr[hd, :, cols] = s
            smax = jnp.max(s, axis=0, keepdims=True)
            if moba:
                bias = bias_scr[hd, pl.ds(j, 1), cols]
                if own:
                    bias = jnp.where(qry_pos[:, :s.shape[1]] >= MOBA_BLOCK, bias, 0.0)
                sbias_scr[hd, :, cols] = bias
                smax = smax + bias
            smax_scr[hd, :, cols] = smax

    ones_rows = jnp.ones((SUM_ROWS, K_TILE), BF16)

    def softmax_stage(first=0):
        cols = slice(first * MOBA_BLOCK, Q_TILE)
        for hd in range(heads):
            m_old = m_scr[hd, :, cols]
            m_new = jnp.maximum(m_old, smax_scr[hd, :, cols])
            shift = m_new - sbias_scr[hd, :, cols] if moba else m_new
            p_scr[hd, :, cols] = jnp.exp2(s_scr[hd, :, cols] - shift).astype(BF16)
            alpha_scr[hd, :, cols] = jnp.exp2(m_old - m_new)
            m_scr[hd, :, cols] = m_new

    def pv_stage(j, first=0):
        v_t = v_t_ref[j]
        cols = slice(first * MOBA_BLOCK, Q_TILE)
        for hd in range(heads):
            v_ext = jnp.concatenate([v_t[hd * vdim:(hd + 1) * vdim], ones_rows], axis=0)
            acc_scr[hd, :, cols] = alpha_scr[hd, :, cols] * acc_scr[hd, :, cols] + jnp.dot(
                v_ext, p_scr[hd, :, cols], preferred_element_type=F32)

    def visited_block(v):
        return jnp.where(v < OWN_BLOCKS, own_lo + (OWN_BLOCKS - 1) - v, v - OWN_BLOCKS)

    m_scr[...] = jnp.full(m_scr.shape, MASK_VALUE, F32)
    acc_scr[...] = jnp.zeros(acc_scr.shape, F32)
    p_scr[:, :, :MOBA_BLOCK] = jnp.zeros(p_scr.shape[:2] + (MOBA_BLOCK,), BF16)
    alpha_scr[:, :, :MOBA_BLOCK] = jnp.ones(alpha_scr.shape[:2] + (MOBA_BLOCK,), F32)
    qk_stage(own_lo + OWN_BLOCKS - 1, OWN_BLOCKS - 1, own=True)
    for v in range(OWN_BLOCKS - 1):
        u = OWN_BLOCKS - 1 - v
        if v > 0:
            pv_stage(own_lo + u + 1, u + 1)
        softmax_stage(u)
        qk_stage(own_lo + u - 1, u - 1, own=True)

    def body(t, carry):
        pv_stage(visited_block(t - 1))
        softmax_stage()
        qk_stage(t + 1 - OWN_BLOCKS)
        return carry

    last = own_lo + OWN_BLOCKS - 1
    lax.fori_loop(OWN_BLOCKS - 1, last, body, 0)
    pv_stage(visited_block(last - 1))
    softmax_stage()
    pv_stage(visited_block(last))

    for hd in range(heads):
        acc = acc_scr[hd]
        o = (acc[:vdim] / acc[vdim:vdim + 1]).astype(o_t_ref.dtype)
        for u in range(OWN_BLOCKS):
            o_t_ref[u, hd * vdim:(hd + 1) * vdim, :] = o[:, u * MOBA_BLOCK:(u + 1) * MOBA_BLOCK]


def _post_kernel(x_ref, oa_t_ref, ob_t_ref, gate_ref, w_oa_ref, w_ob_ref, w_out_ref,
                 g_mlp_ref, w_ff1_ref, w_ff2_ref, g_final_ref, out_ref, *, ff_chunk, final_norm):
    d_model = x_ref.shape[1]
    contract_rows = (((0,), (0,)), ((), ()))
    ya, yb = [], []
    for qt in range(ROW_TILE // K_TILE):
        ya.append(lax.dot_general(oa_t_ref[qt], w_oa_ref[...], contract_rows, preferred_element_type=F32))
        yb.append(lax.dot_general(ob_t_ref[qt], w_ob_ref[...], contract_rows, preferred_element_type=F32))
    ya = ya[0] if len(ya) == 1 else jnp.concatenate(ya, axis=0)
    yb = yb[0] if len(yb) == 1 else jnp.concatenate(yb, axis=0)
    mixed = gate_ref[:, :d_model].astype(F32) * ya + gate_ref[:, d_model:].astype(F32) * yb
    x1 = x_ref[...] + jnp.dot(mixed.astype(BF16), w_out_ref[...], preferred_element_type=F32)

    h2 = _rms(x1, g_mlp_ref[...]).astype(BF16)
    mlp = jnp.zeros_like(x1)
    for c in range(w_ff1_ref.shape[1] // ff_chunk):
        u = jnp.maximum(jnp.dot(h2, w_ff1_ref[:, c * ff_chunk:(c + 1) * ff_chunk],
                                preferred_element_type=F32), 0.0)
        mlp = mlp + jnp.dot((u * u).astype(BF16), w_ff2_ref[c * ff_chunk:(c + 1) * ff_chunk, :],
                            preferred_element_type=F32)
    x2 = x1 + mlp
    out_ref[...] = _rms(x2, g_final_ref[...]) if final_norm else x2


def _resident(shape):
    return pl.BlockSpec(shape, lambda *_: (0,) * len(shape), pipeline_mode=pl.Buffered(1))


def _rope_tables(seq, dim):
    half = dim // 2
    inv_freq = np.power(ROPE_THETA, -np.arange(half, dtype=np.float64) / half)
    ang = np.arange(seq, dtype=np.float64)[:, None] * inv_freq[None, :]
    return np.cos(ang).astype(np.float32), np.sin(ang).astype(np.float32)


def _moba_group_columns():
    half = A_HEAD_DIM // 2
    order = []
    for pair in range(A_HEADS // 2):
        h0, h1 = 2 * pair * A_HEAD_DIM, (2 * pair + 1) * A_HEAD_DIM
        for start in (h0, h1, h0 + half, h1 + half):
            order.extend(range(start, start + half))
    return jnp.asarray(order, jnp.int32)


def _mla_slot_columns(nope, rope, xp=jnp):
    r = B_ROPE // 2
    split = LANES // 2 - r
    pad = xp.zeros(nope.shape[:-1] + (HEAD_SLOT - B_QK,), nope.dtype)
    return xp.concatenate([rope[..., :r], nope[..., :split], rope[..., r:], nope[..., split:], pad], axis=-1)


def _params(*semantics):
    return pltpu.CompilerParams(dimension_semantics=semantics, vmem_limit_bytes=VMEM_LIMIT_BYTES)


def _pre_call(x2, seq, g_mix, w_in, b_gate, g_cq, w_q_up, g_ckv, w_kv_up):
    rows, d_model = x2.shape
    a_width = A_HEADS * A_HEAD_DIM
    o_q, o_k, o_v = 0, a_width, 2 * a_width
    o_cq = 3 * a_width
    o_ckv = o_cq + B_Q_RANK
    o_kr = o_ckv + B_KV_RANK
    o_gate = o_kr + B_ROPE
    w_in = w_in.astype(BF16)
    cols = _moba_group_columns()
    w_qkv = jnp.concatenate([w_in[:, o_q:o_k][:, cols], w_in[:, o_k:o_v][:, cols], w_in[:, o_v:o_cq]], axis=1)
    w_cq = w_in[:, o_cq:o_ckv]
    w_ckv = w_in[:, o_ckv:o_kr]
    w_kr = _mla_slot_columns(jnp.zeros((d_model, B_NOPE), BF16), w_in[:, o_kr:o_gate])
    w_gate = w_in[:, o_gate:]
    w_q = w_q_up.astype(BF16).reshape(B_Q_RANK, B_HEADS, B_QK)
    w_qup = _mla_slot_columns(w_q[..., :B_NOPE], w_q[..., B_NOPE:]).reshape(B_Q_RANK, B_HEADS * HEAD_SLOT)
    w_kv = w_kv_up.astype(BF16).reshape(B_KV_RANK, B_HEADS, B_NOPE + B_V)
    w_kup = _mla_slot_columns(w_kv[..., :B_NOPE], jnp.zeros((B_KV_RANK, B_HEADS, B_ROPE), BF16)).reshape(
        B_KV_RANK, B_HEADS * HEAD_SLOT)
    w_vup = w_kv[:, :, B_NOPE:].reshape(B_KV_RANK, B_HEADS * B_V)

    cos_a, sin_a = _rope_tables(seq, A_HEAD_DIM)
    cos_a = np.tile(cos_a, (1, LANES // (A_HEAD_DIM // 2)))
    sin_a = np.concatenate([-sin_a, -sin_a, sin_a, sin_a], axis=1)
    cos_b, sin_b = _rope_tables(seq, B_ROPE)
    cos_b = _mla_slot_columns(np.ones((seq, B_NOPE), np.float32), np.concatenate([cos_b, cos_b], axis=1), np)
    sin_b = _mla_slot_columns(np.zeros((seq, B_NOPE), np.float32), np.concatenate([-sin_b, sin_b], axis=1), np)

    n_tiles = rows // ROW_TILE
    seq_tiles = seq // ROW_TILE
    row_spec = lambda w: pl.BlockSpec((ROW_TILE, w), lambda r: (r, 0))
    table_spec = pl.BlockSpec((ROW_TILE, LANES), lambda r: (r % seq_tiles, 0))
    t_spec = lambda h, width: pl.BlockSpec((ROW_TILE // width, h, width), lambda r: (r, 0, 0))
    t_shape = lambda h, width: jax.ShapeDtypeStruct((rows // width, h, width), BF16)
    vec = lambda a: a.reshape(1, -1).astype(F32)
    operands = [
        (x2, row_spec(d_model)), (vec(g_mix), _resident((1, d_model))),
        (cos_a, table_spec), (sin_a, table_spec), (cos_b, table_spec), (sin_b, table_spec),
        (w_qkv, _resident(w_qkv.shape)), (w_cq, _resident(w_cq.shape)), (w_ckv, _resident(w_ckv.shape)),
        (w_kr, _resident(w_kr.shape)), (w_gate, _resident(w_gate.shape)),
        (vec(b_gate), _resident((1, 2 * d_model))),
        (vec(g_cq), _resident((1, B_Q_RANK))), (w_qup, _resident(w_qup.shape)),
        (vec(g_ckv), _resident((1, B_KV_RANK))), (w_kup, _resident(w_kup.shape)),
        (w_vup, _resident(w_vup.shape)),
    ]
    out_shape = [
        t_shape(A_HEADS * HEAD_SLOT, K_TILE),
        jax.ShapeDtypeStruct((rows, a_width), BF16),
        t_shape(a_width, K_TILE),
        jax.ShapeDtypeStruct((n_tiles, ROW_TILE // MOBA_BLOCK, a_width), F32),
        jax.ShapeDtypeStruct((rows, 2 * d_model), BF16),
        t_shape(B_HEADS * HEAD_SLOT, K_TILE),
        jax.ShapeDtypeStruct((rows, B_HEADS * HEAD_SLOT), BF16),
        t_shape(B_HEADS * B_V, K_TILE),
    ]
    out_specs = [
        t_spec(A_HEADS * HEAD_SLOT, K_TILE), row_spec(a_width), t_spec(a_width, K_TILE),
        pl.BlockSpec((1, ROW_TILE // MOBA_BLOCK, a_width), lambda r: (r, 0, 0)),
        row_spec(2 * d_model), t_spec(B_HEADS * HEAD_SLOT, K_TILE), row_spec(B_HEADS * HEAD_SLOT),
        t_spec(B_HEADS * B_V, K_TILE),
    ]
    return pl.pallas_call(
        _pre_kernel,
        grid=(n_tiles,),
        in_specs=[s for _, s in operands],
        out_specs=out_specs,
        out_shape=out_shape,
        compiler_params=_params("parallel"),
        name="pre_proj",
    )(*[a for a, _ in operands])


def _attn_call(q_t, k, v_t, kmean, *, batch, seq, moba):
    nblk = seq // K_TILE
    n_q = seq // Q_TILE
    heads = q_t.shape[1] // HEAD_SLOT
    groups = heads // HEADS_PER_STEP
    vdim = v_t.shape[2] // heads
    key_lanes = k.shape[2] // groups
    kv_mode = None if (moba and OWN_BLOCKS <= 2) else pl.Buffered(1)
    in_specs = [
        pl.BlockSpec((OWN_BLOCKS, HEADS_PER_STEP * HEAD_SLOT, K_TILE), lambda b, g, i: (b * n_q + i, g, 0)),
        pl.BlockSpec((None, seq, key_lanes), lambda b, g, i: (b, 0, g), pipeline_mode=kv_mode),
        pl.BlockSpec((None, nblk, HEADS_PER_STEP * vdim, K_TILE), lambda b, g, i: (b, 0, g, 0),
                     pipeline_mode=kv_mode),
    ]
    operands = [q_t, k, v_t]
    stat = pltpu.VMEM((HEADS_PER_STEP, 1, Q_TILE), F32)
    scratch = [
        pltpu.VMEM((HEADS_PER_STEP, K_TILE, Q_TILE), F32),
        stat,
        pltpu.VMEM((HEADS_PER_STEP, K_TILE, Q_TILE), BF16),
        stat,
        stat,
        pltpu.VMEM((HEADS_PER_STEP, vdim + SUM_ROWS, Q_TILE), F32),
    ]
    if moba:
        in_specs.append(pl.BlockSpec((None, nblk, key_lanes), lambda b, g, i: (b, 0, g)))
        operands.append(kmean)
        scratch.append(pltpu.VMEM((HEADS_PER_STEP, nblk, Q_TILE), F32))
        scratch.append(stat)
    return pl.pallas_call(
        functools.partial(_attn_kernel, moba=moba),
        grid=(batch, groups, n_q),
        in_specs=in_specs,
        out_specs=pl.BlockSpec((OWN_BLOCKS, HEADS_PER_STEP * vdim, K_TILE), lambda b, g, i: (b * n_q + i, g, 0)),
        out_shape=jax.ShapeDtypeStruct((batch * nblk, heads * vdim, K_TILE), BF16),
        scratch_shapes=scratch,
        compiler_params=_params("parallel", "parallel", "arbitrary"),
        name="moba_attn" if moba else "mla_attn",
    )(*operands)


def _post_call(x2, oa_t, ob_t, gates, w_o_a, w_o_b, w_out, g_mlp, w_ff1, w_ff2, g_final, final_norm):
    rows, d_model = x2.shape
    row_spec = lambda w: pl.BlockSpec((ROW_TILE, w), lambda r: (r, 0))
    t_spec = lambda a: pl.BlockSpec((ROW_TILE // K_TILE,) + a.shape[1:], lambda r: (r, 0, 0))
    vec = lambda a: a.reshape(1, -1).astype(F32)
    weights = [w.astype(BF16) for w in (w_o_a, w_o_b, w_out)]
    operands = [(x2, row_spec(d_model)), (oa_t, t_spec(oa_t)), (ob_t, t_spec(ob_t)),
                (gates, row_spec(2 * d_model))]
    operands += [(w, _resident(w.shape)) for w in weights]
    operands += [(vec(g_mlp), _resident((1, d_model))),
                 (w_ff1.astype(BF16), _resident(w_ff1.shape)), (w_ff2.astype(BF16), _resident(w_ff2.shape)),
                 (vec(g_final), _resident((1, d_model)))]
    return pl.pallas_call(
        functools.partial(_post_kernel, ff_chunk=1024, final_norm=final_norm),
        grid=(rows // ROW_TILE,),
        in_specs=[s for _, s in operands],
        out_specs=row_spec(d_model),
        out_shape=jax.ShapeDtypeStruct((rows, d_model), F32),
        compiler_params=_params("parallel"),
        name="post_mlp",
    )(*[a for a, _ in operands])


def kernel(x, g_mix, w_in, b_gate, g_cq, w_q_up, g_ckv, w_kv_up, w_o_a, w_o_b, w_out, g_mlp, w_ff1,
           w_ff2, g_final):
    batch, seq, d_model = x.shape
    depth = w_in.shape[0]
    assert seq % ROW_TILE == 0 and ROW_TILE % MOBA_BLOCK == 0 and seq % Q_TILE == 0 and OWN_BLOCKS >= 2
    assert A_HEADS % HEADS_PER_STEP == 0 and B_HEADS % HEADS_PER_STEP == 0 and HEADS_PER_STEP % 2 == 0
    nblk = seq // K_TILE
    x2 = x.reshape(batch * seq, d_model)
    for l in range(depth):
        qa_t, ka, va_t, kmean, gates, qb_t, kb, vb_t = _pre_call(
            x2, seq, g_mix[l], w_in[l], b_gate[l], g_cq[l], w_q_up[l], g_ckv[l], w_kv_up[l])
        oa_t = _attn_call(qa_t, ka.reshape(batch, seq, -1), va_t.reshape(batch, nblk, -1, K_TILE),
                          kmean.reshape(batch, nblk, -1), batch=batch, seq=seq, moba=True)
        ob_t = _attn_call(qb_t, kb.reshape(batch, seq, -1), vb_t.reshape(batch, nblk, -1, K_TILE),
                          None, batch=batch, seq=seq, moba=False)
        x2 = _post_call(x2, oa_t, ob_t, gates, w_o_a[l], w_o_b[l], w_out[l], g_mlp[l], w_ff1[l],
                        w_ff2[l], g_final, final_norm=(l == depth - 1))
    return x2.reshape(batch, seq, d_model)
```

```python
import functools

import jax
import jax.numpy as jnp
import numpy as np
from jax import lax
from jax.experimental import pallas as pl
from jax.experimental.pallas import tpu as pltpu

EPS = 1e-6
ROPE_THETA = 10000.0

A_HEADS = 8
A_HEAD_DIM = 64
MOBA_BLOCK = 256
MOBA_TOPK = 3

B_HEADS = 8
B_NOPE = 64
B_ROPE = 32
B_QK = B_NOPE + B_ROPE
B_V = 64
B_Q_RANK = 384
B_KV_RANK = 256

LANES = 128
HEAD_SLOT = 128
ROW_TILE = 512
K_TILE = MOBA_BLOCK
OWN_BLOCKS = 4
Q_TILE = OWN_BLOCKS * MOBA_BLOCK
HEADS_PER_STEP = 8
MASK_VALUE = -1e30
LOG2_E = 1.4426950408889634
SUM_ROWS = 16
VMEM_LIMIT_BYTES = 56 * 1024 * 1024

F32 = jnp.float32
BF16 = jnp.bfloat16


def _rms(x, g):
    return x * lax.rsqrt(jnp.mean(x * x, axis=-1, keepdims=True) + EPS) * g


def _rope_lanes(x, cos, sin_signed):
    groups = []
    for g in range(x.shape[1] // LANES):
        xg = x[:, g * LANES:(g + 1) * LANES]
        groups.append(xg * cos + pltpu.roll(xg, LANES // 2, axis=1) * sin_signed)
    return groups


def _store_transposed(dst_ref, row0, t):
    rows, width = t.shape[0], dst_ref.shape[2]
    for tile in range(ROW_TILE // width):
        dst_ref[tile, row0:row0 + rows, :] = t[:, tile * width:(tile + 1) * width].astype(dst_ref.dtype)


def _pre_kernel(x_ref, gmix_ref, cos_a_ref, sin_a_ref, cos_b_ref, sin_b_ref,
                w_qkv_ref, w_cq_ref, w_ckv_ref, w_kr_ref, w_gate_ref, b_gate_ref,
                g_cq_ref, w_qup_ref, g_ckv_ref, w_kup_ref, w_vup_ref,
                qa_t_ref, ka_ref, va_t_ref, kmean_ref, gate_ref, qb_t_ref, kb_ref, vb_t_ref):
    a_width = A_HEADS * A_HEAD_DIM
    h = _rms(x_ref[...], gmix_ref[...]).astype(BF16)

    qkv = jnp.dot(h, w_qkv_ref[...], preferred_element_type=F32)
    cos_a, sin_a = cos_a_ref[...], sin_a_ref[...]
    q_groups = _rope_lanes(qkv[:, :a_width], cos_a, sin_a)
    k_groups = _rope_lanes(qkv[:, a_width:2 * a_width], cos_a, sin_a)
    first_head = (lax.broadcasted_iota(jnp.int32, (LANES, ROW_TILE), 0) % A_HEAD_DIM) < A_HEAD_DIM // 2
    for g, qg in enumerate(q_groups):
        t = (qg * (A_HEAD_DIM ** -0.5 * LOG2_E)).T
        _store_transposed(qa_t_ref, (2 * g) * HEAD_SLOT, jnp.where(first_head, t, 0.0))
        _store_transposed(qa_t_ref, (2 * g + 1) * HEAD_SLOT, jnp.where(first_head, 0.0, t))
    for g, kg in enumerate(k_groups):
        ka_ref[:, g * LANES:(g + 1) * LANES] = kg.astype(BF16)
        for blk in range(ROW_TILE // MOBA_BLOCK):
            kmean_ref[0, blk:blk + 1, g * LANES:(g + 1) * LANES] = jnp.mean(
                kg[blk * MOBA_BLOCK:(blk + 1) * MOBA_BLOCK], axis=0, keepdims=True)
    for g in range(a_width // LANES):
        vg = qkv[:, 2 * a_width + g * LANES:2 * a_width + (g + 1) * LANES]
        _store_transposed(va_t_ref, g * LANES, vg.T)

    gate = jnp.dot(h, w_gate_ref[...], preferred_element_type=F32) + b_gate_ref[...]
    gate_ref[...] = jax.nn.sigmoid(gate).astype(BF16)

    cos_b, sin_b = cos_b_ref[...], sin_b_ref[...]
    cq = jnp.dot(h, w_cq_ref[...], preferred_element_type=F32)
    cqn = _rms(cq, g_cq_ref[...]).astype(BF16)
    qb = jnp.dot(cqn, w_qup_ref[...], preferred_element_type=F32)
    for hd, qg in enumerate(_rope_lanes(qb, cos_b, sin_b)):
        _store_transposed(qb_t_ref, hd * HEAD_SLOT, (qg * (B_QK ** -0.5 * LOG2_E)).T)
    ckv = jnp.dot(h, w_ckv_ref[...], preferred_element_type=F32)
    ckvn = _rms(ckv, g_ckv_ref[...]).astype(BF16)
    k_nope = jnp.dot(ckvn, w_kup_ref[...], preferred_element_type=F32)
    kr = jnp.dot(h, w_kr_ref[...], preferred_element_type=F32)
    kr = _rope_lanes(kr, cos_b, sin_b)[0]
    for hd in range(B_HEADS):
        kb_ref[:, hd * HEAD_SLOT:(hd + 1) * HEAD_SLOT] = (
            k_nope[:, hd * HEAD_SLOT:(hd + 1) * HEAD_SLOT] + kr).astype(BF16)
    vb = jnp.dot(ckvn, w_vup_ref[...], preferred_element_type=F32)
    for g in range(B_HEADS * B_V // LANES):
        _store_transposed(vb_t_ref, g * LANES, vb[:, g * LANES:(g + 1) * LANES].T)


def _attn_kernel(*refs, moba):
    if moba:
        q_t_ref, k_ref, v_t_ref, kmean_ref, o_t_ref = refs[:5]
        s_scr, smax_scr, p_scr, alpha_scr, m_scr, acc_scr, bias_scr, sbias_scr = refs[5:]
    else:
        q_t_ref, k_ref, v_t_ref, o_t_ref = refs[:4]
        s_scr, smax_scr, p_scr, alpha_scr, m_scr, acc_scr = refs[4:]
    own_lo = OWN_BLOCKS * pl.program_id(2)
    heads = HEADS_PER_STEP
    vdim = v_t_ref.shape[1] // heads
    qry_pos = lax.broadcasted_iota(jnp.int32, (1, Q_TILE), 1)

    def key_lanes(hd):
        g = hd // 2 if moba else hd
        return slice(g * LANES, (g + 1) * LANES)

    def q_head(hd, first=0):
        rows = slice(hd * HEAD_SLOT, (hd + 1) * HEAD_SLOT)
        return jnp.concatenate([q_t_ref[u, rows, :] for u in range(first, OWN_BLOCKS)], axis=1)

    if moba:
        nblk = kmean_ref.shape[0]
        blk = lax.broadcasted_iota(jnp.int32, (nblk, Q_TILE), 0).astype(F32)
        own = (own_lo + qry_pos // MOBA_BLOCK).astype(F32)
        past = blk < own
        for hd in range(heads):
            gate = jnp.dot(kmean_ref[:, key_lanes(hd)].astype(BF16), q_head(hd),
                           preferred_element_type=F32)
            gate = jnp.where(past, gate, -jnp.inf)
            bias = jnp.full(gate.shape, MASK_VALUE, F32)
            for _ in range(MOBA_TOPK):
                best = jnp.max(gate, axis=0, keepdims=True)
                first = jnp.min(jnp.where(gate == best, blk, float(nblk)), axis=0, keepdims=True)
                pick = blk == first
                bias = jnp.where(pick, 0.0, bias)
                gate = jnp.where(pick, -jnp.inf, gate)
            bias_scr[hd] = jnp.where(past, bias, MASK_VALUE)


    def qk_stage(j, first=0, own=False, hds=None):
        out = None
        start = pl.multiple_of(j * K_TILE, K_TILE)
        k_blk = k_ref[pl.ds(start, K_TILE), :]
        cols = slice(first * MOBA_BLOCK, Q_TILE)
        for hd in (range(heads) if hds is None else hds):
            s = jnp.dot(k_blk[:, key_lanes(hd)], q_head(hd, first), preferred_element_type=F32)
            if own:
                key_pos = lax.broadcasted_iota(jnp.int32, s.shape, 0)
                s = jnp.where(key_pos <= qry_pos[:, :s.shape[1]], s, MASK_VALUE)
            s_scr[hd, :, cols] = s
            smax = jnp.max(s, axis=0, keepdims=True)
            if moba:
                bias = bias_scr[hd, pl.ds(j, 1), cols]
                if own:
                    bias = jnp.where(qry_pos[:, :s.shape[1]] >= MOBA_BLOCK, bias, 0.0)
                sbias_scr[hd, :, cols] = bias
                smax = smax + bias
            smax_scr[hd, :, cols] = smax
            out = smax
        return out

    ones_rows = jnp.ones((SUM_ROWS, K_TILE), BF16)

    def softmax_stage(first=0, hds=None, anchor=None):
        cols = slice(first * MOBA_BLOCK, Q_TILE)
        for hd in (range(heads) if hds is None else hds):
            m_old = m_scr[hd, :, cols]
            m_new = jnp.maximum(m_old, smax_scr[hd, :, cols])
            shift = m_new - sbias_scr[hd, :, cols] if moba else m_new
            if anchor is not None:
                shift = shift + anchor * 0.0
            p_scr[hd, :, cols] = jnp.exp2(s_scr[hd, :, cols] - shift).astype(BF16)
            alpha_scr[hd, :, cols] = jnp.exp2(m_old - m_new)
            m_scr[hd, :, cols] = m_new

    def pv_stage(j, first=0):
        v_t = v_t_ref[j]
        cols = slice(first * MOBA_BLOCK, Q_TILE)
        for hd in range(heads):
            v_ext = jnp.concatenate([v_t[hd * vdim:(hd + 1) * vdim], ones_rows], axis=0)
            acc_scr[hd, :, cols] = alpha_scr[hd, :, cols] * acc_scr[hd, :, cols] + jnp.dot(
                v_ext, p_scr[hd, :, cols], preferred_element_type=F32)

    def visited_block(v):
        return jnp.where(v < OWN_BLOCKS, own_lo + (OWN_BLOCKS - 1) - v, v - OWN_BLOCKS)

    m_scr[...] = jnp.full(m_scr.shape, MASK_VALUE, F32)
    acc_scr[...] = jnp.zeros(acc_scr.shape, F32)
    p_scr[:, :, :MOBA_BLOCK] = jnp.zeros(p_scr.shape[:2] + (MOBA_BLOCK,), BF16)
    alpha_scr[:, :, :MOBA_BLOCK] = jnp.ones(alpha_scr.shape[:2] + (MOBA_BLOCK,), F32)
    qk_stage(own_lo + OWN_BLOCKS - 1, OWN_BLOCKS - 1, own=True)
    for v in range(OWN_BLOCKS - 1):
        u = OWN_BLOCKS - 1 - v
        if v > 0:
            pv_stage(own_lo + u + 1, u + 1)
        softmax_stage(u)
        qk_stage(own_lo + u - 1, u - 1, own=True)

    def body(t, carry):
        pv_stage(visited_block(t - 1))
        done = []
        for hd in range(heads):
            softmax_stage(hds=[hd], anchor=done[hd - 2] if hd >= 2 else None)
            done.append(qk_stage(t + 1 - OWN_BLOCKS, hds=[hd]))
        return carry

    last = own_lo + OWN_BLOCKS - 1
    lax.fori_loop(OWN_BLOCKS - 1, last, body, 0)
    pv_stage(visited_block(last - 1))
    softmax_stage()
    pv_stage(visited_block(last))

    for hd in range(heads):
        acc = acc_scr[hd]
        o = (acc[:vdim] / acc[vdim:vdim + 1]).astype(o_t_ref.dtype)
        for u in range(OWN_BLOCKS):
            o_t_ref[u, hd * vdim:(hd + 1) * vdim, :] = o[:, u * MOBA_BLOCK:(u + 1) * MOBA_BLOCK]


def _post_kernel(x_ref, oa_t_ref, ob_t_ref, gate_ref, w_oa_ref, w_ob_ref, w_out_ref,
                 g_mlp_ref, w_ff1_ref, w_ff2_ref, g_final_ref, out_ref, *, ff_chunk, final_norm):
    d_model = x_ref.shape[1]
    contract_rows = (((0,), (0,)), ((), ()))
    ya, yb = [], []
    for qt in range(ROW_TILE // K_TILE):
        ya.append(lax.dot_general(oa_t_ref[qt], w_oa_ref[...], contract_rows, preferred_element_type=F32))
        yb.append(lax.dot_general(ob_t_ref[qt], w_ob_ref[...], contract_rows, preferred_element_type=F32))
    ya = ya[0] if len(ya) == 1 else jnp.concatenate(ya, axis=0)
    yb = yb[0] if len(yb) == 1 else jnp.concatenate(yb, axis=0)
    mixed = gate_ref[:, :d_model].astype(F32) * ya + gate_ref[:, d_model:].astype(F32) * yb
    x1 = x_ref[...] + jnp.dot(mixed.astype(BF16), w_out_ref[...], preferred_element_type=F32)

    h2 = _rms(x1, g_mlp_ref[...]).astype(BF16)
    mlp = jnp.zeros_like(x1)
    for c in range(w_ff1_ref.shape[1] // ff_chunk):
        u = jnp.maximum(jnp.dot(h2, w_ff1_ref[:, c * ff_chunk:(c + 1) * ff_chunk],
                                preferred_element_type=F32), 0.0)
        mlp = mlp + jnp.dot((u * u).astype(BF16), w_ff2_ref[c * ff_chunk:(c + 1) * ff_chunk, :],
                            preferred_element_type=F32)
    x2 = x1 + mlp
    out_ref[...] = _rms(x2, g_final_ref[...]) if final_norm else x2


def _resident(shape):
    return pl.BlockSpec(shape, lambda *_: (0,) * len(shape), pipeline_mode=pl.Buffered(1))


def _rope_tables(seq, dim):
    half = dim // 2
    inv_freq = np.power(ROPE_THETA, -np.arange(half, dtype=np.float64) / half)
    ang = np.arange(seq, dtype=np.float64)[:, None] * inv_freq[None, :]
    return np.cos(ang).astype(np.float32), np.sin(ang).astype(np.float32)


def _moba_group_columns():
    half = A_HEAD_DIM // 2
    order = []
    for pair in range(A_HEADS // 2):
        h0, h1 = 2 * pair * A_HEAD_DIM, (2 * pair + 1) * A_HEAD_DIM
        for start in (h0, h1, h0 + half, h1 + half):
            order.extend(range(start, start + half))
    return jnp.asarray(order, jnp.int32)


def _mla_slot_columns(nope, rope, xp=jnp):
    r = B_ROPE // 2
    split = LANES // 2 - r
    pad = xp.zeros(nope.shape[:-1] + (HEAD_SLOT - B_QK,), nope.dtype)
    return xp.concatenate([rope[..., :r], nope[..., :split], rope[..., r:], nope[..., split:], pad], axis=-1)


def _params(*semantics):
    return pltpu.CompilerParams(dimension_semantics=semantics, vmem_limit_bytes=VMEM_LIMIT_BYTES)


def _pre_call(x2, seq, g_mix, w_in, b_gate, g_cq, w_q_up, g_ckv, w_kv_up):
    rows, d_model = x2.shape
    a_width = A_HEADS * A_HEAD_DIM
    o_q, o_k, o_v = 0, a_width, 2 * a_width
    o_cq = 3 * a_width
    o_ckv = o_cq + B_Q_RANK
    o_kr = o_ckv + B_KV_RANK
    o_gate = o_kr + B_ROPE
    w_in = w_in.astype(BF16)
    cols = _moba_group_columns()
    w_qkv = jnp.concatenate([w_in[:, o_q:o_k][:, cols], w_in[:, o_k:o_v][:, cols], w_in[:, o_v:o_cq]], axis=1)
    w_cq = w_in[:, o_cq:o_ckv]
    w_ckv = w_in[:, o_ckv:o_kr]
    w_kr = _mla_slot_columns(jnp.zeros((d_model, B_NOPE), BF16), w_in[:, o_kr:o_gate])
    w_gate = w_in[:, o_gate:]
    w_q = w_q_up.astype(BF16).reshape(B_Q_RANK, B_HEADS, B_QK)
    w_qup = _mla_slot_columns(w_q[..., :B_NOPE], w_q[..., B_NOPE:]).reshape(B_Q_RANK, B_HEADS * HEAD_SLOT)
    w_kv = w_kv_up.astype(BF16).reshape(B_KV_RANK, B_HEADS, B_NOPE + B_V)
    w_kup = _mla_slot_columns(w_kv[..., :B_NOPE], jnp.zeros((B_KV_RANK, B_HEADS, B_ROPE), BF16)).reshape(
        B_KV_RANK, B_HEADS * HEAD_SLOT)
    w_vup = w_kv[:, :, B_NOPE:].reshape(B_KV_RANK, B_HEADS * B_V)

    cos_a, sin_a = _rope_tables(seq, A_HEAD_DIM)
    cos_a = np.tile(cos_a, (1, LANES // (A_HEAD_DIM // 2)))
    sin_a = np.concatenate([-sin_a, -sin_a, sin_a, sin_a], axis=1)
    cos_b, sin_b = _rope_tables(seq, B_ROPE)
    cos_b = _mla_slot_columns(np.ones((seq, B_NOPE), np.float32), np.concatenate([cos_b, cos_b], axis=1), np)
    sin_b = _mla_slot_columns(np.zeros((seq, B_NOPE), np.float32), np.concatenate([-sin_b, sin_b], axis=1), np)

    n_tiles = rows // ROW_TILE
    seq_tiles = seq // ROW_TILE
    row_spec = lambda w: pl.BlockSpec((ROW_TILE, w), lambda r: (r, 0))
    table_spec = pl.BlockSpec((ROW_TILE, LANES), lambda r: (r % seq_tiles, 0))
    t_spec = lambda h, width: pl.BlockSpec((ROW_TILE // width, h, width), lambda r: (r, 0, 0))
    t_shape = lambda h, width: jax.ShapeDtypeStruct((rows // width, h, width), BF16)
    vec = lambda a: a.reshape(1, -1).astype(F32)
    operands = [
        (x2, row_spec(d_model)), (vec(g_mix), _resident((1, d_model))),
        (cos_a, table_spec), (sin_a, table_spec), (cos_b, table_spec), (sin_b, table_spec),
        (w_qkv, _resident(w_qkv.shape)), (w_cq, _resident(w_cq.shape)), (w_ckv, _resident(w_ckv.shape)),
        (w_kr, _resident(w_kr.shape)), (w_gate, _resident(w_gate.shape)),
        (vec(b_gate), _resident((1, 2 * d_model))),
        (vec(g_cq), _resident((1, B_Q_RANK))), (w_qup, _resident(w_qup.shape)),
        (vec(g_ckv), _resident((1, B_KV_RANK))), (w_kup, _resident(w_kup.shape)),
        (w_vup, _resident(w_vup.shape)),
    ]
    out_shape = [
        t_shape(A_HEADS * HEAD_SLOT, K_TILE),
        jax.ShapeDtypeStruct((rows, a_width), BF16),
        t_shape(a_width, K_TILE),
        jax.ShapeDtypeStruct((n_tiles, ROW_TILE // MOBA_BLOCK, a_width), F32),
        jax.ShapeDtypeStruct((rows, 2 * d_model), BF16),
        t_shape(B_HEADS * HEAD_SLOT, K_TILE),
        jax.ShapeDtypeStruct((rows, B_HEADS * HEAD_SLOT), BF16),
        t_shape(B_HEADS * B_V, K_TILE),
    ]
    out_specs = [
        t_spec(A_HEADS * HEAD_SLOT, K_TILE), row_spec(a_width), t_spec(a_width, K_TILE),
        pl.BlockSpec((1, ROW_TILE // MOBA_BLOCK, a_width), lambda r: (r, 0, 0)),
        row_spec(2 * d_model), t_spec(B_HEADS * HEAD_SLOT, K_TILE), row_spec(B_HEADS * HEAD_SLOT),
        t_spec(B_HEADS * B_V, K_TILE),
    ]
    return pl.pallas_call(
        _pre_kernel,
        grid=(n_tiles,),
        in_specs=[s for _, s in operands],
        out_specs=out_specs,
        out_shape=out_shape,
        compiler_params=_params("parallel"),
        name="pre_proj",
    )(*[a for a, _ in operands])


def _attn_call(q_t, k, v_t, kmean, *, batch, seq, moba):
    nblk = seq // K_TILE
    n_q = seq // Q_TILE
    heads = q_t.shape[1] // HEAD_SLOT
    groups = heads // HEADS_PER_STEP
    vdim = v_t.shape[2] // heads
    key_lanes = k.shape[2] // groups
    kv_mode = None if (moba and OWN_BLOCKS <= 2) else pl.Buffered(1)
    in_specs = [
        pl.BlockSpec((OWN_BLOCKS, HEADS_PER_STEP * HEAD_SLOT, K_TILE), lambda b, g, i: (b * n_q + i, g, 0)),
        pl.BlockSpec((None, seq, key_lanes), lambda b, g, i: (b, 0, g), pipeline_mode=kv_mode),
        pl.BlockSpec((None, nblk, HEADS_PER_STEP * vdim, K_TILE), lambda b, g, i: (b, 0, g, 0),
                     pipeline_mode=kv_mode),
    ]
    operands = [q_t, k, v_t]
    stat = pltpu.VMEM((HEADS_PER_STEP, 1, Q_TILE), F32)
    scratch = [
        pltpu.VMEM((HEADS_PER_STEP, K_TILE, Q_TILE), F32),
        stat,
        pltpu.VMEM((HEADS_PER_STEP, K_TILE, Q_TILE), BF16),
        stat,
        stat,
        pltpu.VMEM((HEADS_PER_STEP, vdim + SUM_ROWS, Q_TILE), F32),
    ]
    if moba:
        in_specs.append(pl.BlockSpec((None, nblk, key_lanes), lambda b, g, i: (b, 0, g)))
        operands.append(kmean)
        scratch.append(pltpu.VMEM((HEADS_PER_STEP, nblk, Q_TILE), F32))
        scratch.append(stat)
    return pl.pallas_call(
        functools.partial(_attn_kernel, moba=moba),
        grid=(batch, groups, n_q),
        in_specs=in_specs,
        out_specs=pl.BlockSpec((OWN_BLOCKS, HEADS_PER_STEP * vdim, K_TILE), lambda b, g, i: (b * n_q + i, g, 0)),
        out_shape=jax.ShapeDtypeStruct((batch * nblk, heads * vdim, K_TILE), BF16),
        scratch_shapes=scratch,
        compiler_params=_params("parallel", "parallel", "arbitrary"),
        name="moba_attn" if moba else "mla_attn",
    )(*operands)


def _post_call(x2, oa_t, ob_t, gates, w_o_a, w_o_b, w_out, g_mlp, w_ff1, w_ff2, g_final, final_norm):
    rows, d_model = x2.shape
    row_spec = lambda w: pl.BlockSpec((ROW_TILE, w), lambda r: (r, 0))
    t_spec = lambda a: pl.BlockSpec((ROW_TILE // K_TILE,) + a.shape[1:], lambda r: (r, 0, 0))
    vec = lambda a: a.reshape(1, -1).astype(F32)
    weights = [w.astype(BF16) for w in (w_o_a, w_o_b, w_out)]
    operands = [(x2, row_spec(d_model)), (oa_t, t_spec(oa_t)), (ob_t, t_spec(ob_t)),
                (gates, row_spec(2 * d_model))]
    operands += [(w, _resident(w.shape)) for w in weights]
    operands += [(vec(g_mlp), _resident((1, d_model))),
                 (w_ff1.astype(BF16), _resident(w_ff1.shape)), (w_ff2.astype(BF16), _resident(w_ff2.shape)),
                 (vec(g_final), _resident((1, d_model)))]
    return pl.pallas_call(
        functools.partial(_post_kernel, ff_chunk=1024, final_norm=final_norm),
        grid=(rows // ROW_TILE,),
        in_specs=[s for _, s in operands],
        out_specs=row_spec(d_model),
        out_shape=jax.ShapeDtypeStruct((rows, d_model), F32),
        compiler_params=_params("parallel"),
        name="post_mlp",
    )(*[a for a, _ in operands])


def kernel(x, g_mix, w_in, b_gate, g_cq, w_q_up, g_ckv, w_kv_up, w_o_a, w_o_b, w_out, g_mlp, w_ff1,
           w_ff2, g_final):
    batch, seq, d_model = x.shape
    depth = w_in.shape[0]
    assert seq % ROW_TILE == 0 and ROW_TILE % MOBA_BLOCK == 0 and seq % Q_TILE == 0 and OWN_BLOCKS >= 2
    assert A_HEADS % HEADS_PER_STEP == 0 and B_HEADS % HEADS_PER_STEP == 0 and HEADS_PER_STEP % 2 == 0
    nblk = seq // K_TILE
    x2 = x.reshape(batch * seq, d_model)
    for l in range(depth):
        qa_t, ka, va_t, kmean, gates, qb_t, kb, vb_t = _pre_call(
            x2, seq, g_mix[l], w_in[l], b_gate[l], g_cq[l], w_q_up[l], g_ckv[l], w_kv_up[l])
        oa_t = _attn_call(qa_t, ka.reshape(batch, seq, -1), va_t.reshape(batch, nblk, -1, K_TILE),
                          kmean.reshape(batch, nblk, -1), batch=batch, seq=seq, moba=True)
        ob_t = _attn_call(qb_t, kb.reshape(batch, seq, -1), vb_t.reshape(batch, nblk, -1, K_TILE),
                          None, batch=batch, seq=seq, moba=False)
        x2 = _post_call(x2, oa_t, ob_t, gates, w_o_a[l], w_o_b[l], w_out[l], g_mlp[l], w_ff1[l],
                        w_ff2[l], g_final, final_norm=(l == depth - 1))
    return x2.reshape(batch, seq, d_model)
```

```python
import functools

import jax
import jax.numpy as jnp
import numpy as np
from jax import lax
from jax.experimental import pallas as pl
from jax.experimental.pallas import tpu as pltpu

EPS = 1e-6
ROPE_THETA = 10000.0

A_HEADS = 8
A_HEAD_DIM = 64
MOBA_BLOCK = 256
MOBA_TOPK = 3

B_HEADS = 8
B_NOPE = 64
B_ROPE = 32
B_QK = B_NOPE + B_ROPE
B_V = 64
B_Q_RANK = 384
B_KV_RANK = 256

LANES = 128
HEAD_SLOT = 128
ROW_TILE = 512
FF_CHUNK = 1024
K_TILE = MOBA_BLOCK
OWN_BLOCKS = 4
Q_TILE = OWN_BLOCKS * MOBA_BLOCK
HEADS_PER_STEP = 8
MASK_VALUE = -1e30
LOG2_E = 1.4426950408889634
SUM_ROWS = 16
VMEM_LIMIT_BYTES = 56 * 1024 * 1024

F32 = jnp.float32
BF16 = jnp.bfloat16


def _rms(x, g):
    return x * lax.rsqrt(jnp.mean(x * x, axis=-1, keepdims=True) + EPS) * g


def _rope_lanes(x, cos, sin_signed):
    groups = []
    for g in range(x.shape[1] // LANES):
        xg = x[:, g * LANES:(g + 1) * LANES]
        groups.append(xg * cos + pltpu.roll(xg, LANES // 2, axis=1) * sin_signed)
    return groups


def _store_transposed(dst_ref, row0, t):
    rows, width = t.shape[0], dst_ref.shape[2]
    for tile in range(ROW_TILE // width):
        dst_ref[tile, row0:row0 + rows, :] = t[:, tile * width:(tile + 1) * width].astype(dst_ref.dtype)


def _pre_kernel(x_ref, gmix_ref, cos_a_ref, sin_a_ref, cos_b_ref, sin_b_ref,
                w_qkv_ref, w_cq_ref, w_ckv_ref, w_kr_ref, w_gate_ref, b_gate_ref,
                g_cq_ref, w_qup_ref, g_ckv_ref, w_kup_ref, w_vup_ref,
                qa_t_ref, ka_ref, va_t_ref, kmean_ref, gate_ref, qb_t_ref, kb_ref, vb_t_ref):
    a_width = A_HEADS * A_HEAD_DIM
    h = _rms(x_ref[...], gmix_ref[...]).astype(BF16)

    qkv = jnp.dot(h, w_qkv_ref[...], preferred_element_type=F32)
    cos_a, sin_a = cos_a_ref[...], sin_a_ref[...]
    q_groups = _rope_lanes(qkv[:, :a_width], cos_a, sin_a)
    k_groups = _rope_lanes(qkv[:, a_width:2 * a_width], cos_a, sin_a)
    first_head = (lax.broadcasted_iota(jnp.int32, (LANES, ROW_TILE), 0) % A_HEAD_DIM) < A_HEAD_DIM // 2
    for g, qg in enumerate(q_groups):
        t = (qg * (A_HEAD_DIM ** -0.5 * LOG2_E)).T
        _store_transposed(qa_t_ref, (2 * g) * HEAD_SLOT, jnp.where(first_head, t, 0.0))
        _store_transposed(qa_t_ref, (2 * g + 1) * HEAD_SLOT, jnp.where(first_head, 0.0, t))
    for g, kg in enumerate(k_groups):
        ka_ref[:, g * LANES:(g + 1) * LANES] = kg.astype(BF16)
        for blk in range(ROW_TILE // MOBA_BLOCK):
            kmean_ref[0, blk:blk + 1, g * LANES:(g + 1) * LANES] = jnp.mean(
                kg[blk * MOBA_BLOCK:(blk + 1) * MOBA_BLOCK], axis=0, keepdims=True)
    for g in range(a_width // LANES):
        vg = qkv[:, 2 * a_width + g * LANES:2 * a_width + (g + 1) * LANES]
        _store_transposed(va_t_ref, g * LANES, vg.T)

    gate = jnp.dot(h, w_gate_ref[...], preferred_element_type=F32) + b_gate_ref[...]
    gate_ref[...] = jax.nn.sigmoid(gate).astype(BF16)

    cos_b, sin_b = cos_b_ref[...], sin_b_ref[...]
    cq = jnp.dot(h, w_cq_ref[...], preferred_element_type=F32)
    cqn = _rms(cq, g_cq_ref[...]).astype(BF16)
    qb = jnp.dot(cqn, w_qup_ref[...], preferred_element_type=F32)
    for hd, qg in enumerate(_rope_lanes(qb, cos_b, sin_b)):
        _store_transposed(qb_t_ref, hd * HEAD_SLOT, (qg * (B_QK ** -0.5 * LOG2_E)).T)
    ckv = jnp.dot(h, w_ckv_ref[...], preferred_element_type=F32)
    ckvn = _rms(ckv, g_ckv_ref[...]).astype(BF16)
    k_nope = jnp.dot(ckvn, w_kup_ref[...], preferred_element_type=F32)
    kr = jnp.dot(h, w_kr_ref[...], preferred_element_type=F32)
    kr = _rope_lanes(kr, cos_b, sin_b)[0]
    for hd in range(B_HEADS):
        kb_ref[:, hd * HEAD_SLOT:(hd + 1) * HEAD_SLOT] = (
            k_nope[:, hd * HEAD_SLOT:(hd + 1) * HEAD_SLOT] + kr).astype(BF16)
    vb = jnp.dot(ckvn, w_vup_ref[...], preferred_element_type=F32)
    for g in range(B_HEADS * B_V // LANES):
        _store_transposed(vb_t_ref, g * LANES, vb[:, g * LANES:(g + 1) * LANES].T)


def _attn_kernel(*refs, moba):
    if moba:
        q_t_ref, k_ref, v_t_ref, kmean_ref, o_t_ref = refs[:5]
        s_scr, smax_scr, p_scr, alpha_scr, m_scr, acc_scr, bias_scr, sbias_scr = refs[5:]
    else:
        q_t_ref, k_ref, v_t_ref, o_t_ref = refs[:4]
        s_scr, smax_scr, p_scr, alpha_scr, m_scr, acc_scr = refs[4:]
    own_lo = OWN_BLOCKS * pl.program_id(2)
    heads = HEADS_PER_STEP
    vdim = v_t_ref.shape[1] // heads
    qry_pos = lax.broadcasted_iota(jnp.int32, (1, Q_TILE), 1)

    def key_lanes(hd):
        g = hd // 2 if moba else hd
        return slice(g * LANES, (g + 1) * LANES)

    def q_head(hd, first=0):
        rows = slice(hd * HEAD_SLOT, (hd + 1) * HEAD_SLOT)
        return jnp.concatenate([q_t_ref[u, rows, :] for u in range(first, OWN_BLOCKS)], axis=1)

    if moba:
        nblk = kmean_ref.shape[0]
        blk = lax.broadcasted_iota(jnp.int32, (nblk, Q_TILE), 0).astype(F32)
        own = (own_lo + qry_pos // MOBA_BLOCK).astype(F32)
        past = blk < own
        for hd in range(heads):
            gate = jnp.dot(kmean_ref[:, key_lanes(hd)].astype(BF16), q_head(hd),
                           preferred_element_type=F32)
            gate = jnp.where(past, gate, -jnp.inf)
            bias = jnp.full(gate.shape, MASK_VALUE, F32)
            for _ in range(MOBA_TOPK):
                best = jnp.max(gate, axis=0, keepdims=True)
                first = jnp.min(jnp.where(gate == best, blk, float(nblk)), axis=0, keepdims=True)
                pick = blk == first
                bias = jnp.where(pick, 0.0, bias)
                gate = jnp.where(pick, -jnp.inf, gate)
            bias_scr[hd] = jnp.where(past, bias, MASK_VALUE)


    def qk_stage(j, first=0, own=False):
        start = pl.multiple_of(j * K_TILE, K_TILE)
        k_blk = k_ref[pl.ds(start, K_TILE), :]
        cols = slice(first * MOBA_BLOCK, Q_TILE)
        for hd in range(heads):
            s = jnp.dot(k_blk[:, key_lanes(hd)], q_head(hd, first), preferred_element_type=F32)
            if own:
                key_pos = lax.broadcasted_iota(jnp.int32, s.shape, 0)
                s = jnp.where(key_pos <= qry_pos[:, :s.shape[1]], s, MASK_VALUE)
            s_scr[hd, :, cols] = s
            smax = jnp.max(s, axis=0, keepdims=True)
            if moba:
                bias = bias_scr[hd, pl.ds(j, 1), cols]
                if own:
                    bias = jnp.where(qry_pos[:, :s.shape[1]] >= MOBA_BLOCK, bias, 0.0)
                sbias_scr[hd, :, cols] = bias
                smax = smax + bias
            smax_scr[hd, :, cols] = smax

    ones_rows = jnp.ones((SUM_ROWS, K_TILE), BF16)

    def softmax_stage(first=0):
        cols = slice(first * MOBA_BLOCK, Q_TILE)
        for hd in range(heads):
            m_old = m_scr[hd, :, cols]
            m_new = jnp.maximum(m_old, smax_scr[hd, :, cols])
            shift = m_new - sbias_scr[hd, :, cols] if moba else m_new
            p_scr[hd, :, cols] = jnp.exp2(s_scr[hd, :, cols] - shift).astype(BF16)
            alpha_scr[hd, :, cols] = jnp.exp2(m_old - m_new)
            m_scr[hd, :, cols] = m_new

    def pv_stage(j, first=0):
        v_t = v_t_ref[j]
        cols = slice(first * MOBA_BLOCK, Q_TILE)
        for hd in range(heads):
            v_ext = jnp.concatenate([v_t[hd * vdim:(hd + 1) * vdim], ones_rows], axis=0)
            acc_scr[hd, :, cols] = alpha_scr[hd, :, cols] * acc_scr[hd, :, cols] + jnp.dot(
                v_ext, p_scr[hd, :, cols], preferred_element_type=F32)

    def visited_block(v):
        return jnp.where(v < OWN_BLOCKS, own_lo + (OWN_BLOCKS - 1) - v, v - OWN_BLOCKS)

    m_scr[...] = jnp.full(m_scr.shape, MASK_VALUE, F32)
    acc_scr[...] = jnp.zeros(acc_scr.shape, F32)
    p_scr[:, :, :MOBA_BLOCK] = jnp.zeros(p_scr.shape[:2] + (MOBA_BLOCK,), BF16)
    alpha_scr[:, :, :MOBA_BLOCK] = jnp.ones(alpha_scr.shape[:2] + (MOBA_BLOCK,), F32)
    qk_stage(own_lo + OWN_BLOCKS - 1, OWN_BLOCKS - 1, own=True)
    for v in range(OWN_BLOCKS - 1):
        u = OWN_BLOCKS - 1 - v
        if v > 0:
            pv_stage(own_lo + u + 1, u + 1)
        softmax_stage(u)
        qk_stage(own_lo + u - 1, u - 1, own=True)

    def body(t, carry):
        pv_stage(visited_block(t - 1))
        softmax_stage()
        qk_stage(t + 1 - OWN_BLOCKS)
        return carry

    last = own_lo + OWN_BLOCKS - 1
    lax.fori_loop(OWN_BLOCKS - 1, last, body, 0)
    pv_stage(visited_block(last - 1))
    softmax_stage()
    pv_stage(visited_block(last))

    for hd in range(heads):
        acc = acc_scr[hd]
        o = (acc[:vdim] / acc[vdim:vdim + 1]).astype(o_t_ref.dtype)
        for u in range(OWN_BLOCKS):
            o_t_ref[u, hd * vdim:(hd + 1) * vdim, :] = o[:, u * MOBA_BLOCK:(u + 1) * MOBA_BLOCK]


def _post_kernel(x_ref, oa_t_ref, ob_t_ref, gate_ref, w_oa_ref, w_ob_ref, w_out_ref,
                 g_mlp_ref, w_ff1_ref, w_ff2_ref, g_final_ref, out_ref, *, ff_chunk, final_norm):
    d_model = x_ref.shape[1]
    contract_rows = (((0,), (0,)), ((), ()))
    ya, yb = [], []
    for qt in range(ROW_TILE // K_TILE):
        ya.append(lax.dot_general(oa_t_ref[qt], w_oa_ref[...], contract_rows, preferred_element_type=F32))
        yb.append(lax.dot_general(ob_t_ref[qt], w_ob_ref[...], contract_rows, preferred_element_type=F32))
    ya = ya[0] if len(ya) == 1 else jnp.concatenate(ya, axis=0)
    yb = yb[0] if len(yb) == 1 else jnp.concatenate(yb, axis=0)
    mixed = gate_ref[:, :d_model].astype(F32) * ya + gate_ref[:, d_model:].astype(F32) * yb
    x1 = x_ref[...] + jnp.dot(mixed.astype(BF16), w_out_ref[...], preferred_element_type=F32)

    h2 = _rms(x1, g_mlp_ref[...]).astype(BF16)
    mlp = jnp.zeros_like(x1)
    for c in range(w_ff1_ref.shape[1] // ff_chunk):
        u = jnp.maximum(jnp.dot(h2, w_ff1_ref[:, c * ff_chunk:(c + 1) * ff_chunk],
                                preferred_element_type=F32), 0.0)
        mlp = mlp + jnp.dot((u * u).astype(BF16), w_ff2_ref[c * ff_chunk:(c + 1) * ff_chunk, :],
                            preferred_element_type=F32)
    x2 = x1 + mlp
    out_ref[...] = _rms(x2, g_final_ref[...]) if final_norm else x2


def _resident(shape):
    return pl.BlockSpec(shape, lambda *_: (0,) * len(shape), pipeline_mode=pl.Buffered(1))


def _rope_tables(seq, dim):
    half = dim // 2
    inv_freq = np.power(ROPE_THETA, -np.arange(half, dtype=np.float64) / half)
    ang = np.arange(seq, dtype=np.float64)[:, None] * inv_freq[None, :]
    return np.cos(ang).astype(np.float32), np.sin(ang).astype(np.float32)


def _moba_group_columns():
    half = A_HEAD_DIM // 2
    order = []
    for pair in range(A_HEADS // 2):
        h0, h1 = 2 * pair * A_HEAD_DIM, (2 * pair + 1) * A_HEAD_DIM
        for start in (h0, h1, h0 + half, h1 + half):
            order.extend(range(start, start + half))
    return jnp.asarray(order, jnp.int32)


def _mla_slot_columns(nope, rope, xp=jnp):
    r = B_ROPE // 2
    split = LANES // 2 - r
    pad = xp.zeros(nope.shape[:-1] + (HEAD_SLOT - B_QK,), nope.dtype)
    return xp.concatenate([rope[..., :r], nope[..., :split], rope[..., r:], nope[..., split:], pad], axis=-1)


def _params(*semantics):
    return pltpu.CompilerParams(dimension_semantics=semantics, vmem_limit_bytes=VMEM_LIMIT_BYTES)


def _pre_call(x2, seq, g_mix, w_in, b_gate, g_cq, w_q_up, g_ckv, w_kv_up):
    rows, d_model = x2.shape
    a_width = A_HEADS * A_HEAD_DIM
    o_q, o_k, o_v = 0, a_width, 2 * a_width
    o_cq = 3 * a_width
    o_ckv = o_cq + B_Q_RANK
    o_kr = o_ckv + B_KV_RANK
    o_gate = o_kr + B_ROPE
    piece = lambda lo, hi: w_in[:, lo:hi].astype(BF16)
    cols = _moba_group_columns()
    w_qkv = jnp.concatenate([piece(o_q, o_k)[:, cols], piece(o_k, o_v)[:, cols], piece(o_v, o_cq)], axis=1)
    w_cq = piece(o_cq, o_ckv)
    w_ckv = piece(o_ckv, o_kr)
    w_kr = _mla_slot_columns(jnp.zeros((d_model, B_NOPE), BF16), piece(o_kr, o_gate))
    w_gate = piece(o_gate, w_in.shape[1])
    w_q = w_q_up.astype(BF16).reshape(B_Q_RANK, B_HEADS, B_QK)
    w_qup = _mla_slot_columns(w_q[..., :B_NOPE], w_q[..., B_NOPE:]).reshape(B_Q_RANK, B_HEADS * HEAD_SLOT)
    w_kv = w_kv_up.astype(BF16).reshape(B_KV_RANK, B_HEADS, B_NOPE + B_V)
    w_kup = _mla_slot_columns(w_kv[..., :B_NOPE], jnp.zeros((B_KV_RANK, B_HEADS, B_ROPE), BF16)).reshape(
        B_KV_RANK, B_HEADS * HEAD_SLOT)
    w_vup = w_kv[:, :, B_NOPE:].reshape(B_KV_RANK, B_HEADS * B_V)

    cos_a, sin_a = _rope_tables(seq, A_HEAD_DIM)
    cos_a = np.tile(cos_a, (1, LANES // (A_HEAD_DIM // 2)))
    sin_a = np.concatenate([-sin_a, -sin_a, sin_a, sin_a], axis=1)
    cos_b, sin_b = _rope_tables(seq, B_ROPE)
    cos_b = _mla_slot_columns(np.ones((seq, B_NOPE), np.float32), np.concatenate([cos_b, cos_b], axis=1), np)
    sin_b = _mla_slot_columns(np.zeros((seq, B_NOPE), np.float32), np.concatenate([-sin_b, sin_b], axis=1), np)

    n_tiles = rows // ROW_TILE
    seq_tiles = seq // ROW_TILE
    row_spec = lambda w: pl.BlockSpec((ROW_TILE, w), lambda r: (r, 0))
    table_spec = pl.BlockSpec((ROW_TILE, LANES), lambda r: (r % seq_tiles, 0))
    t_spec = lambda h, width: pl.BlockSpec((ROW_TILE // width, h, width), lambda r: (r, 0, 0))
    t_shape = lambda h, width: jax.ShapeDtypeStruct((rows // width, h, width), BF16)
    vec = lambda a: a.reshape(1, -1).astype(F32)
    operands = [
        (x2, row_spec(d_model)), (vec(g_mix), _resident((1, d_model))),
        (cos_a, table_spec), (sin_a, table_spec), (cos_b, table_spec), (sin_b, table_spec),
        (w_qkv, _resident(w_qkv.shape)), (w_cq, _resident(w_cq.shape)), (w_ckv, _resident(w_ckv.shape)),
        (w_kr, _resident(w_kr.shape)), (w_gate, _resident(w_gate.shape)),
        (vec(b_gate), _resident((1, 2 * d_model))),
        (vec(g_cq), _resident((1, B_Q_RANK))), (w_qup, _resident(w_qup.shape)),
        (vec(g_ckv), _resident((1, B_KV_RANK))), (w_kup, _resident(w_kup.shape)),
        (w_vup, _resident(w_vup.shape)),
    ]
    out_shape = [
        t_shape(A_HEADS * HEAD_SLOT, K_TILE),
        jax.ShapeDtypeStruct((rows, a_width), BF16),
        t_shape(a_width, K_TILE),
        jax.ShapeDtypeStruct((n_tiles, ROW_TILE // MOBA_BLOCK, a_width), F32),
        jax.ShapeDtypeStruct((rows, 2 * d_model), BF16),
        t_shape(B_HEADS * HEAD_SLOT, K_TILE),
        jax.ShapeDtypeStruct((rows, B_HEADS * HEAD_SLOT), BF16),
        t_shape(B_HEADS * B_V, K_TILE),
    ]
    out_specs = [
        t_spec(A_HEADS * HEAD_SLOT, K_TILE), row_spec(a_width), t_spec(a_width, K_TILE),
        pl.BlockSpec((1, ROW_TILE // MOBA_BLOCK, a_width), lambda r: (r, 0, 0)),
        row_spec(2 * d_model), t_spec(B_HEADS * HEAD_SLOT, K_TILE), row_spec(B_HEADS * HEAD_SLOT),
        t_spec(B_HEADS * B_V, K_TILE),
    ]
    return pl.pallas_call(
        _pre_kernel,
        grid=(n_tiles,),
        in_specs=[s for _, s in operands],
        out_specs=out_specs,
        out_shape=out_shape,
        compiler_params=_params("parallel"),
        name="pre_proj",
    )(*[a for a, _ in operands])


def _attn_call(q_t, k, v_t, kmean, *, batch, seq, moba):
    nblk = seq // K_TILE
    n_q = seq // Q_TILE
    heads = q_t.shape[1] // HEAD_SLOT
    groups = heads // HEADS_PER_STEP
    vdim = v_t.shape[2] // heads
    key_lanes = k.shape[2] // groups
    kv_mode = None if (moba and OWN_BLOCKS <= 2) else pl.Buffered(1)
    in_specs = [
        pl.BlockSpec((OWN_BLOCKS, HEADS_PER_STEP * HEAD_SLOT, K_TILE), lambda b, g, i: (b * n_q + i, g, 0)),
        pl.BlockSpec((None, seq, key_lanes), lambda b, g, i: (b, 0, g), pipeline_mode=kv_mode),
        pl.BlockSpec((None, nblk, HEADS_PER_STEP * vdim, K_TILE), lambda b, g, i: (b, 0, g, 0),
                     pipeline_mode=kv_mode),
    ]
    operands = [q_t, k, v_t]
    stat = pltpu.VMEM((HEADS_PER_STEP, 1, Q_TILE), F32)
    scratch = [
        pltpu.VMEM((HEADS_PER_STEP, K_TILE, Q_TILE), F32),
        stat,
        pltpu.VMEM((HEADS_PER_STEP, K_TILE, Q_TILE), BF16),
        stat,
        stat,
        pltpu.VMEM((HEADS_PER_STEP, vdim + SUM_ROWS, Q_TILE), F32),
    ]
    if moba:
        in_specs.append(pl.BlockSpec((None, nblk, key_lanes), lambda b, g, i: (b, 0, g)))
        operands.append(kmean)
        scratch.append(pltpu.VMEM((HEADS_PER_STEP, nblk, Q_TILE), F32))
        scratch.append(stat)
    return pl.pallas_call(
        functools.partial(_attn_kernel, moba=moba),
        grid=(batch, groups, n_q),
        in_specs=in_specs,
        out_specs=pl.BlockSpec((OWN_BLOCKS, HEADS_PER_STEP * vdim, K_TILE), lambda b, g, i: (b * n_q + i, g, 0)),
        out_shape=jax.ShapeDtypeStruct((batch * nblk, heads * vdim, K_TILE), BF16),
        scratch_shapes=scratch,
        compiler_params=_params("parallel", "parallel", "arbitrary"),
        name="moba_attn" if moba else "mla_attn",
    )(*operands)


def _post_call(x2, oa_t, ob_t, gates, w_o_a, w_o_b, w_out, g_mlp, w_ff1, w_ff2, g_final, final_norm):
    rows, d_model = x2.shape
    row_spec = lambda w: pl.BlockSpec((ROW_TILE, w), lambda r: (r, 0))
    t_spec = lambda a: pl.BlockSpec((ROW_TILE // K_TILE,) + a.shape[1:], lambda r: (r, 0, 0))
    vec = lambda a: a.reshape(1, -1).astype(F32)
    weights = [w.astype(BF16) for w in (w_o_a, w_o_b, w_out)]
    operands = [(x2, row_spec(d_model)), (oa_t, t_spec(oa_t)), (ob_t, t_spec(ob_t)),
                (gates, row_spec(2 * d_model))]
    operands += [(w, _resident(w.shape)) for w in weights]
    operands += [(vec(g_mlp), _resident((1, d_model))),
                 (w_ff1.astype(BF16), _resident(w_ff1.shape)), (w_ff2.astype(BF16), _resident(w_ff2.shape)),
                 (vec(g_final), _resident((1, d_model)))]
    return pl.pallas_call(
        functools.partial(_post_kernel, ff_chunk=FF_CHUNK, final_norm=final_norm),
        grid=(rows // ROW_TILE,),
        in_specs=[s for _, s in operands],
        out_specs=row_spec(d_model),
        out_shape=jax.ShapeDtypeStruct((rows, d_model), F32),
        compiler_params=_params("parallel"),
        name="post_mlp",
    )(*[a for a, _ in operands])


def kernel(x, g_mix, w_in, b_gate, g_cq, w_q_up, g_ckv, w_kv_up, w_o_a, w_o_b, w_out, g_mlp, w_ff1,
           w_ff2, g_final):
    batch, seq, d_model = x.shape
    depth = w_in.shape[0]
    assert seq % ROW_TILE == 0 and ROW_TILE % MOBA_BLOCK == 0 and seq % Q_TILE == 0 and OWN_BLOCKS >= 2
    assert A_HEADS % HEADS_PER_STEP == 0 and B_HEADS % HEADS_PER_STEP == 0 and HEADS_PER_STEP % 2 == 0
    nblk = seq // K_TILE
    x2 = x.reshape(batch * seq, d_model)
    for l in range(depth):
        qa_t, ka, va_t, kmean, gates, qb_t, kb, vb_t = _pre_call(
            x2, seq, g_mix[l], w_in[l], b_gate[l], g_cq[l], w_q_up[l], g_ckv[l], w_kv_up[l])
        oa_t = _attn_call(qa_t, ka.reshape(batch, seq, -1), va_t.reshape(batch, nblk, -1, K_TILE),
                          kmean.reshape(batch, nblk, -1), batch=batch, seq=seq, moba=True)
        ob_t = _attn_call(qb_t, kb.reshape(batch, seq, -1), vb_t.reshape(batch, nblk, -1, K_TILE),
                          None, batch=batch, seq=seq, moba=False)
        x2 = _post_call(x2, oa_t, ob_t, gates, w_o_a[l], w_o_b[l], w_out[l], g_mlp[l], w_ff1[l],
                        w_ff2[l], g_final, final_norm=(l == depth - 1))
    return x2.reshape(batch, seq, d_model)
```

```python
import functools

import jax
import jax.numpy as jnp
import numpy as np
from jax import lax
from jax.experimental import pallas as pl
from jax.experimental.pallas import tpu as pltpu

EPS = 1e-6
ROPE_THETA = 10000.0

A_HEADS = 8
A_HEAD_DIM = 64
MOBA_BLOCK = 256
MOBA_TOPK = 3

B_HEADS = 8
B_NOPE = 64
B_ROPE = 32
B_QK = B_NOPE + B_ROPE
B_V = 64
B_Q_RANK = 384
B_KV_RANK = 256

LANES = 128
HEAD_SLOT = 128
ROW_TILE = 512
FF_CHUNK = 1024
K_TILE = MOBA_BLOCK
OWN_BLOCKS = 4
Q_TILE = OWN_BLOCKS * MOBA_BLOCK
HEADS_PER_STEP = 8
MASK_VALUE = -1e30
LOG2_E = 1.4426950408889634
SUM_ROWS = 16
VMEM_LIMIT_BYTES = 56 * 1024 * 1024

F32 = jnp.float32
BF16 = jnp.bfloat16


def _rms(x, g):
    return x * lax.rsqrt(jnp.mean(x * x, axis=-1, keepdims=True) + EPS) * g


def _rope_lanes(x, cos, sin_signed):
    groups = []
    for g in range(x.shape[1] // LANES):
        xg = x[:, g * LANES:(g + 1) * LANES]
        groups.append(xg * cos + pltpu.roll(xg, LANES // 2, axis=1) * sin_signed)
    return groups


def _store_transposed(dst_ref, row0, t):
    rows, width = t.shape[0], dst_ref.shape[2]
    for tile in range(ROW_TILE // width):
        dst_ref[tile, row0:row0 + rows, :] = t[:, tile * width:(tile + 1) * width].astype(dst_ref.dtype)


def _pre_kernel(x_ref, gmix_ref, cos_a_ref, sin_a_ref, cos_b_ref, sin_b_ref,
                w_qkv_ref, w_cq_ref, w_ckv_ref, w_kr_ref, w_gate_ref, b_gate_ref,
                g_cq_ref, w_qup_ref, g_ckv_ref, w_kup_ref, w_vup_ref,
                qa_t_ref, ka_ref, va_t_ref, kmean_ref, gate_ref, qb_t_ref, kb_ref, vb_t_ref):
    a_width = A_HEADS * A_HEAD_DIM
    h = _rms(x_ref[...], gmix_ref[...]).astype(BF16)

    qkv = jnp.dot(h, w_qkv_ref[...], preferred_element_type=F32)
    cos_a, sin_a = cos_a_ref[...], sin_a_ref[...]
    q_groups = _rope_lanes(qkv[:, :a_width], cos_a, sin_a)
    k_groups = _rope_lanes(qkv[:, a_width:2 * a_width], cos_a, sin_a)
    first_head = (lax.broadcasted_iota(jnp.int32, (LANES, ROW_TILE), 0) % A_HEAD_DIM) < A_HEAD_DIM // 2
    for g, qg in enumerate(q_groups):
        t = (qg * (A_HEAD_DIM ** -0.5 * LOG2_E)).T
        _store_transposed(qa_t_ref, (2 * g) * HEAD_SLOT, jnp.where(first_head, t, 0.0))
        _store_transposed(qa_t_ref, (2 * g + 1) * HEAD_SLOT, jnp.where(first_head, 0.0, t))
    for g, kg in enumerate(k_groups):
        ka_ref[:, g * LANES:(g + 1) * LANES] = kg.astype(BF16)
        for blk in range(ROW_TILE // MOBA_BLOCK):
            kmean_ref[0, blk:blk + 1, g * LANES:(g + 1) * LANES] = jnp.mean(
                kg[blk * MOBA_BLOCK:(blk + 1) * MOBA_BLOCK], axis=0, keepdims=True)
    for g in range(a_width // LANES):
        vg = qkv[:, 2 * a_width + g * LANES:2 * a_width + (g + 1) * LANES]
        _store_transposed(va_t_ref, g * LANES, vg.T)

    gate = jnp.dot(h, w_gate_ref[...], preferred_element_type=F32) + b_gate_ref[...]
    gate_ref[...] = jax.nn.sigmoid(gate).astype(BF16)

    cos_b, sin_b = cos_b_ref[...], sin_b_ref[...]
    cq = jnp.dot(h, w_cq_ref[...], preferred_element_type=F32)
    cqn = _rms(cq, g_cq_ref[...]).astype(BF16)
    qb = jnp.dot(cqn, w_qup_ref[...], preferred_element_type=F32)
    for hd, qg in enumerate(_rope_lanes(qb, cos_b, sin_b)):
        _store_transposed(qb_t_ref, hd * HEAD_SLOT, (qg * (B_QK ** -0.5 * LOG2_E)).T)
    ckv = jnp.dot(h, w_ckv_ref[...], preferred_element_type=F32)
    ckvn = _rms(ckv, g_ckv_ref[...]).astype(BF16)
    k_nope = jnp.dot(ckvn, w_kup_ref[...], preferred_element_type=F32)
    kr = jnp.dot(h, w_kr_ref[...], preferred_element_type=F32)
    kr = _rope_lanes(kr, cos_b, sin_b)[0]
    for hd in range(B_HEADS):
        kb_ref[:, hd * HEAD_SLOT:(hd + 1) * HEAD_SLOT] = (
            k_nope[:, hd * HEAD_SLOT:(hd + 1) * HEAD_SLOT] + kr).astype(BF16)
    vb = jnp.dot(ckvn, w_vup_ref[...], preferred_element_type=F32)
    for g in range(B_HEADS * B_V // LANES):
        _store_transposed(vb_t_ref, g * LANES, vb[:, g * LANES:(g + 1) * LANES].T)


def _attn_kernel(*refs, moba):
    if moba:
        q_t_ref, k_ref, v_t_ref, kmean_ref, o_t_ref = refs[:5]
        s_scr, smax_scr, p_scr, alpha_scr, m_scr, acc_scr, bias_scr, sbias_scr = refs[5:]
    else:
        q_t_ref, k_ref, v_t_ref, o_t_ref = refs[:4]
        s_scr, smax_scr, p_scr, alpha_scr, m_scr, acc_scr = refs[4:]
    own_lo = OWN_BLOCKS * pl.program_id(2)
    heads = HEADS_PER_STEP
    vdim = v_t_ref.shape[1] // heads
    qry_pos = lax.broadcasted_iota(jnp.int32, (1, Q_TILE), 1)

    def key_lanes(hd):
        g = hd // 2 if moba else hd
        return slice(g * LANES, (g + 1) * LANES)

    def q_head(hd, first=0):
        rows = slice(hd * HEAD_SLOT, (hd + 1) * HEAD_SLOT)
        return jnp.concatenate([q_t_ref[u, rows, :] for u in range(first, OWN_BLOCKS)], axis=1)

    if moba:
        nblk = kmean_ref.shape[0]
        blk = lax.broadcasted_iota(jnp.int32, (nblk, Q_TILE), 0).astype(F32)
        own = (own_lo + qry_pos // MOBA_BLOCK).astype(F32)
        past = blk < own
        for hd in range(heads):
            gate = jnp.dot(kmean_ref[:, key_lanes(hd)].astype(BF16), q_head(hd),
                           preferred_element_type=F32)
            gate = jnp.where(past, gate, -jnp.inf)
            bias = jnp.full(gate.shape, MASK_VALUE, F32)
            for _ in range(MOBA_TOPK):
                best = jnp.max(gate, axis=0, keepdims=True)
                first = jnp.min(jnp.where(gate == best, blk, float(nblk)), axis=0, keepdims=True)
                pick = blk == first
                bias = jnp.where(pick, 0.0, bias)
                gate = jnp.where(pick, -jnp.inf, gate)
            bias_scr[hd] = jnp.where(past, bias, MASK_VALUE)


    def qk_stage(j, first=0, own=False):
        start = pl.multiple_of(j * K_TILE, K_TILE)
        k_blk = k_ref[pl.ds(start, K_TILE), :]
        cols = slice(first * MOBA_BLOCK, Q_TILE)
        for hd in range(heads):
            s = jnp.dot(k_blk[:, key_lanes(hd)], q_head(hd, first), preferred_element_type=F32)
            if own:
                key_pos = lax.broadcasted_iota(jnp.int32, s.shape, 0)
                s = jnp.where(key_pos <= qry_pos[:, :s.shape[1]], s, MASK_VALUE)
            s_scr[hd, :, cols] = s
            smax = jnp.max(s, axis=0, keepdims=True)
            if moba:
                bias = bias_scr[hd, pl.ds(j, 1), cols]
                if own:
                    bias = jnp.where(qry_pos[:, :s.shape[1]] >= MOBA_BLOCK, bias, 0.0)
                sbias_scr[hd, :, cols] = bias
                smax = smax + bias
            smax_scr[hd, :, cols] = smax

    ones_rows = jnp.ones((SUM_ROWS, K_TILE), BF16)

    def softmax_stage(first=0):
        cols = slice(first * MOBA_BLOCK, Q_TILE)
        for hd in range(heads):
            m_old = m_scr[hd, :, cols]
            m_new = jnp.maximum(m_old, smax_scr[hd, :, cols])
            shift = m_new - sbias_scr[hd, :, cols] if moba else m_new
            p_scr[hd, :, cols] = jnp.exp2(s_scr[hd, :, cols] - shift).astype(BF16)
            alpha_scr[hd, :, cols] = jnp.exp2(m_old - m_new)
            m_scr[hd, :, cols] = m_new

    def pv_stage(j, first=0):
        v_t = v_t_ref[j]
        cols = slice(first * MOBA_BLOCK, Q_TILE)
        for hd in range(heads):
            v_ext = jnp.concatenate([v_t[hd * vdim:(hd + 1) * vdim], ones_rows], axis=0)
            acc_scr[hd, :, cols] = alpha_scr[hd, :, cols] * acc_scr[hd, :, cols] + jnp.dot(
                v_ext, p_scr[hd, :, cols], preferred_element_type=F32)

    def visited_block(v):
        return jnp.where(v < OWN_BLOCKS, own_lo + (OWN_BLOCKS - 1) - v, v - OWN_BLOCKS)

    m_scr[...] = jnp.full(m_scr.shape, MASK_VALUE, F32)
    acc_scr[...] = jnp.zeros(acc_scr.shape, F32)
    p_scr[:, :, :MOBA_BLOCK] = jnp.zeros(p_scr.shape[:2] + (MOBA_BLOCK,), BF16)
    alpha_scr[:, :, :MOBA_BLOCK] = jnp.ones(alpha_scr.shape[:2] + (MOBA_BLOCK,), F32)
    qk_stage(own_lo + OWN_BLOCKS - 1, OWN_BLOCKS - 1, own=True)
    for v in range(OWN_BLOCKS - 1):
        u = OWN_BLOCKS - 1 - v
        if v > 0:
            pv_stage(own_lo + u + 1, u + 1)
        softmax_stage(u)
        qk_stage(own_lo + u - 1, u - 1, own=True)

    def body(t, carry):
        pv_stage(visited_block(t - 1))
        softmax_stage()
        qk_stage(t + 1 - OWN_BLOCKS)
        return carry

    last = own_lo + OWN_BLOCKS - 1
    lax.fori_loop(OWN_BLOCKS - 1, last, body, 0)
    pv_stage(visited_block(last - 1))
    softmax_stage()
    pv_stage(visited_block(last))

    for hd in range(heads):
        acc = acc_scr[hd]
        o = (acc[:vdim] / acc[vdim:vdim + 1]).astype(o_t_ref.dtype)
        for u in range(OWN_BLOCKS):
            o_t_ref[u, hd * vdim:(hd + 1) * vdim, :] = o[:, u * MOBA_BLOCK:(u + 1) * MOBA_BLOCK]


def _post_kernel(x_ref, oa_t_ref, ob_t_ref, gate_ref, w_oa_ref, w_ob_ref, w_out_ref,
                 g_mlp_ref, w_ff1_ref, w_ff2_ref, g_final_ref, out_ref, *, ff_chunk, final_norm):
    d_model = x_ref.shape[1]
    contract_rows = (((0,), (0,)), ((), ()))
    ya, yb = [], []
    for qt in range(ROW_TILE // K_TILE):
        ya.append(lax.dot_general(oa_t_ref[qt], w_oa_ref[...], contract_rows, preferred_element_type=F32))
        yb.append(lax.dot_general(ob_t_ref[qt], w_ob_ref[...], contract_rows, preferred_element_type=F32))
    ya = ya[0] if len(ya) == 1 else jnp.concatenate(ya, axis=0)
    yb = yb[0] if len(yb) == 1 else jnp.concatenate(yb, axis=0)
    mixed = gate_ref[:, :d_model].astype(F32) * ya + gate_ref[:, d_model:].astype(F32) * yb
    x1 = x_ref[...] + jnp.dot(mixed.astype(BF16), w_out_ref[...], preferred_element_type=F32)

    h2 = _rms(x1, g_mlp_ref[...]).astype(BF16)
    mlp = jnp.zeros_like(x1)
    for c in range(w_ff1_ref.shape[1] // ff_chunk):
        u = jnp.maximum(jnp.dot(h2, w_ff1_ref[:, c * ff_chunk:(c + 1) * ff_chunk],
                                preferred_element_type=F32), 0.0)
        mlp = mlp + jnp.dot((u * u).astype(BF16), w_ff2_ref[c * ff_chunk:(c + 1) * ff_chunk, :],
                            preferred_element_type=F32)
    x2 = x1 + mlp
    out_ref[...] = _rms(x2, g_final_ref[...]) if final_norm else x2


def _resident(shape):
    return pl.BlockSpec(shape, lambda *_: (0,) * len(shape), pipeline_mode=pl.Buffered(1))


def _rope_tables(seq, dim):
    half = dim // 2
    inv_freq = np.power(ROPE_THETA, -np.arange(half, dtype=np.float64) / half)
    ang = np.arange(seq, dtype=np.float64)[:, None] * inv_freq[None, :]
    return np.cos(ang).astype(np.float32), np.sin(ang).astype(np.float32)


def _moba_group_columns():
    half = A_HEAD_DIM // 2
    order = []
    for pair in range(A_HEADS // 2):
        h0, h1 = 2 * pair * A_HEAD_DIM, (2 * pair + 1) * A_HEAD_DIM
        for start in (h0, h1, h0 + half, h1 + half):
            order.extend(range(start, start + half))
    return jnp.asarray(order, jnp.int32)


def _mla_slot_columns(nope, rope, xp=jnp):
    r = B_ROPE // 2
    split = LANES // 2 - r
    pad = xp.zeros(nope.shape[:-1] + (HEAD_SLOT - B_QK,), nope.dtype)
    return xp.concatenate([rope[..., :r], nope[..., :split], rope[..., r:], nope[..., split:], pad], axis=-1)


def _params(*semantics):
    return pltpu.CompilerParams(dimension_semantics=semantics, vmem_limit_bytes=VMEM_LIMIT_BYTES)


def _pre_call(x2, seq, g_mix, w_in, b_gate, g_cq, w_q_up, g_ckv, w_kv_up):
    rows, d_model = x2.shape
    a_width = A_HEADS * A_HEAD_DIM
    o_q, o_k, o_v = 0, a_width, 2 * a_width
    o_cq = 3 * a_width
    o_ckv = o_cq + B_Q_RANK
    o_kr = o_ckv + B_KV_RANK
    o_gate = o_kr + B_ROPE
    w_in = w_in.astype(BF16)
    cols = _moba_group_columns()
    w_qkv = jnp.concatenate([w_in[:, o_q:o_k][:, cols], w_in[:, o_k:o_v][:, cols], w_in[:, o_v:o_cq]], axis=1)
    w_cq = w_in[:, o_cq:o_ckv]
    w_ckv = w_in[:, o_ckv:o_kr]
    w_kr = _mla_slot_columns(jnp.zeros((d_model, B_NOPE), BF16), w_in[:, o_kr:o_gate])
    w_gate = w_in[:, o_gate:]
    w_q = w_q_up.astype(BF16).reshape(B_Q_RANK, B_HEADS, B_QK)
    w_qup = _mla_slot_columns(w_q[..., :B_NOPE], w_q[..., B_NOPE:]).reshape(B_Q_RANK, B_HEADS * HEAD_SLOT)
    w_kv = w_kv_up.astype(BF16).reshape(B_KV_RANK, B_HEADS, B_NOPE + B_V)
    w_kup = _mla_slot_columns(w_kv[..., :B_NOPE], jnp.zeros((B_KV_RANK, B_HEADS, B_ROPE), BF16)).reshape(
        B_KV_RANK, B_HEADS * HEAD_SLOT)
    w_vup = w_kv[:, :, B_NOPE:].reshape(B_KV_RANK, B_HEADS * B_V)

    cos_a, sin_a = _rope_tables(seq, A_HEAD_DIM)
    cos_a = np.tile(cos_a, (1, LANES // (A_HEAD_DIM // 2)))
    sin_a = np.concatenate([-sin_a, -sin_a, sin_a, sin_a], axis=1)
    cos_b, sin_b = _rope_tables(seq, B_ROPE)
    cos_b = _mla_slot_columns(np.ones((seq, B_NOPE), np.float32), np.concatenate([cos_b, cos_b], axis=1), np)
    sin_b = _mla_slot_columns(np.zeros((seq, B_NOPE), np.float32), np.concatenate([-sin_b, sin_b], axis=1), np)

    n_tiles = rows // ROW_TILE
    seq_tiles = seq // ROW_TILE
    row_spec = lambda w: pl.BlockSpec((ROW_TILE, w), lambda r: (r, 0))
    table_spec = pl.BlockSpec((ROW_TILE, LANES), lambda r: (r % seq_tiles, 0))
    t_spec = lambda h, width: pl.BlockSpec((ROW_TILE // width, h, width), lambda r: (r, 0, 0))
    t_shape = lambda h, width: jax.ShapeDtypeStruct((rows // width, h, width), BF16)
    vec = lambda a: a.reshape(1, -1).astype(F32)
    operands = [
        (x2, row_spec(d_model)), (vec(g_mix), _resident((1, d_model))),
        (cos_a, table_spec), (sin_a, table_spec), (cos_b, table_spec), (sin_b, table_spec),
        (w_qkv, _resident(w_qkv.shape)), (w_cq, _resident(w_cq.shape)), (w_ckv, _resident(w_ckv.shape)),
        (w_kr, _resident(w_kr.shape)), (w_gate, _resident(w_gate.shape)),
        (vec(b_gate), _resident((1, 2 * d_model))),
        (vec(g_cq), _resident((1, B_Q_RANK))), (w_qup, _resident(w_qup.shape)),
        (vec(g_ckv), _resident((1, B_KV_RANK))), (w_kup, _resident(w_kup.shape)),
        (w_vup, _resident(w_vup.shape)),
    ]
    out_shape = [
        t_shape(A_HEADS * HEAD_SLOT, K_TILE),
        jax.ShapeDtypeStruct((rows, a_width), BF16),
        t_shape(a_width, K_TILE),
        jax.ShapeDtypeStruct((n_tiles, ROW_TILE // MOBA_BLOCK, a_width), F32),
        jax.ShapeDtypeStruct((rows, 2 * d_model), BF16),
        t_shape(B_HEADS * HEAD_SLOT, K_TILE),
        jax.ShapeDtypeStruct((rows, B_HEADS * HEAD_SLOT), BF16),
        t_shape(B_HEADS * B_V, K_TILE),
    ]
    out_specs = [
        t_spec(A_HEADS * HEAD_SLOT, K_TILE), row_spec(a_width), t_spec(a_width, K_TILE),
        pl.BlockSpec((1, ROW_TILE // MOBA_BLOCK, a_width), lambda r: (r, 0, 0)),
        row_spec(2 * d_model), t_spec(B_HEADS * HEAD_SLOT, K_TILE), row_spec(B_HEADS * HEAD_SLOT),
        t_spec(B_HEADS * B_V, K_TILE),
    ]
    return pl.pallas_call(
        _pre_kernel,
        grid=(n_tiles,),
        in_specs=[s for _, s in operands],
        out_specs=out_specs,
        out_shape=out_shape,
        compiler_params=_params("parallel"),
        name="pre_proj",
    )(*[a for a, _ in operands])


def _attn_call(q_t, k, v_t, kmean, *, batch, seq, moba):
    nblk = seq // K_TILE
    n_q = seq // Q_TILE
    heads = q_t.shape[1] // HEAD_SLOT
    groups = heads // HEADS_PER_STEP
    vdim = v_t.shape[2] // heads
    key_lanes = k.shape[2] // groups
    kv_mode = None if (moba and OWN_BLOCKS <= 2) else pl.Buffered(1)
    in_specs = [
        pl.BlockSpec((OWN_BLOCKS, HEADS_PER_STEP * HEAD_SLOT, K_TILE), lambda b, g, i: (b * n_q + i, g, 0)),
        pl.BlockSpec((None, seq, key_lanes), lambda b, g, i: (b, 0, g), pipeline_mode=kv_mode),
        pl.BlockSpec((None, nblk, HEADS_PER_STEP * vdim, K_TILE), lambda b, g, i: (b, 0, g, 0),
                     pipeline_mode=kv_mode),
    ]
    operands = [q_t, k, v_t]
    stat = pltpu.VMEM((HEADS_PER_STEP, 1, Q_TILE), F32)
    scratch = [
        pltpu.VMEM((HEADS_PER_STEP, K_TILE, Q_TILE), F32),
        stat,
        pltpu.VMEM((HEADS_PER_STEP, K_TILE, Q_TILE), BF16),
        stat,
        stat,
        pltpu.VMEM((HEADS_PER_STEP, vdim + SUM_ROWS, Q_TILE), F32),
    ]
    if moba:
        in_specs.append(pl.BlockSpec((None, nblk, key_lanes), lambda b, g, i: (b, 0, g)))
        operands.append(kmean)
        scratch.append(pltpu.VMEM((HEADS_PER_STEP, nblk, Q_TILE), F32))
        scratch.append(stat)
    return pl.pallas_call(
        functools.partial(_attn_kernel, moba=moba),
        grid=(batch, groups, n_q),
        in_specs=in_specs,
        out_specs=pl.BlockSpec((OWN_BLOCKS, HEADS_PER_STEP * vdim, K_TILE), lambda b, g, i: (b * n_q + i, g, 0)),
        out_shape=jax.ShapeDtypeStruct((batch * nblk, heads * vdim, K_TILE), BF16),
        scratch_shapes=scratch,
        compiler_params=_params("parallel", "parallel", "arbitrary"),
        name="moba_attn" if moba else "mla_attn",
    )(*operands)


def _post_call(x2, oa_t, ob_t, gates, w_o_a, w_o_b, w_out, g_mlp, w_ff1, w_ff2, g_final, final_norm):
    rows, d_model = x2.shape
    row_spec = lambda w: pl.BlockSpec((ROW_TILE, w), lambda r: (r, 0))
    t_spec = lambda a: pl.BlockSpec((ROW_TILE // K_TILE,) + a.shape[1:], lambda r: (r, 0, 0))
    vec = lambda a: a.reshape(1, -1).astype(F32)
    weights = [w.astype(BF16) for w in (w_o_a, w_o_b, w_out)]
    operands = [(x2, row_spec(d_model)), (oa_t, t_spec(oa_t)), (ob_t, t_spec(ob_t)),
                (gates, row_spec(2 * d_model))]
    operands += [(w, _resident(w.shape)) for w in weights]
    operands += [(vec(g_mlp), _resident((1, d_model))),
                 (w_ff1.astype(BF16), _resident(w_ff1.shape)), (w_ff2.astype(BF16), _resident(w_ff2.shape)),
                 (vec(g_final), _resident((1, d_model)))]
    return pl.pallas_call(
        functools.partial(_post_kernel, ff_chunk=FF_CHUNK, final_norm=final_norm),
        grid=(rows // ROW_TILE,),
        in_specs=[s for _, s in operands],
        out_specs=row_spec(d_model),
        out_shape=jax.ShapeDtypeStruct((rows, d_model), F32),
        compiler_params=_params("parallel"),
        name="post_mlp",
    )(*[a for a, _ in operands])


def kernel(x, g_mix, w_in, b_gate, g_cq, w_q_up, g_ckv, w_kv_up, w_o_a, w_o_b, w_out, g_mlp, w_ff1,
           w_ff2, g_final):
    batch, seq, d_model = x.shape
    depth = w_in.shape[0]
    assert seq % ROW_TILE == 0 and ROW_TILE % MOBA_BLOCK == 0 and seq % Q_TILE == 0 and OWN_BLOCKS >= 2
    assert A_HEADS % HEADS_PER_STEP == 0 and B_HEADS % HEADS_PER_STEP == 0 and HEADS_PER_STEP % 2 == 0
    nblk = seq // K_TILE
    x2 = x.reshape(batch * seq, d_model)
    for l in range(depth):
        qa_t, ka, va_t, kmean, gates, qb_t, kb, vb_t = _pre_call(
            x2, seq, g_mix[l], w_in[l], b_gate[l], g_cq[l], w_q_up[l], g_ckv[l], w_kv_up[l])
        oa_t = _attn_call(qa_t, ka.reshape(batch, seq, -1), va_t.reshape(batch, nblk, -1, K_TILE),
                          kmean.reshape(batch, nblk, -1), batch=batch, seq=seq, moba=True)
        ob_t = _attn_call(qb_t, kb.reshape(batch, seq, -1), vb_t.reshape(batch, nblk, -1, K_TILE),
                          None, batch=batch, seq=seq, moba=False)
        x2 = _post_call(x2, oa_t, ob_t, gates, w_o_a[l], w_o_b[l], w_out[l], g_mlp[l], w_ff1[l],
                        w_ff2[l], g_final, final_norm=(l == depth - 1))
    return x2.reshape(batch, seq, d_model)
```

```python
import functools

import jax
import jax.numpy as jnp
import numpy as np
from jax import lax
from jax.experimental import pallas as pl
from jax.experimental.pallas import tpu as pltpu

EPS = 1e-6
ROPE_THETA = 10000.0

A_HEADS = 8
A_HEAD_DIM = 64
MOBA_BLOCK = 256
MOBA_TOPK = 3

B_HEADS = 8
B_NOPE = 64
B_ROPE = 32
B_QK = B_NOPE + B_ROPE
B_V = 64
B_Q_RANK = 384
B_KV_RANK = 256

LANES = 128
HEAD_SLOT = 128
ROW_TILE = 512
FF_CHUNK = 1024
K_TILE = MOBA_BLOCK
OWN_BLOCKS = 4
Q_TILE = OWN_BLOCKS * MOBA_BLOCK
HEADS_PER_STEP = 4
K_STEP = 2 * K_TILE
MASK_VALUE = -1e30
LOG2_E = 1.4426950408889634
SUM_ROWS = 16
VMEM_LIMIT_BYTES = 56 * 1024 * 1024

F32 = jnp.float32
BF16 = jnp.bfloat16


def _rms(x, g):
    return x * lax.rsqrt(jnp.mean(x * x, axis=-1, keepdims=True) + EPS) * g


def _rope_lanes(x, cos, sin_signed):
    groups = []
    for g in range(x.shape[1] // LANES):
        xg = x[:, g * LANES:(g + 1) * LANES]
        groups.append(xg * cos + pltpu.roll(xg, LANES // 2, axis=1) * sin_signed)
    return groups


def _store_transposed(dst_ref, row0, t):
    rows, width = t.shape[0], dst_ref.shape[2]
    for tile in range(ROW_TILE // width):
        dst_ref[tile, row0:row0 + rows, :] = t[:, tile * width:(tile + 1) * width].astype(dst_ref.dtype)


def _pre_kernel(x_ref, gmix_ref, cos_a_ref, sin_a_ref, cos_b_ref, sin_b_ref,
                w_qkv_ref, w_cq_ref, w_ckv_ref, w_kr_ref, w_gate_ref, b_gate_ref,
                g_cq_ref, w_qup_ref, g_ckv_ref, w_kup_ref, w_vup_ref,
                qa_t_ref, ka_ref, va_t_ref, kmean_ref, gate_ref, qb_t_ref, kb_ref, vb_t_ref):
    a_width = A_HEADS * A_HEAD_DIM
    h = _rms(x_ref[...], gmix_ref[...]).astype(BF16)

    qkv = jnp.dot(h, w_qkv_ref[...], preferred_element_type=F32)
    cos_a, sin_a = cos_a_ref[...], sin_a_ref[...]
    q_groups = _rope_lanes(qkv[:, :a_width], cos_a, sin_a)
    k_groups = _rope_lanes(qkv[:, a_width:2 * a_width], cos_a, sin_a)
    first_head = (lax.broadcasted_iota(jnp.int32, (LANES, ROW_TILE), 0) % A_HEAD_DIM) < A_HEAD_DIM // 2
    for g, qg in enumerate(q_groups):
        t = (qg * (A_HEAD_DIM ** -0.5 * LOG2_E)).T
        _store_transposed(qa_t_ref, (2 * g) * HEAD_SLOT, jnp.where(first_head, t, 0.0))
        _store_transposed(qa_t_ref, (2 * g + 1) * HEAD_SLOT, jnp.where(first_head, 0.0, t))
    for g, kg in enumerate(k_groups):
        ka_ref[:, g * LANES:(g + 1) * LANES] = kg.astype(BF16)
        for blk in range(ROW_TILE // MOBA_BLOCK):
            kmean_ref[0, blk:blk + 1, g * LANES:(g + 1) * LANES] = jnp.mean(
                kg[blk * MOBA_BLOCK:(blk + 1) * MOBA_BLOCK], axis=0, keepdims=True)
    for g in range(a_width // LANES):
        vg = qkv[:, 2 * a_width + g * LANES:2 * a_width + (g + 1) * LANES]
        _store_transposed(va_t_ref, g * LANES, vg.T)

    gate = jnp.dot(h, w_gate_ref[...], preferred_element_type=F32) + b_gate_ref[...]
    gate_ref[...] = jax.nn.sigmoid(gate).astype(BF16)

    cos_b, sin_b = cos_b_ref[...], sin_b_ref[...]
    cq = jnp.dot(h, w_cq_ref[...], preferred_element_type=F32)
    cqn = _rms(cq, g_cq_ref[...]).astype(BF16)
    qb = jnp.dot(cqn, w_qup_ref[...], preferred_element_type=F32)
    for hd, qg in enumerate(_rope_lanes(qb, cos_b, sin_b)):
        _store_transposed(qb_t_ref, hd * HEAD_SLOT, (qg * (B_QK ** -0.5 * LOG2_E)).T)
    ckv = jnp.dot(h, w_ckv_ref[...], preferred_element_type=F32)
    ckvn = _rms(ckv, g_ckv_ref[...]).astype(BF16)
    k_nope = jnp.dot(ckvn, w_kup_ref[...], preferred_element_type=F32)
    kr = jnp.dot(h, w_kr_ref[...], preferred_element_type=F32)
    kr = _rope_lanes(kr, cos_b, sin_b)[0]
    for hd in range(B_HEADS):
        kb_ref[:, hd * HEAD_SLOT:(hd + 1) * HEAD_SLOT] = (
            k_nope[:, hd * HEAD_SLOT:(hd + 1) * HEAD_SLOT] + kr).astype(BF16)
    vb = jnp.dot(ckvn, w_vup_ref[...], preferred_element_type=F32)
    for g in range(B_HEADS * B_V // LANES):
        _store_transposed(vb_t_ref, g * LANES, vb[:, g * LANES:(g + 1) * LANES].T)


def _attn_kernel(*refs, moba):
    if moba:
        q_t_ref, k_ref, v_t_ref, kmean_ref, o_t_ref = refs[:5]
        s_scr, smax_scr, p_scr, alpha_scr, m_scr, acc_scr, bias_scr, sbias_scr = refs[5:]
    else:
        q_t_ref, k_ref, v_t_ref, o_t_ref = refs[:4]
        s_scr, smax_scr, p_scr, alpha_scr, m_scr, acc_scr = refs[4:]
    own_lo = OWN_BLOCKS * pl.program_id(2)
    heads = HEADS_PER_STEP
    vdim = v_t_ref.shape[1] // heads
    qry_pos = lax.broadcasted_iota(jnp.int32, (1, Q_TILE), 1)

    def key_lanes(hd):
        g = hd // 2 if moba else hd
        return slice(g * LANES, (g + 1) * LANES)

    def q_head(hd, first=0):
        rows = slice(hd * HEAD_SLOT, (hd + 1) * HEAD_SLOT)
        return jnp.concatenate([q_t_ref[u, rows, :] for u in range(first, OWN_BLOCKS)], axis=1)

    if moba:
        nblk = kmean_ref.shape[0]
        blk = lax.broadcasted_iota(jnp.int32, (nblk, Q_TILE), 0).astype(F32)
        own = (own_lo + qry_pos // MOBA_BLOCK).astype(F32)
        past = blk < own
        for hd in range(heads):
            gate = jnp.dot(kmean_ref[:, key_lanes(hd)].astype(BF16), q_head(hd),
                           preferred_element_type=F32)
            gate = jnp.where(past, gate, -jnp.inf)
            bias = jnp.full(gate.shape, MASK_VALUE, F32)
            for _ in range(MOBA_TOPK):
                best = jnp.max(gate, axis=0, keepdims=True)
                first = jnp.min(jnp.where(gate == best, blk, float(nblk)), axis=0, keepdims=True)
                pick = blk == first
                bias = jnp.where(pick, 0.0, bias)
                gate = jnp.where(pick, -jnp.inf, gate)
            bias_scr[hd] = jnp.where(past, bias, MASK_VALUE)


    def qk_stage(j0, first=0, own=False):
        start = pl.multiple_of(j0 * K_TILE, K_TILE)
        k_blk = k_ref[pl.ds(start, K_STEP), :]
        cols = slice(first * MOBA_BLOCK, Q_TILE)
        for hd in range(heads):
            s = jnp.dot(k_blk[:, key_lanes(hd)], q_head(hd, first), preferred_element_type=F32)
            qry = qry_pos[:, :s.shape[1]]
            if own:
                key_pos = lax.broadcasted_iota(jnp.int32, s.shape, 0)
                s = jnp.where(key_pos <= qry, s, MASK_VALUE)
            s_scr[hd, :, cols] = s
            halves = [jnp.max(s[h * K_TILE:(h + 1) * K_TILE], axis=0, keepdims=True) for h in range(2)]
            if moba:
                for h in range(2):
                    bias = bias_scr[hd, pl.ds(j0 + h, 1), cols]
                    if own:
                        bias = jnp.where(qry >= (h + 1) * MOBA_BLOCK, bias, 0.0)
                    sbias_scr[hd, h:h + 1, cols] = bias
                    halves[h] = halves[h] + bias
            smax_scr[hd, :, cols] = jnp.maximum(halves[0], halves[1])

    ones_rows = jnp.ones((SUM_ROWS, K_STEP), BF16)

    def softmax_stage(first=0):
        cols = slice(first * MOBA_BLOCK, Q_TILE)
        for hd in range(heads):
            m_old = m_scr[hd, :, cols]
            m_new = jnp.maximum(m_old, smax_scr[hd, :, cols])
            if moba:
                for h in range(2):
                    rows = slice(h * K_TILE, (h + 1) * K_TILE)
                    shift = m_new - sbias_scr[hd, h:h + 1, cols]
                    p_scr[hd, rows, cols] = jnp.exp2(s_scr[hd, rows, cols] - shift).astype(BF16)
            else:
                p_scr[hd, :, cols] = jnp.exp2(s_scr[hd, :, cols] - m_new).astype(BF16)
            alpha_scr[hd, :, cols] = jnp.exp2(m_old - m_new)
            m_scr[hd, :, cols] = m_new

    def pv_stage(j0, first=0):
        v_t = jnp.concatenate([v_t_ref[j0], v_t_ref[j0 + 1]], axis=1)
        cols = slice(first * MOBA_BLOCK, Q_TILE)
        for hd in range(heads):
            v_ext = jnp.concatenate([v_t[hd * vdim:(hd + 1) * vdim], ones_rows], axis=0)
            acc_scr[hd, :, cols] = alpha_scr[hd, :, cols] * acc_scr[hd, :, cols] + jnp.dot(
                v_ext, p_scr[hd, :, cols], preferred_element_type=F32)

    own_pairs = OWN_BLOCKS // 2

    def visited_pair(v):
        return jnp.where(v < own_pairs, own_lo + OWN_BLOCKS - 2 - 2 * v, 2 * (v - own_pairs))

    m_scr[...] = jnp.full(m_scr.shape, MASK_VALUE, F32)
    acc_scr[...] = jnp.zeros(acc_scr.shape, F32)
    p_scr[:, :, :K_STEP] = jnp.zeros(p_scr.shape[:2] + (K_STEP,), BF16)
    alpha_scr[:, :, :K_STEP] = jnp.ones(alpha_scr.shape[:2] + (K_STEP,), F32)
    qk_stage(own_lo + OWN_BLOCKS - 2, OWN_BLOCKS - 2, own=True)
    for v in range(own_pairs - 1):
        u = OWN_BLOCKS - 2 - 2 * v
        if v > 0:
            pv_stage(own_lo + u + 2, u + 2)
        softmax_stage(u)
        qk_stage(own_lo + u - 2, u - 2, own=True)

    def body(t, carry):
        pv_stage(visited_pair(t - 1))
        softmax_stage()
        qk_stage(2 * (t + 1 - own_pairs))
        return carry

    last = own_lo // 2 + own_pairs - 1
    lax.fori_loop(own_pairs - 1, last, body, 0)
    pv_stage(visited_pair(last - 1))
    softmax_stage()
    pv_stage(visited_pair(last))

    for hd in range(heads):
        acc = acc_scr[hd]
        o = (acc[:vdim] / acc[vdim:vdim + 1]).astype(o_t_ref.dtype)
        for u in range(OWN_BLOCKS):
            o_t_ref[u, hd * vdim:(hd + 1) * vdim, :] = o[:, u * MOBA_BLOCK:(u + 1) * MOBA_BLOCK]


def _post_kernel(x_ref, oa_t_ref, ob_t_ref, gate_ref, w_oa_ref, w_ob_ref, w_out_ref,
                 g_mlp_ref, w_ff1_ref, w_ff2_ref, g_final_ref, out_ref, *, ff_chunk, final_norm):
    d_model = x_ref.shape[1]
    contract_rows = (((0,), (0,)), ((), ()))
    ya, yb = [], []
    for qt in range(ROW_TILE // K_TILE):
        ya.append(lax.dot_general(oa_t_ref[qt], w_oa_ref[...], contract_rows, preferred_element_type=F32))
        yb.append(lax.dot_general(ob_t_ref[qt], w_ob_ref[...], contract_rows, preferred_element_type=F32))
    ya = ya[0] if len(ya) == 1 else jnp.concatenate(ya, axis=0)
    yb = yb[0] if len(yb) == 1 else jnp.concatenate(yb, axis=0)
    mixed = gate_ref[:, :d_model].astype(F32) * ya + gate_ref[:, d_model:].astype(F32) * yb
    x1 = x_ref[...] + jnp.dot(mixed.astype(BF16), w_out_ref[...], preferred_element_type=F32)

    h2 = _rms(x1, g_mlp_ref[...]).astype(BF16)
    mlp = jnp.zeros_like(x1)
    for c in range(w_ff1_ref.shape[1] // ff_chunk):
        u = jnp.maximum(jnp.dot(h2, w_ff1_ref[:, c * ff_chunk:(c + 1) * ff_chunk],
                                preferred_element_type=F32), 0.0)
        mlp = mlp + jnp.dot((u * u).astype(BF16), w_ff2_ref[c * ff_chunk:(c + 1) * ff_chunk, :],
                            preferred_element_type=F32)
    x2 = x1 + mlp
    out_ref[...] = _rms(x2, g_final_ref[...]) if final_norm else x2


def _resident(shape):
    return pl.BlockSpec(shape, lambda *_: (0,) * len(shape), pipeline_mode=pl.Buffered(1))


def _rope_tables(seq, dim):
    half = dim // 2
    inv_freq = np.power(ROPE_THETA, -np.arange(half, dtype=np.float64) / half)
    ang = np.arange(seq, dtype=np.float64)[:, None] * inv_freq[None, :]
    return np.cos(ang).astype(np.float32), np.sin(ang).astype(np.float32)


def _moba_group_columns():
    half = A_HEAD_DIM // 2
    order = []
    for pair in range(A_HEADS // 2):
        h0, h1 = 2 * pair * A_HEAD_DIM, (2 * pair + 1) * A_HEAD_DIM
        for start in (h0, h1, h0 + half, h1 + half):
            order.extend(range(start, start + half))
    return jnp.asarray(order, jnp.int32)


def _mla_slot_columns(nope, rope, xp=jnp):
    r = B_ROPE // 2
    split = LANES // 2 - r
    pad = xp.zeros(nope.shape[:-1] + (HEAD_SLOT - B_QK,), nope.dtype)
    return xp.concatenate([rope[..., :r], nope[..., :split], rope[..., r:], nope[..., split:], pad], axis=-1)


def _params(*semantics):
    return pltpu.CompilerParams(dimension_semantics=semantics, vmem_limit_bytes=VMEM_LIMIT_BYTES)


def _pre_call(x2, seq, g_mix, w_in, b_gate, g_cq, w_q_up, g_ckv, w_kv_up):
    rows, d_model = x2.shape
    a_width = A_HEADS * A_HEAD_DIM
    o_q, o_k, o_v = 0, a_width, 2 * a_width
    o_cq = 3 * a_width
    o_ckv = o_cq + B_Q_RANK
    o_kr = o_ckv + B_KV_RANK
    o_gate = o_kr + B_ROPE
    w_in = w_in.astype(BF16)
    cols = _moba_group_columns()
    w_qkv = jnp.concatenate([w_in[:, o_q:o_k][:, cols], w_in[:, o_k:o_v][:, cols], w_in[:, o_v:o_cq]], axis=1)
    w_cq = w_in[:, o_cq:o_ckv]
    w_ckv = w_in[:, o_ckv:o_kr]
    w_kr = _mla_slot_columns(jnp.zeros((d_model, B_NOPE), BF16), w_in[:, o_kr:o_gate])
    w_gate = w_in[:, o_gate:]
    w_q = w_q_up.astype(BF16).reshape(B_Q_RANK, B_HEADS, B_QK)
    w_qup = _mla_slot_columns(w_q[..., :B_NOPE], w_q[..., B_NOPE:]).reshape(B_Q_RANK, B_HEADS * HEAD_SLOT)
    w_kv = w_kv_up.astype(BF16).reshape(B_KV_RANK, B_HEADS, B_NOPE + B_V)
    w_kup = _mla_slot_columns(w_kv[..., :B_NOPE], jnp.zeros((B_KV_RANK, B_HEADS, B_ROPE), BF16)).reshape(
        B_KV_RANK, B_HEADS * HEAD_SLOT)
    w_vup = w_kv[:, :, B_NOPE:].reshape(B_KV_RANK, B_HEADS * B_V)

    cos_a, sin_a = _rope_tables(seq, A_HEAD_DIM)
    cos_a = np.tile(cos_a, (1, LANES // (A_HEAD_DIM // 2)))
    sin_a = np.concatenate([-sin_a, -sin_a, sin_a, sin_a], axis=1)
    cos_b, sin_b = _rope_tables(seq, B_ROPE)
    cos_b = _mla_slot_columns(np.ones((seq, B_NOPE), np.float32), np.concatenate([cos_b, cos_b], axis=1), np)
    sin_b = _mla_slot_columns(np.zeros((seq, B_NOPE), np.float32), np.concatenate([-sin_b, sin_b], axis=1), np)

    n_tiles = rows // ROW_TILE
    seq_tiles = seq // ROW_TILE
    row_spec = lambda w: pl.BlockSpec((ROW_TILE, w), lambda r: (r, 0))
    table_spec = pl.BlockSpec((ROW_TILE, LANES), lambda r: (r % seq_tiles, 0))
    t_spec = lambda h, width: pl.BlockSpec((ROW_TILE // width, h, width), lambda r: (r, 0, 0))
    t_shape = lambda h, width: jax.ShapeDtypeStruct((rows // width, h, width), BF16)
    vec = lambda a: a.reshape(1, -1).astype(F32)
    operands = [
        (x2, row_spec(d_model)), (vec(g_mix), _resident((1, d_model))),
        (cos_a, table_spec), (sin_a, table_spec), (cos_b, table_spec), (sin_b, table_spec),
        (w_qkv, _resident(w_qkv.shape)), (w_cq, _resident(w_cq.shape)), (w_ckv, _resident(w_ckv.shape)),
        (w_kr, _resident(w_kr.shape)), (w_gate, _resident(w_gate.shape)),
        (vec(b_gate), _resident((1, 2 * d_model))),
        (vec(g_cq), _resident((1, B_Q_RANK))), (w_qup, _resident(w_qup.shape)),
        (vec(g_ckv), _resident((1, B_KV_RANK))), (w_kup, _resident(w_kup.shape)),
        (w_vup, _resident(w_vup.shape)),
    ]
    out_shape = [
        t_shape(A_HEADS * HEAD_SLOT, K_TILE),
        jax.ShapeDtypeStruct((rows, a_width), BF16),
        t_shape(a_width, K_TILE),
        jax.ShapeDtypeStruct((n_tiles, ROW_TILE // MOBA_BLOCK, a_width), F32),
        jax.ShapeDtypeStruct((rows, 2 * d_model), BF16),
        t_shape(B_HEADS * HEAD_SLOT, K_TILE),
        jax.ShapeDtypeStruct((rows, B_HEADS * HEAD_SLOT), BF16),
        t_shape(B_HEADS * B_V, K_TILE),
    ]
    out_specs = [
        t_spec(A_HEADS * HEAD_SLOT, K_TILE), row_spec(a_width), t_spec(a_width, K_TILE),
        pl.BlockSpec((1, ROW_TILE // MOBA_BLOCK, a_width), lambda r: (r, 0, 0)),
        row_spec(2 * d_model), t_spec(B_HEADS * HEAD_SLOT, K_TILE), row_spec(B_HEADS * HEAD_SLOT),
        t_spec(B_HEADS * B_V, K_TILE),
    ]
    return pl.pallas_call(
        _pre_kernel,
        grid=(n_tiles,),
        in_specs=[s for _, s in operands],
        out_specs=out_specs,
        out_shape=out_shape,
        compiler_params=_params("parallel"),
        name="pre_proj",
    )(*[a for a, _ in operands])


def _attn_call(q_t, k, v_t, kmean, *, batch, seq, moba):
    nblk = seq // K_TILE
    n_q = seq // Q_TILE
    heads = q_t.shape[1] // HEAD_SLOT
    groups = heads // HEADS_PER_STEP
    vdim = v_t.shape[2] // heads
    key_lanes = k.shape[2] // groups
    kv_mode = None
    in_specs = [
        pl.BlockSpec((OWN_BLOCKS, HEADS_PER_STEP * HEAD_SLOT, K_TILE), lambda b, g, i: (b * n_q + i, g, 0)),
        pl.BlockSpec((None, seq, key_lanes), lambda b, g, i: (b, 0, g), pipeline_mode=kv_mode),
        pl.BlockSpec((None, nblk, HEADS_PER_STEP * vdim, K_TILE), lambda b, g, i: (b, 0, g, 0),
                     pipeline_mode=kv_mode),
    ]
    operands = [q_t, k, v_t]
    stat = pltpu.VMEM((HEADS_PER_STEP, 1, Q_TILE), F32)
    scratch = [
        pltpu.VMEM((HEADS_PER_STEP, K_STEP, Q_TILE), F32),
        stat,
        pltpu.VMEM((HEADS_PER_STEP, K_STEP, Q_TILE), BF16),
        stat,
        stat,
        pltpu.VMEM((HEADS_PER_STEP, vdim + SUM_ROWS, Q_TILE), F32),
    ]
    if moba:
        in_specs.append(pl.BlockSpec((None, nblk, key_lanes), lambda b, g, i: (b, 0, g)))
        operands.append(kmean)
        scratch.append(pltpu.VMEM((HEADS_PER_STEP, nblk, Q_TILE), F32))
        scratch.append(pltpu.VMEM((HEADS_PER_STEP, 2, Q_TILE), F32))
    return pl.pallas_call(
        functools.partial(_attn_kernel, moba=moba),
        grid=(batch, groups, n_q),
        in_specs=in_specs,
        out_specs=pl.BlockSpec((OWN_BLOCKS, HEADS_PER_STEP * vdim, K_TILE), lambda b, g, i: (b * n_q + i, g, 0)),
        out_shape=jax.ShapeDtypeStruct((batch * nblk, heads * vdim, K_TILE), BF16),
        scratch_shapes=scratch,
        compiler_params=_params("parallel", "parallel", "arbitrary"),
        name="moba_attn" if moba else "mla_attn",
    )(*operands)


def _post_call(x2, oa_t, ob_t, gates, w_o_a, w_o_b, w_out, g_mlp, w_ff1, w_ff2, g_final, final_norm):
    rows, d_model = x2.shape
    row_spec = lambda w: pl.BlockSpec((ROW_TILE, w), lambda r: (r, 0))
    t_spec = lambda a: pl.BlockSpec((ROW_TILE // K_TILE,) + a.shape[1:], lambda r: (r, 0, 0))
    vec = lambda a: a.reshape(1, -1).astype(F32)
    weights = [w.astype(BF16) for w in (w_o_a, w_o_b, w_out)]
    operands = [(x2, row_spec(d_model)), (oa_t, t_spec(oa_t)), (ob_t, t_spec(ob_t)),
                (gates, row_spec(2 * d_model))]
    operands += [(w, _resident(w.shape)) for w in weights]
    operands += [(vec(g_mlp), _resident((1, d_model))),
                 (w_ff1.astype(BF16), _resident(w_ff1.shape)), (w_ff2.astype(BF16), _resident(w_ff2.shape)),
                 (vec(g_final), _resident((1, d_model)))]
    return pl.pallas_call(
        functools.partial(_post_kernel, ff_chunk=FF_CHUNK, final_norm=final_norm),
        grid=(rows // ROW_TILE,),
        in_specs=[s for _, s in operands],
        out_specs=row_spec(d_model),
        out_shape=jax.ShapeDtypeStruct((rows, d_model), F32),
        compiler_params=_params("parallel"),
        name="post_mlp",
    )(*[a for a, _ in operands])


def kernel(x, g_mix, w_in, b_gate, g_cq, w_q_up, g_ckv, w_kv_up, w_o_a, w_o_b, w_out, g_mlp, w_ff1,
           w_ff2, g_final):
    batch, seq, d_model = x.shape
    depth = w_in.shape[0]
    assert seq % ROW_TILE == 0 and ROW_TILE % MOBA_BLOCK == 0 and seq % Q_TILE == 0 and OWN_BLOCKS >= 2 and OWN_BLOCKS % 2 == 0
    assert A_HEADS % HEADS_PER_STEP == 0 and B_HEADS % HEADS_PER_STEP == 0 and HEADS_PER_STEP % 2 == 0
    nblk = seq // K_TILE
    x2 = x.reshape(batch * seq, d_model)
    for l in range(depth):
        qa_t, ka, va_t, kmean, gates, qb_t, kb, vb_t = _pre_call(
            x2, seq, g_mix[l], w_in[l], b_gate[l], g_cq[l], w_q_up[l], g_ckv[l], w_kv_up[l])
        oa_t = _attn_call(qa_t, ka.reshape(batch, seq, -1), va_t.reshape(batch, nblk, -1, K_TILE),
                          kmean.reshape(batch, nblk, -1), batch=batch, seq=seq, moba=True)
        ob_t = _attn_call(qb_t, kb.reshape(batch, seq, -1), vb_t.reshape(batch, nblk, -1, K_TILE),
                          None, batch=batch, seq=seq, moba=False)
        x2 = _post_call(x2, oa_t, ob_t, gates, w_o_a[l], w_o_b[l], w_out[l], g_mlp[l], w_ff1[l],
                        w_ff2[l], g_final, final_norm=(l == depth - 1))
    return x2.reshape(batch, seq, d_model)
```

```python
import functools

import jax
import jax.numpy as jnp
import numpy as np
from jax import lax
from jax.experimental import pallas as pl
from jax.experimental.pallas import tpu as pltpu

EPS = 1e-6
ROPE_THETA = 10000.0

A_HEADS = 8
A_HEAD_DIM = 64
MOBA_BLOCK = 256
MOBA_TOPK = 3

B_HEADS = 8
B_NOPE = 64
B_ROPE = 32
B_QK = B_NOPE + B_ROPE
B_V = 64
B_Q_RANK = 384
B_KV_RANK = 256

LANES = 128
HEAD_SLOT = 128
ROW_TILE = 512
FF_CHUNK = 1024
K_TILE = MOBA_BLOCK
OWN_BLOCKS = 2
Q_TILE = OWN_BLOCKS * MOBA_BLOCK
HEADS_PER_STEP = 8
MASK_VALUE = -1e30
LOG2_E = 1.4426950408889634
SUM_ROWS = 16
VMEM_LIMIT_BYTES = 56 * 1024 * 1024

F32 = jnp.float32
BF16 = jnp.bfloat16


def _rms(x, g):
    return x * lax.rsqrt(jnp.mean(x * x, axis=-1, keepdims=True) + EPS) * g


def _rope_lanes(x, cos, sin_signed):
    groups = []
    for g in range(x.shape[1] // LANES):
        xg = x[:, g * LANES:(g + 1) * LANES]
        groups.append(xg * cos + pltpu.roll(xg, LANES // 2, axis=1) * sin_signed)
    return groups


def _store_transposed(dst_ref, row0, t):
    rows, width = t.shape[0], dst_ref.shape[2]
    for tile in range(ROW_TILE // width):
        dst_ref[tile, row0:row0 + rows, :] = t[:, tile * width:(tile + 1) * width].astype(dst_ref.dtype)


def _pre_kernel(x_ref, gmix_ref, cos_a_ref, sin_a_ref, cos_b_ref, sin_b_ref,
                w_qkv_ref, w_cq_ref, w_ckv_ref, w_kr_ref, w_gate_ref, b_gate_ref,
                g_cq_ref, w_qup_ref, g_ckv_ref, w_kup_ref, w_vup_ref,
                qa_t_ref, ka_ref, va_t_ref, kmean_ref, gate_ref, qb_t_ref, kb_ref, vb_t_ref):
    a_width = A_HEADS * A_HEAD_DIM
    h = _rms(x_ref[...], gmix_ref[...]).astype(BF16)

    qkv = jnp.dot(h, w_qkv_ref[...], preferred_element_type=F32)
    cos_a, sin_a = cos_a_ref[...], sin_a_ref[...]
    q_groups = _rope_lanes(qkv[:, :a_width], cos_a, sin_a)
    k_groups = _rope_lanes(qkv[:, a_width:2 * a_width], cos_a, sin_a)
    first_head = (lax.broadcasted_iota(jnp.int32, (LANES, ROW_TILE), 0) % A_HEAD_DIM) < A_HEAD_DIM // 2
    for g, qg in enumerate(q_groups):
        t = (qg * (A_HEAD_DIM ** -0.5 * LOG2_E)).T
        _store_transposed(qa_t_ref, (2 * g) * HEAD_SLOT, jnp.where(first_head, t, 0.0))
        _store_transposed(qa_t_ref, (2 * g + 1) * HEAD_SLOT, jnp.where(first_head, 0.0, t))
    for g, kg in enumerate(k_groups):
        ka_ref[:, g * LANES:(g + 1) * LANES] = kg.astype(BF16)
        for blk in range(ROW_TILE // MOBA_BLOCK):
            kmean_ref[0, blk:blk + 1, g * LANES:(g + 1) * LANES] = jnp.mean(
                kg[blk * MOBA_BLOCK:(blk + 1) * MOBA_BLOCK], axis=0, keepdims=True)
    for g in range(a_width // LANES):
        vg = qkv[:, 2 * a_width + g * LANES:2 * a_width + (g + 1) * LANES]
        _store_transposed(va_t_ref, g * LANES, vg.T)

    gate = jnp.dot(h, w_gate_ref[...], preferred_element_type=F32) + b_gate_ref[...]
    gate_ref[...] = jax.nn.sigmoid(gate).astype(BF16)

    cos_b, sin_b = cos_b_ref[...], sin_b_ref[...]
    cq = jnp.dot(h, w_cq_ref[...], preferred_element_type=F32)
    cqn = _rms(cq, g_cq_ref[...]).astype(BF16)
    qb = jnp.dot(cqn, w_qup_ref[...], preferred_element_type=F32)
    for hd, qg in enumerate(_rope_lanes(qb, cos_b, sin_b)):
        _store_transposed(qb_t_ref, hd * HEAD_SLOT, (qg * (B_QK ** -0.5 * LOG2_E)).T)
    ckv = jnp.dot(h, w_ckv_ref[...], preferred_element_type=F32)
    ckvn = _rms(ckv, g_ckv_ref[...]).astype(BF16)
    k_nope = jnp.dot(ckvn, w_kup_ref[...], preferred_element_type=F32)
    kr = jnp.dot(h, w_kr_ref[...], preferred_element_type=F32)
    kr = _rope_lanes(kr, cos_b, sin_b)[0]
    for hd in range(B_HEADS):
        kb_ref[:, hd * HEAD_SLOT:(hd + 1) * HEAD_SLOT] = (
            k_nope[:, hd * HEAD_SLOT:(hd + 1) * HEAD_SLOT] + kr).astype(BF16)
    vb = jnp.dot(ckvn, w_vup_ref[...], preferred_element_type=F32)
    for g in range(B_HEADS * B_V // LANES):
        _store_transposed(vb_t_ref, g * LANES, vb[:, g * LANES:(g + 1) * LANES].T)


def _attn_kernel(*refs, moba):
    if moba:
        q_t_ref, k_ref, v_t_ref, kmean_ref, o_t_ref = refs[:5]
        s_scr, smax_scr, p_scr, alpha_scr, m_scr, acc_scr, bias_scr, sbias_scr = refs[5:]
    else:
        q_t_ref, k_ref, v_t_ref, o_t_ref = refs[:4]
        s_scr, smax_scr, p_scr, alpha_scr, m_scr, acc_scr = refs[4:]
    own_lo = OWN_BLOCKS * pl.program_id(2)
    heads = HEADS_PER_STEP
    vdim = v_t_ref.shape[1] // heads
    qry_pos = lax.broadcasted_iota(jnp.int32, (1, Q_TILE), 1)

    def key_lanes(hd):
        g = hd // 2 if moba else hd
        return slice(g * LANES, (g + 1) * LANES)

    def q_head(hd, first=0):
        rows = slice(hd * HEAD_SLOT, (hd + 1) * HEAD_SLOT)
        return jnp.concatenate([q_t_ref[u, rows, :] for u in range(first, OWN_BLOCKS)], axis=1)

    if moba:
        nblk = kmean_ref.shape[0]
        blk = lax.broadcasted_iota(jnp.int32, (nblk, Q_TILE), 0).astype(F32)
        own = (own_lo + qry_pos // MOBA_BLOCK).astype(F32)
        past = blk < own
        for hd in range(heads):
            gate = jnp.dot(kmean_ref[:, key_lanes(hd)].astype(BF16), q_head(hd),
                           preferred_element_type=F32)
            gate = jnp.where(past, gate, -jnp.inf)
            bias = jnp.full(gate.shape, MASK_VALUE, F32)
            for _ in range(MOBA_TOPK):
                best = jnp.max(gate, axis=0, keepdims=True)
                first = jnp.min(jnp.where(gate == best, blk, float(nblk)), axis=0, keepdims=True)
                pick = blk == first
                bias = jnp.where(pick, 0.0, bias)
                gate = jnp.where(pick, -jnp.inf, gate)
            bias_scr[hd] = jnp.where(past, bias, MASK_VALUE)


    def qk_stage(j, first=0, own=False):
        start = pl.multiple_of(j * K_TILE, K_TILE)
        k_blk = k_ref[pl.ds(start, K_TILE), :]
        cols = slice(first * MOBA_BLOCK, Q_TILE)
        for hd in range(heads):
            s = jnp.dot(k_blk[:, key_lanes(hd)], q_head(hd, first), preferred_element_type=F32)
            if own:
                key_pos = lax.broadcasted_iota(jnp.int32, s.shape, 0)
                s = jnp.where(key_pos <= qry_pos[:, :s.shape[1]], s, MASK_VALUE)
            s_scr[hd, :, cols] = s
            smax = jnp.max(s, axis=0, keepdims=True)
            if moba:
                bias = bias_scr[hd, pl.ds(j, 1), cols]
                if own:
                    bias = jnp.where(qry_pos[:, :s.shape[1]] >= MOBA_BLOCK, bias, 0.0)
                sbias_scr[hd, :, cols] = bias
                smax = smax + bias
            smax_scr[hd, :, cols] = smax

    ones_rows = jnp.ones((SUM_ROWS, K_TILE), BF16)

    def softmax_stage(first=0):
        cols = slice(first * MOBA_BLOCK, Q_TILE)
        for hd in range(heads):
            m_old = m_scr[hd, :, cols]
            m_new = jnp.maximum(m_old, smax_scr[hd, :, cols])
            shift = m_new - sbias_scr[hd, :, cols] if moba else m_new
            p_scr[hd, :, cols] = jnp.exp2(s_scr[hd, :, cols] - shift).astype(BF16)
            alpha_scr[hd, :, cols] = jnp.exp2(m_old - m_new)
            m_scr[hd, :, cols] = m_new

    def pv_stage(j, first=0):
        v_t = v_t_ref[j]
        cols = slice(first * MOBA_BLOCK, Q_TILE)
        for hd in range(heads):
            v_ext = jnp.concatenate([v_t[hd * vdim:(hd + 1) * vdim], ones_rows], axis=0)
            acc_scr[hd, :, cols] = alpha_scr[hd, :, cols] * acc_scr[hd, :, cols] + jnp.dot(
                v_ext, p_scr[hd, :, cols], preferred_element_type=F32)

    def visited_block(v):
        return jnp.where(v < OWN_BLOCKS, own_lo + (OWN_BLOCKS - 1) - v, v - OWN_BLOCKS)

    m_scr[...] = jnp.full(m_scr.shape, MASK_VALUE, F32)
    acc_scr[...] = jnp.zeros(acc_scr.shape, F32)
    p_scr[:, :, :MOBA_BLOCK] = jnp.zeros(p_scr.shape[:2] + (MOBA_BLOCK,), BF16)
    alpha_scr[:, :, :MOBA_BLOCK] = jnp.ones(alpha_scr.shape[:2] + (MOBA_BLOCK,), F32)
    qk_stage(own_lo + OWN_BLOCKS - 1, OWN_BLOCKS - 1, own=True)
    for v in range(OWN_BLOCKS - 1):
        u = OWN_BLOCKS - 1 - v
        if v > 0:
            pv_stage(own_lo + u + 1, u + 1)
        softmax_stage(u)
        qk_stage(own_lo + u - 1, u - 1, own=True)

    def body(t, carry):
        pv_stage(visited_block(t - 1))
        softmax_stage()
        qk_stage(t + 1 - OWN_BLOCKS)
        return carry

    last = own_lo + OWN_BLOCKS - 1
    lax.fori_loop(OWN_BLOCKS - 1, last, body, 0)
    pv_stage(visited_block(last - 1))
    softmax_stage()
    pv_stage(visited_block(last))

    for hd in range(heads):
        acc = acc_scr[hd]
        o = (acc[:vdim] / acc[vdim:vdim + 1]).astype(o_t_ref.dtype)
        for u in range(OWN_BLOCKS):
            o_t_ref[u, hd * vdim:(hd + 1) * vdim, :] = o[:, u * MOBA_BLOCK:(u + 1) * MOBA_BLOCK]


def _post_kernel(x_ref, oa_t_ref, ob_t_ref, gate_ref, w_oa_ref, w_ob_ref, w_out_ref,
                 g_mlp_ref, w_ff1_ref, w_ff2_ref, g_final_ref, out_ref, *, ff_chunk, final_norm):
    d_model = x_ref.shape[1]
    contract_rows = (((0,), (0,)), ((), ()))
    ya, yb = [], []
    for qt in range(ROW_TILE // K_TILE):
        ya.append(lax.dot_general(oa_t_ref[qt], w_oa_ref[...], contract_rows, preferred_element_type=F32))
        yb.append(lax.dot_general(ob_t_ref[qt], w_ob_ref[...], contract_rows, preferred_element_type=F32))
    ya = ya[0] if len(ya) == 1 else jnp.concatenate(ya, axis=0)
    yb = yb[0] if len(yb) == 1 else jnp.concatenate(yb, axis=0)
    mixed = gate_ref[:, :d_model].astype(F32) * ya + gate_ref[:, d_model:].astype(F32) * yb
    x1 = x_ref[...] + jnp.dot(mixed.astype(BF16), w_out_ref[...], preferred_element_type=F32)

    h2 = _rms(x1, g_mlp_ref[...]).astype(BF16)
    mlp = jnp.zeros_like(x1)
    for c in range(w_ff1_ref.shape[1] // ff_chunk):
        u = jnp.maximum(jnp.dot(h2, w_ff1_ref[:, c * ff_chunk:(c + 1) * ff_chunk],
                                preferred_element_type=F32), 0.0)
        mlp = mlp + jnp.dot((u * u).astype(BF16), w_ff2_ref[c * ff_chunk:(c + 1) * ff_chunk, :],
                            preferred_element_type=F32)
    x2 = x1 + mlp
    out_ref[...] = _rms(x2, g_final_ref[...]) if final_norm else x2


def _resident(shape):
    return pl.BlockSpec(shape, lambda *_: (0,) * len(shape), pipeline_mode=pl.Buffered(1))


def _rope_tables(seq, dim):
    half = dim // 2
    inv_freq = np.power(ROPE_THETA, -np.arange(half, dtype=np.float64) / half)
    ang = np.arange(seq, dtype=np.float64)[:, None] * inv_freq[None, :]
    return np.cos(ang).astype(np.float32), np.sin(ang).astype(np.float32)


def _moba_group_columns():
    half = A_HEAD_DIM // 2
    order = []
    for pair in range(A_HEADS // 2):
        h0, h1 = 2 * pair * A_HEAD_DIM, (2 * pair + 1) * A_HEAD_DIM
        for start in (h0, h1, h0 + half, h1 + half):
            order.extend(range(start, start + half))
    return jnp.asarray(order, jnp.int32)


def _mla_slot_columns(nope, rope, xp=jnp):
    r = B_ROPE // 2
    split = LANES // 2 - r
    pad = xp.zeros(nope.shape[:-1] + (HEAD_SLOT - B_QK,), nope.dtype)
    return xp.concatenate([rope[..., :r], nope[..., :split], rope[..., r:], nope[..., split:], pad], axis=-1)


def _params(*semantics):
    return pltpu.CompilerParams(dimension_semantics=semantics, vmem_limit_bytes=VMEM_LIMIT_BYTES)


def _pre_call(x2, seq, g_mix, w_in, b_gate, g_cq, w_q_up, g_ckv, w_kv_up):
    rows, d_model = x2.shape
    a_width = A_HEADS * A_HEAD_DIM
    o_q, o_k, o_v = 0, a_width, 2 * a_width
    o_cq = 3 * a_width
    o_ckv = o_cq + B_Q_RANK
    o_kr = o_ckv + B_KV_RANK
    o_gate = o_kr + B_ROPE
    w_in = w_in.astype(BF16)
    cols = _moba_group_columns()
    w_qkv = jnp.concatenate([w_in[:, o_q:o_k][:, cols], w_in[:, o_k:o_v][:, cols], w_in[:, o_v:o_cq]], axis=1)
    w_cq = w_in[:, o_cq:o_ckv]
    w_ckv = w_in[:, o_ckv:o_kr]
    w_kr = _mla_slot_columns(jnp.zeros((d_model, B_NOPE), BF16), w_in[:, o_kr:o_gate])
    w_gate = w_in[:, o_gate:]
    w_q = w_q_up.astype(BF16).reshape(B_Q_RANK, B_HEADS, B_QK)
    w_qup = _mla_slot_columns(w_q[..., :B_NOPE], w_q[..., B_NOPE:]).reshape(B_Q_RANK, B_HEADS * HEAD_SLOT)
    w_kv = w_kv_up.astype(BF16).reshape(B_KV_RANK, B_HEADS, B_NOPE + B_V)
    w_kup = _mla_slot_columns(w_kv[..., :B_NOPE], jnp.zeros((B_KV_RANK, B_HEADS, B_ROPE), BF16)).reshape(
        B_KV_RANK, B_HEADS * HEAD_SLOT)
    w_vup = w_kv[:, :, B_NOPE:].reshape(B_KV_RANK, B_HEADS * B_V)

    cos_a, sin_a = _rope_tables(seq, A_HEAD_DIM)
    cos_a = np.tile(cos_a, (1, LANES // (A_HEAD_DIM // 2)))
    sin_a = np.concatenate([-sin_a, -sin_a, sin_a, sin_a], axis=1)
    cos_b, sin_b = _rope_tables(seq, B_ROPE)
    cos_b = _mla_slot_columns(np.ones((seq, B_NOPE), np.float32), np.concatenate([cos_b, cos_b], axis=1), np)
    sin_b = _mla_slot_columns(np.zeros((seq, B_NOPE), np.float32), np.concatenate([-sin_b, sin_b], axis=1), np)

    n_tiles = rows // ROW_TILE
    seq_tiles = seq // ROW_TILE
    row_spec = lambda w: pl.BlockSpec((ROW_TILE, w), lambda r: (r, 0))
    table_spec = pl.BlockSpec((ROW_TILE, LANES), lambda r: (r % seq_tiles, 0))
    t_spec = lambda h, width: pl.BlockSpec((ROW_TILE // width, h, width), lambda r: (r, 0, 0))
    t_shape = lambda h, width: jax.ShapeDtypeStruct((rows // width, h, width), BF16)
    vec = lambda a: a.reshape(1, -1).astype(F32)
    operands = [
        (x2, row_spec(d_model)), (vec(g_mix), _resident((1, d_model))),
        (cos_a, table_spec), (sin_a, table_spec), (cos_b, table_spec), (sin_b, table_spec),
        (w_qkv, _resident(w_qkv.shape)), (w_cq, _resident(w_cq.shape)), (w_ckv, _resident(w_ckv.shape)),
        (w_kr, _resident(w_kr.shape)), (w_gate, _resident(w_gate.shape)),
        (vec(b_gate), _resident((1, 2 * d_model))),
        (vec(g_cq), _resident((1, B_Q_RANK))), (w_qup, _resident(w_qup.shape)),
        (vec(g_ckv), _resident((1, B_KV_RANK))), (w_kup, _resident(w_kup.shape)),
        (w_vup, _resident(w_vup.shape)),
    ]
    out_shape = [
        t_shape(A_HEADS * HEAD_SLOT, K_TILE),
        jax.ShapeDtypeStruct((rows, a_width), BF16),
        t_shape(a_width, K_TILE),
        jax.ShapeDtypeStruct((n_tiles, ROW_TILE // MOBA_BLOCK, a_width), F32),
        jax.ShapeDtypeStruct((rows, 2 * d_model), BF16),
        t_shape(B_HEADS * HEAD_SLOT, K_TILE),
        jax.ShapeDtypeStruct((rows, B_HEADS * HEAD_SLOT), BF16),
        t_shape(B_HEADS * B_V, K_TILE),
    ]
    out_specs = [
        t_spec(A_HEADS * HEAD_SLOT, K_TILE), row_spec(a_width), t_spec(a_width, K_TILE),
        pl.BlockSpec((1, ROW_TILE // MOBA_BLOCK, a_width), lambda r: (r, 0, 0)),
        row_spec(2 * d_model), t_spec(B_HEADS * HEAD_SLOT, K_TILE), row_spec(B_HEADS * HEAD_SLOT),
        t_spec(B_HEADS * B_V, K_TILE),
    ]
    return pl.pallas_call(
        _pre_kernel,
        grid=(n_tiles,),
        in_specs=[s for _, s in operands],
        out_specs=out_specs,
        out_shape=out_shape,
        compiler_params=_params("parallel"),
        name="pre_proj",
    )(*[a for a, _ in operands])


def _attn_call(q_t, k, v_t, kmean, *, batch, seq, moba):
    nblk = seq // K_TILE
    n_q = seq // Q_TILE
    heads = q_t.shape[1] // HEAD_SLOT
    groups = heads // HEADS_PER_STEP
    vdim = v_t.shape[2] // heads
    key_lanes = k.shape[2] // groups
    kv_mode = None if (moba and OWN_BLOCKS <= 2) else pl.Buffered(1)
    in_specs = [
        pl.BlockSpec((OWN_BLOCKS, HEADS_PER_STEP * HEAD_SLOT, K_TILE), lambda b, g, i: (b * n_q + i, g, 0)),
        pl.BlockSpec((None, seq, key_lanes), lambda b, g, i: (b, 0, g), pipeline_mode=kv_mode),
        pl.BlockSpec((None, nblk, HEADS_PER_STEP * vdim, K_TILE), lambda b, g, i: (b, 0, g, 0),
                     pipeline_mode=kv_mode),
    ]
    operands = [q_t, k, v_t]
    stat = pltpu.VMEM((HEADS_PER_STEP, 1, Q_TILE), F32)
    scratch = [
        pltpu.VMEM((HEADS_PER_STEP, K_TILE, Q_TILE), F32),
        stat,
        pltpu.VMEM((HEADS_PER_STEP, K_TILE, Q_TILE), BF16),
        stat,
        stat,
        pltpu.VMEM((HEADS_PER_STEP, vdim + SUM_ROWS, Q_TILE), F32),
    ]
    if moba:
        in_specs.append(pl.BlockSpec((None, nblk, key_lanes), lambda b, g, i: (b, 0, g)))
        operands.append(kmean)
        scratch.append(pltpu.VMEM((HEADS_PER_STEP, nblk, Q_TILE), F32))
        scratch.append(stat)
    return pl.pallas_call(
        functools.partial(_attn_kernel, moba=moba),
        grid=(batch, groups, n_q),
        in_specs=in_specs,
        out_specs=pl.BlockSpec((OWN_BLOCKS, HEADS_PER_STEP * vdim, K_TILE), lambda b, g, i: (b * n_q + i, g, 0)),
        out_shape=jax.ShapeDtypeStruct((batch * nblk, heads * vdim, K_TILE), BF16),
        scratch_shapes=scratch,
        compiler_params=_params("parallel", "parallel", "arbitrary"),
        name="moba_attn" if moba else "mla_attn",
    )(*operands)


def _post_call(x2, oa_t, ob_t, gates, w_o_a, w_o_b, w_out, g_mlp, w_ff1, w_ff2, g_final, final_norm):
    rows, d_model = x2.shape
    row_spec = lambda w: pl.BlockSpec((ROW_TILE, w), lambda r: (r, 0))
    t_spec = lambda a: pl.BlockSpec((ROW_TILE // K_TILE,) + a.shape[1:], lambda r: (r, 0, 0))
    vec = lambda a: a.reshape(1, -1).astype(F32)
    weights = [w.astype(BF16) for w in (w_o_a, w_o_b, w_out)]
    operands = [(x2, row_spec(d_model)), (oa_t, t_spec(oa_t)), (ob_t, t_spec(ob_t)),
                (gates, row_spec(2 * d_model))]
    operands += [(w, _resident(w.shape)) for w in weights]
    operands += [(vec(g_mlp), _resident((1, d_model))),
                 (w_ff1.astype(BF16), _resident(w_ff1.shape)), (w_ff2.astype(BF16), _resident(w_ff2.shape)),
                 (vec(g_final), _resident((1, d_model)))]
    return pl.pallas_call(
        functools.partial(_post_kernel, ff_chunk=FF_CHUNK, final_norm=final_norm),
        grid=(rows // ROW_TILE,),
        in_specs=[s for _, s in operands],
        out_specs=row_spec(d_model),
        out_shape=jax.ShapeDtypeStruct((rows, d_model), F32),
        compiler_params=_params("parallel"),
        name="post_mlp",
    )(*[a for a, _ in operands])


def kernel(x, g_mix, w_in, b_gate, g_cq, w_q_up, g_ckv, w_kv_up, w_o_a, w_o_b, w_out, g_mlp, w_ff1,
           w_ff2, g_final):
    batch, seq, d_model = x.shape
    depth = w_in.shape[0]
    assert seq % ROW_TILE == 0 and ROW_TILE % MOBA_BLOCK == 0 and seq % Q_TILE == 0 and OWN_BLOCKS >= 2
    assert A_HEADS % HEADS_PER_STEP == 0 and B_HEADS % HEADS_PER_STEP == 0 and HEADS_PER_STEP % 2 == 0
    nblk = seq // K_TILE
    x2 = x.reshape(batch * seq, d_model)
    for l in range(depth):
        qa_t, ka, va_t, kmean, gates, qb_t, kb, vb_t = _pre_call(
            x2, seq, g_mix[l], w_in[l], b_gate[l], g_cq[l], w_q_up[l], g_ckv[l], w_kv_up[l])
        oa_t = _attn_call(qa_t, ka.reshape(batch, seq, -1), va_t.reshape(batch, nblk, -1, K_TILE),
                          kmean.reshape(batch, nblk, -1), batch=batch, seq=seq, moba=True)
        ob_t = _attn_call(qb_t, kb.reshape(batch, seq, -1), vb_t.reshape(batch, nblk, -1, K_TILE),
                          None, batch=batch, seq=seq, moba=False)
        x2 = _post_call(x2, oa_t, ob_t, gates, w_o_a[l], w_o_b[l], w_out[l], g_mlp[l], w_ff1[l],
                        w_ff2[l], g_final, final_norm=(l == depth - 1))
    return x2.reshape(batch, seq, d_model)
```

```python
import functools

import jax
import jax.numpy as jnp
import numpy as np
from jax import lax
from jax.experimental import pallas as pl
from jax.experimental.pallas import tpu as pltpu

EPS = 1e-6
ROPE_THETA = 10000.0

A_HEADS = 8
A_HEAD_DIM = 64
MOBA_BLOCK = 256
MOBA_TOPK = 3

B_HEADS = 8
B_NOPE = 64
B_ROPE = 32
B_QK = B_NOPE + B_ROPE
B_V = 64
B_Q_RANK = 384
B_KV_RANK = 256

LANES = 128
HEAD_SLOT = 128
ROW_TILE = 512
FF_CHUNK = 1024
K_TILE = MOBA_BLOCK
OWN_BLOCKS = 8
Q_TILE = OWN_BLOCKS * MOBA_BLOCK
HEADS_PER_STEP = 4
MASK_VALUE = -1e30
LOG2_E = 1.4426950408889634
SUM_ROWS = 16
VMEM_LIMIT_BYTES = 56 * 1024 * 1024

F32 = jnp.float32
BF16 = jnp.bfloat16


def _rms(x, g):
    return x * lax.rsqrt(jnp.mean(x * x, axis=-1, keepdims=True) + EPS) * g


def _rope_lanes(x, cos, sin_signed):
    groups = []
    for g in range(x.shape[1] // LANES):
        xg = x[:, g * LANES:(g + 1) * LANES]
        groups.append(xg * cos + pltpu.roll(xg, LANES // 2, axis=1) * sin_signed)
    return groups


def _store_transposed(dst_ref, row0, t):
    rows, width = t.shape[0], dst_ref.shape[2]
    for tile in range(ROW_TILE // width):
        dst_ref[tile, row0:row0 + rows, :] = t[:, tile * width:(tile + 1) * width].astype(dst_ref.dtype)


def _pre_kernel(x_ref, gmix_ref, cos_a_ref, sin_a_ref, cos_b_ref, sin_b_ref,
                w_qkv_ref, w_cq_ref, w_ckv_ref, w_kr_ref, w_gate_ref, b_gate_ref,
                g_cq_ref, w_qup_ref, g_ckv_ref, w_kup_ref, w_vup_ref,
                qa_t_ref, ka_ref, va_t_ref, kmean_ref, gate_ref, qb_t_ref, kb_ref, vb_t_ref):
    a_width = A_HEADS * A_HEAD_DIM
    h = _rms(x_ref[...], gmix_ref[...]).astype(BF16)

    qkv = jnp.dot(h, w_qkv_ref[...], preferred_element_type=F32)
    cos_a, sin_a = cos_a_ref[...], sin_a_ref[...]
    q_groups = _rope_lanes(qkv[:, :a_width], cos_a, sin_a)
    k_groups = _rope_lanes(qkv[:, a_width:2 * a_width], cos_a, sin_a)
    first_head = (lax.broadcasted_iota(jnp.int32, (LANES, ROW_TILE), 0) % A_HEAD_DIM) < A_HEAD_DIM // 2
    for g, qg in enumerate(q_groups):
        t = (qg * (A_HEAD_DIM ** -0.5 * LOG2_E)).T
        _store_transposed(qa_t_ref, (2 * g) * HEAD_SLOT, jnp.where(first_head, t, 0.0))
        _store_transposed(qa_t_ref, (2 * g + 1) * HEAD_SLOT, jnp.where(first_head, 0.0, t))
    for g, kg in enumerate(k_groups):
        ka_ref[:, g * LANES:(g + 1) * LANES] = kg.astype(BF16)
        for blk in range(ROW_TILE // MOBA_BLOCK):
            kmean_ref[0, blk:blk + 1, g * LANES:(g + 1) * LANES] = jnp.mean(
                kg[blk * MOBA_BLOCK:(blk + 1) * MOBA_BLOCK], axis=0, keepdims=True)
    for g in range(a_width // LANES):
        vg = qkv[:, 2 * a_width + g * LANES:2 * a_width + (g + 1) * LANES]
        _store_transposed(va_t_ref, g * LANES, vg.T)

    gate = jnp.dot(h, w_gate_ref[...], preferred_element_type=F32) + b_gate_ref[...]
    gate_ref[...] = jax.nn.sigmoid(gate).astype(BF16)

    cos_b, sin_b = cos_b_ref[...], sin_b_ref[...]
    cq = jnp.dot(h, w_cq_ref[...], preferred_element_type=F32)
    cqn = _rms(cq, g_cq_ref[...]).astype(BF16)
    qb = jnp.dot(cqn, w_qup_ref[...], preferred_element_type=F32)
    for hd, qg in enumerate(_rope_lanes(qb, cos_b, sin_b)):
        _store_transposed(qb_t_ref, hd * HEAD_SLOT, (qg * (B_QK ** -0.5 * LOG2_E)).T)
    ckv = jnp.dot(h, w_ckv_ref[...], preferred_element_type=F32)
    ckvn = _rms(ckv, g_ckv_ref[...]).astype(BF16)
    k_nope = jnp.dot(ckvn, w_kup_ref[...], preferred_element_type=F32)
    kr = jnp.dot(h, w_kr_ref[...], preferred_element_type=F32)
    kr = _rope_lanes(kr, cos_b, sin_b)[0]
    for hd in range(B_HEADS):
        kb_ref[:, hd * HEAD_SLOT:(hd + 1) * HEAD_SLOT] = (
            k_nope[:, hd * HEAD_SLOT:(hd + 1) * HEAD_SLOT] + kr).astype(BF16)
    vb = jnp.dot(ckvn, w_vup_ref[...], preferred_element_type=F32)
    for g in range(B_HEADS * B_V // LANES):
        _store_transposed(vb_t_ref, g * LANES, vb[:, g * LANES:(g + 1) * LANES].T)


def _attn_kernel(*refs, moba):
    if moba:
        q_t_ref, k_ref, v_t_ref, kmean_ref, o_t_ref = refs[:5]
        s_scr, smax_scr, p_scr, alpha_scr, m_scr, acc_scr, bias_scr, sbias_scr = refs[5:]
    else:
        q_t_ref, k_ref, v_t_ref, o_t_ref = refs[:4]
        s_scr, smax_scr, p_scr, alpha_scr, m_scr, acc_scr = refs[4:]
    own_lo = OWN_BLOCKS * pl.program_id(2)
    heads = HEADS_PER_STEP
    vdim = v_t_ref.shape[1] // heads
    qry_pos = lax.broadcasted_iota(jnp.int32, (1, Q_TILE), 1)

    def key_lanes(hd):
        g = hd // 2 if moba else hd
        return slice(g * LANES, (g + 1) * LANES)

    def q_head(hd, first=0):
        rows = slice(hd * HEAD_SLOT, (hd + 1) * HEAD_SLOT)
        return jnp.concatenate([q_t_ref[u, rows, :] for u in range(first, OWN_BLOCKS)], axis=1)

    if moba:
        nblk = kmean_ref.shape[0]
        blk = lax.broadcasted_iota(jnp.int32, (nblk, Q_TILE), 0).astype(F32)
        own = (own_lo + qry_pos // MOBA_BLOCK).astype(F32)
        past = blk < own
        for hd in range(heads):
            gate = jnp.dot(kmean_ref[:, key_lanes(hd)].astype(BF16), q_head(hd),
                           preferred_element_type=F32)
            gate = jnp.where(past, gate, -jnp.inf)
            bias = jnp.full(gate.shape, MASK_VALUE, F32)
            for _ in range(MOBA_TOPK):
                best = jnp.max(gate, axis=0, keepdims=True)
                first = jnp.min(jnp.where(gate == best, blk, float(nblk)), axis=0, keepdims=True)
                pick = blk == first
                bias = jnp.where(pick, 0.0, bias)
                gate = jnp.where(pick, -jnp.inf, gate)
            bias_scr[hd] = jnp.where(past, bias, MASK_VALUE)


    def qk_stage(j, first=0, own=False):
        start = pl.multiple_of(j * K_TILE, K_TILE)
        k_blk = k_ref[pl.ds(start, K_TILE), :]
        cols = slice(first * MOBA_BLOCK, Q_TILE)
        for hd in range(heads):
            s = jnp.dot(k_blk[:, key_lanes(hd)], q_head(hd, first), preferred_element_type=F32)
            if own:
                key_pos = lax.broadcasted_iota(jnp.int32, s.shape, 0)
                s = jnp.where(key_pos <= qry_pos[:, :s.shape[1]], s, MASK_VALUE)
            s_scr[hd, :, cols] = s
            smax = jnp.max(s, axis=0, keepdims=True)
            if moba:
                bias = bias_scr[hd, pl.ds(j, 1), cols]
                if own:
                    bias = jnp.where(qry_pos[:, :s.shape[1]] >= MOBA_BLOCK, bias, 0.0)
                sbias_scr[hd, :, cols] = bias
                smax = smax + bias
            smax_scr[hd, :, cols] = smax

    ones_rows = jnp.ones((SUM_ROWS, K_TILE), BF16)

    def softmax_stage(first=0):
        cols = slice(first * MOBA_BLOCK, Q_TILE)
        for hd in range(heads):
            m_old = m_scr[hd, :, cols]
            m_new = jnp.maximum(m_old, smax_scr[hd, :, cols])
            shift = m_new - sbias_scr[hd, :, cols] if moba else m_new
            p_scr[hd, :, cols] = jnp.exp2(s_scr[hd, :, cols] - shift).astype(BF16)
            alpha_scr[hd, :, cols] = jnp.exp2(m_old - m_new)
            m_scr[hd, :, cols] = m_new

    def pv_stage(j, first=0):
        v_t = v_t_ref[j]
        cols = slice(first * MOBA_BLOCK, Q_TILE)
        for hd in range(heads):
            v_ext = jnp.concatenate([v_t[hd * vdim:(hd + 1) * vdim], ones_rows], axis=0)
            acc_scr[hd, :, cols] = alpha_scr[hd, :, cols] * acc_scr[hd, :, cols] + jnp.dot(
                v_ext, p_scr[hd, :, cols], preferred_element_type=F32)

    def visited_block(v):
        return jnp.where(v < OWN_BLOCKS, own_lo + (OWN_BLOCKS - 1) - v, v - OWN_BLOCKS)

    m_scr[...] = jnp.full(m_scr.shape, MASK_VALUE, F32)
    acc_scr[...] = jnp.zeros(acc_scr.shape, F32)
    p_scr[:, :, :MOBA_BLOCK] = jnp.zeros(p_scr.shape[:2] + (MOBA_BLOCK,), BF16)
    alpha_scr[:, :, :MOBA_BLOCK] = jnp.ones(alpha_scr.shape[:2] + (MOBA_BLOCK,), F32)
    qk_stage(own_lo + OWN_BLOCKS - 1, OWN_BLOCKS - 1, own=True)
    for v in range(OWN_BLOCKS - 1):
        u = OWN_BLOCKS - 1 - v
        if v > 0:
            pv_stage(own_lo + u + 1, u + 1)
        softmax_stage(u)
        qk_stage(own_lo + u - 1, u - 1, own=True)

    def body(t, carry):
        pv_stage(visited_block(t - 1))
        softmax_stage()
        qk_stage(t + 1 - OWN_BLOCKS)
        return carry

    last = own_lo + OWN_BLOCKS - 1
    lax.fori_loop(OWN_BLOCKS - 1, last, body, 0)
    pv_stage(visited_block(last - 1))
    softmax_stage()
    pv_stage(visited_block(last))

    for hd in range(heads):
        acc = acc_scr[hd]
        o = (acc[:vdim] / acc[vdim:vdim + 1]).astype(o_t_ref.dtype)
        for u in range(OWN_BLOCKS):
            o_t_ref[u, hd * vdim:(hd + 1) * vdim, :] = o[:, u * MOBA_BLOCK:(u + 1) * MOBA_BLOCK]


def _post_kernel(x_ref, oa_t_ref, ob_t_ref, gate_ref, w_oa_ref, w_ob_ref, w_out_ref,
                 g_mlp_ref, w_ff1_ref, w_ff2_ref, g_final_ref, out_ref, *, ff_chunk, final_norm):
    d_model = x_ref.shape[1]
    contract_rows = (((0,), (0,)), ((), ()))
    ya, yb = [], []
    for qt in range(ROW_TILE // K_TILE):
        ya.append(lax.dot_general(oa_t_ref[qt], w_oa_ref[...], contract_rows, preferred_element_type=F32))
        yb.append(lax.dot_general(ob_t_ref[qt], w_ob_ref[...], contract_rows, preferred_element_type=F32))
    ya = ya[0] if len(ya) == 1 else jnp.concatenate(ya, axis=0)
    yb = yb[0] if len(yb) == 1 else jnp.concatenate(yb, axis=0)
    mixed = gate_ref[:, :d_model].astype(F32) * ya + gate_ref[:, d_model:].astype(F32) * yb
    x1 = x_ref[...] + jnp.dot(mixed.astype(BF16), w_out_ref[...], preferred_element_type=F32)

    h2 = _rms(x1, g_mlp_ref[...]).astype(BF16)
    mlp = jnp.zeros_like(x1)
    for c in range(w_ff1_ref.shape[1] // ff_chunk):
        u = jnp.maximum(jnp.dot(h2, w_ff1_ref[:, c * ff_chunk:(c + 1) * ff_chunk],
                                preferred_element_type=F32), 0.0)
        mlp = mlp + jnp.dot((u * u).astype(BF16), w_ff2_ref[c * ff_chunk:(c + 1) * ff_chunk, :],
                            preferred_element_type=F32)
    x2 = x1 + mlp
    out_ref[...] = _rms(x2, g_final_ref[...]) if final_norm else x2


def _resident(shape):
    return pl.BlockSpec(shape, lambda *_: (0,) * len(shape), pipeline_mode=pl.Buffered(1))


def _rope_tables(seq, dim):
    half = dim // 2
    inv_freq = np.power(ROPE_THETA, -np.arange(half, dtype=np.float64) / half)
    ang = np.arange(seq, dtype=np.float64)[:, None] * inv_freq[None, :]
    return np.cos(ang).astype(np.float32), np.sin(ang).astype(np.float32)


def _moba_group_columns():
    half = A_HEAD_DIM // 2
    order = []
    for pair in range(A_HEADS // 2):
        h0, h1 = 2 * pair * A_HEAD_DIM, (2 * pair + 1) * A_HEAD_DIM
        for start in (h0, h1, h0 + half, h1 + half):
            order.extend(range(start, start + half))
    return jnp.asarray(order, jnp.int32)


def _mla_slot_columns(nope, rope, xp=jnp):
    r = B_ROPE // 2
    split = LANES // 2 - r
    pad = xp.zeros(nope.shape[:-1] + (HEAD_SLOT - B_QK,), nope.dtype)
    return xp.concatenate([rope[..., :r], nope[..., :split], rope[..., r:], nope[..., split:], pad], axis=-1)


def _params(*semantics):
    return pltpu.CompilerParams(dimension_semantics=semantics, vmem_limit_bytes=VMEM_LIMIT_BYTES)


def _pre_call(x2, seq, g_mix, w_in, b_gate, g_cq, w_q_up, g_ckv, w_kv_up):
    rows, d_model = x2.shape
    a_width = A_HEADS * A_HEAD_DIM
    o_q, o_k, o_v = 0, a_width, 2 * a_width
    o_cq = 3 * a_width
    o_ckv = o_cq + B_Q_RANK
    o_kr = o_ckv + B_KV_RANK
    o_gate = o_kr + B_ROPE
    w_in = w_in.astype(BF16)
    cols = _moba_group_columns()
    w_qkv = jnp.concatenate([w_in[:, o_q:o_k][:, cols], w_in[:, o_k:o_v][:, cols], w_in[:, o_v:o_cq]], axis=1)
    w_cq = w_in[:, o_cq:o_ckv]
    w_ckv = w_in[:, o_ckv:o_kr]
    w_kr = _mla_slot_columns(jnp.zeros((d_model, B_NOPE), BF16), w_in[:, o_kr:o_gate])
    w_gate = w_in[:, o_gate:]
    w_q = w_q_up.astype(BF16).reshape(B_Q_RANK, B_HEADS, B_QK)
    w_qup = _mla_slot_columns(w_q[..., :B_NOPE], w_q[..., B_NOPE:]).reshape(B_Q_RANK, B_HEADS * HEAD_SLOT)
    w_kv = w_kv_up.astype(BF16).reshape(B_KV_RANK, B_HEADS, B_NOPE + B_V)
    w_kup = _mla_slot_columns(w_kv[..., :B_NOPE], jnp.zeros((B_KV_RANK, B_HEADS, B_ROPE), BF16)).reshape(
        B_KV_RANK, B_HEADS * HEAD_SLOT)
    w_vup = w_kv[:, :, B_NOPE:].reshape(B_KV_RANK, B_HEADS * B_V)

    cos_a, sin_a = _rope_tables(seq, A_HEAD_DIM)
    cos_a = np.tile(cos_a, (1, LANES // (A_HEAD_DIM // 2)))
    sin_a = np.concatenate([-sin_a, -sin_a, sin_a, sin_a], axis=1)
    cos_b, sin_b = _rope_tables(seq, B_ROPE)
    cos_b = _mla_slot_columns(np.ones((seq, B_NOPE), np.float32), np.concatenate([cos_b, cos_b], axis=1), np)
    sin_b = _mla_slot_columns(np.zeros((seq, B_NOPE), np.float32), np.concatenate([-sin_b, sin_b], axis=1), np)

    n_tiles = rows // ROW_TILE
    seq_tiles = seq // ROW_TILE
    row_spec = lambda w: pl.BlockSpec((ROW_TILE, w), lambda r: (r, 0))
    table_spec = pl.BlockSpec((ROW_TILE, LANES), lambda r: (r % seq_tiles, 0))
    t_spec = lambda h, width: pl.BlockSpec((ROW_TILE // width, h, width), lambda r: (r, 0, 0))
    t_shape = lambda h, width: jax.ShapeDtypeStruct((rows // width, h, width), BF16)
    vec = lambda a: a.reshape(1, -1).astype(F32)
    operands = [
        (x2, row_spec(d_model)), (vec(g_mix), _resident((1, d_model))),
        (cos_a, table_spec), (sin_a, table_spec), (cos_b, table_spec), (sin_b, table_spec),
        (w_qkv, _resident(w_qkv.shape)), (w_cq, _resident(w_cq.shape)), (w_ckv, _resident(w_ckv.shape)),
        (w_kr, _resident(w_kr.shape)), (w_gate, _resident(w_gate.shape)),
        (vec(b_gate), _resident((1, 2 * d_model))),
        (vec(g_cq), _resident((1, B_Q_RANK))), (w_qup, _resident(w_qup.shape)),
        (vec(g_ckv), _resident((1, B_KV_RANK))), (w_kup, _resident(w_kup.shape)),
        (w_vup, _resident(w_vup.shape)),
    ]
    out_shape = [
        t_shape(A_HEADS * HEAD_SLOT, K_TILE),
        jax.ShapeDtypeStruct((rows, a_width), BF16),
        t_shape(a_width, K_TILE),
        jax.ShapeDtypeStruct((n_tiles, ROW_TILE // MOBA_BLOCK, a_width), F32),
        jax.ShapeDtypeStruct((rows, 2 * d_model), BF16),
        t_shape(B_HEADS * HEAD_SLOT, K_TILE),
        jax.ShapeDtypeStruct((rows, B_HEADS * HEAD_SLOT), BF16),
        t_shape(B_HEADS * B_V, K_TILE),
    ]
    out_specs = [
        t_spec(A_HEADS * HEAD_SLOT, K_TILE), row_spec(a_width), t_spec(a_width, K_TILE),
        pl.BlockSpec((1, ROW_TILE // MOBA_BLOCK, a_width), lambda r: (r, 0, 0)),
        row_spec(2 * d_model), t_spec(B_HEADS * HEAD_SLOT, K_TILE), row_spec(B_HEADS * HEAD_SLOT),
        t_spec(B_HEADS * B_V, K_TILE),
    ]
    return pl.pallas_call(
        _pre_kernel,
        grid=(n_tiles,),
        in_specs=[s for _, s in operands],
        out_specs=out_specs,
        out_shape=out_shape,
        compiler_params=_params("parallel"),
        name="pre_proj",
    )(*[a for a, _ in operands])


def _attn_call(q_t, k, v_t, kmean, *, batch, seq, moba):
    nblk = seq // K_TILE
    n_q = seq // Q_TILE
    heads = q_t.shape[1] // HEAD_SLOT
    groups = heads // HEADS_PER_STEP
    vdim = v_t.shape[2] // heads
    key_lanes = k.shape[2] // groups
    kv_mode = None if (moba and OWN_BLOCKS <= 2) else pl.Buffered(1)
    in_specs = [
        pl.BlockSpec((OWN_BLOCKS, HEADS_PER_STEP * HEAD_SLOT, K_TILE), lambda b, g, i: (b * n_q + i, g, 0)),
        pl.BlockSpec((None, seq, key_lanes), lambda b, g, i: (b, 0, g), pipeline_mode=kv_mode),
        pl.BlockSpec((None, nblk, HEADS_PER_STEP * vdim, K_TILE), lambda b, g, i: (b, 0, g, 0),
                     pipeline_mode=kv_mode),
    ]
    operands = [q_t, k, v_t]
    stat = pltpu.VMEM((HEADS_PER_STEP, 1, Q_TILE), F32)
    scratch = [
        pltpu.VMEM((HEADS_PER_STEP, K_TILE, Q_TILE), F32),
        stat,
        pltpu.VMEM((HEADS_PER_STEP, K_TILE, Q_TILE), BF16),
        stat,
        stat,
        pltpu.VMEM((HEADS_PER_STEP, vdim + SUM_ROWS, Q_TILE), F32),
    ]
    if moba:
        in_specs.append(pl.BlockSpec((None, nblk, key_lanes), lambda b, g, i: (b, 0, g)))
        operands.append(kmean)
        scratch.append(pltpu.VMEM((HEADS_PER_STEP, nblk, Q_TILE), F32))
        scratch.append(stat)
    return pl.pallas_call(
        functools.partial(_attn_kernel, moba=moba),
        grid=(batch, groups, n_q),
        in_specs=in_specs,
        out_specs=pl.BlockSpec((OWN_BLOCKS, HEADS_PER_STEP * vdim, K_TILE), lambda b, g, i: (b * n_q + i, g, 0)),
        out_shape=jax.ShapeDtypeStruct((batch * nblk, heads * vdim, K_TILE), BF16),
        scratch_shapes=scratch,
        compiler_params=_params("parallel", "parallel", "arbitrary"),
        name="moba_attn" if moba else "mla_attn",
    )(*operands)


def _post_call(x2, oa_t, ob_t, gates, w_o_a, w_o_b, w_out, g_mlp, w_ff1, w_ff2, g_final, final_norm):
    rows, d_model = x2.shape
    row_spec = lambda w: pl.BlockSpec((ROW_TILE, w), lambda r: (r, 0))
    t_spec = lambda a: pl.BlockSpec((ROW_TILE // K_TILE,) + a.shape[1:], lambda r: (r, 0, 0))
    vec = lambda a: a.reshape(1, -1).astype(F32)
    weights = [w.astype(BF16) for w in (w_o_a, w_o_b, w_out)]
    operands = [(x2, row_spec(d_model)), (oa_t, t_spec(oa_t)), (ob_t, t_spec(ob_t)),
                (gates, row_spec(2 * d_model))]
    operands += [(w, _resident(w.shape)) for w in weights]
    operands += [(vec(g_mlp), _resident((1, d_model))),
                 (w_ff1.astype(BF16), _resident(w_ff1.shape)), (w_ff2.astype(BF16), _resident(w_ff2.shape)),
                 (vec(g_final), _resident((1, d_model)))]
    return pl.pallas_call(
        functools.partial(_post_kernel, ff_chunk=FF_CHUNK, final_norm=final_norm),
        grid=(rows // ROW_TILE,),
        in_specs=[s for _, s in operands],
        out_specs=row_spec(d_model),
        out_shape=jax.ShapeDtypeStruct((rows, d_model), F32),
        compiler_params=_params("parallel"),
        name="post_mlp",
    )(*[a for a, _ in operands])


def kernel(x, g_mix, w_in, b_gate, g_cq, w_q_up, g_ckv, w_kv_up, w_o_a, w_o_b, w_out, g_mlp, w_ff1,
           w_ff2, g_final):
    batch, seq, d_model = x.shape
    depth = w_in.shape[0]
    assert seq % ROW_TILE == 0 and ROW_TILE % MOBA_BLOCK == 0 and seq % Q_TILE == 0 and OWN_BLOCKS >= 2
    assert A_HEADS % HEADS_PER_STEP == 0 and B_HEADS % HEADS_PER_STEP == 0 and HEADS_PER_STEP % 2 == 0
    nblk = seq // K_TILE
    x2 = x.reshape(batch * seq, d_model)
    for l in range(depth):
        qa_t, ka, va_t, kmean, gates, qb_t, kb, vb_t = _pre_call(
            x2, seq, g_mix[l], w_in[l], b_gate[l], g_cq[l], w_q_up[l], g_ckv[l], w_kv_up[l])
        oa_t = _attn_call(qa_t, ka.reshape(batch, seq, -1), va_t.reshape(batch, nblk, -1, K_TILE),
                          kmean.reshape(batch, nblk, -1), batch=batch, seq=seq, moba=True)
        ob_t = _attn_call(qb_t, kb.reshape(batch, seq, -1), vb_t.reshape(batch, nblk, -1, K_TILE),
                          None, batch=batch, seq=seq, moba=False)
        x2 = _post_call(x2, oa_t, ob_t, gates, w_o_a[l], w_o_b[l], w_out[l], g_mlp[l], w_ff1[l],
                        w_ff2[l], g_final, final_norm=(l == depth - 1))
    return x2.reshape(batch, seq, d_model)
```

```python
import functools

import jax
import jax.numpy as jnp
import numpy as np
from jax import lax
from jax.experimental import pallas as pl
from jax.experimental.pallas import tpu as pltpu

EPS = 1e-6
ROPE_THETA = 10000.0

A_HEADS = 8
A_HEAD_DIM = 64
MOBA_BLOCK = 256
MOBA_TOPK = 3

B_HEADS = 8
B_NOPE = 64
B_ROPE = 32
B_QK = B_NOPE + B_ROPE
B_V = 64
B_Q_RANK = 384
B_KV_RANK = 256

LANES = 128
HEAD_SLOT = 128
ROW_TILE = 512
FF_CHUNK = 1024
K_TILE = MOBA_BLOCK
OWN_BLOCKS = 8
Q_TILE = OWN_BLOCKS * MOBA_BLOCK
HEADS_PER_STEP = 4
MASK_VALUE = -1e30
LOG2_E = 1.4426950408889634
SUM_ROWS = 16
VMEM_LIMIT_BYTES = 56 * 1024 * 1024

F32 = jnp.float32
BF16 = jnp.bfloat16


def _rms(x, g):
    return x * lax.rsqrt(jnp.mean(x * x, axis=-1, keepdims=True) + EPS) * g


def _rope_lanes(x, cos, sin_signed):
    groups = []
    for g in range(x.shape[1] // LANES):
        xg = x[:, g * LANES:(g + 1) * LANES]
        groups.append(xg * cos + pltpu.roll(xg, LANES // 2, axis=1) * sin_signed)
    return groups


def _store_transposed(dst_ref, row0, t):
    rows, width = t.shape[0], dst_ref.shape[2]
    for tile in range(ROW_TILE // width):
        dst_ref[tile, row0:row0 + rows, :] = t[:, tile * width:(tile + 1) * width].astype(dst_ref.dtype)


def _pre_kernel(x_ref, gmix_ref, cos_a_ref, sin_a_ref, cos_b_ref, sin_b_ref,
                w_qkv_ref, w_cq_ref, w_ckv_ref, w_kr_ref, w_gate_ref, b_gate_ref,
                g_cq_ref, w_qup_ref, g_ckv_ref, w_kup_ref, w_vup_ref,
                qa_t_ref, ka_ref, va_t_ref, kmean_ref, gate_ref, qb_t_ref, kb_ref, vb_t_ref):
    a_width = A_HEADS * A_HEAD_DIM
    h = _rms(x_ref[...], gmix_ref[...]).astype(BF16)

    qkv = jnp.dot(h, w_qkv_ref[...], preferred_element_type=F32)
    cos_a, sin_a = cos_a_ref[...], sin_a_ref[...]
    q_groups = _rope_lanes(qkv[:, :a_width], cos_a, sin_a)
    k_groups = _rope_lanes(qkv[:, a_width:2 * a_width], cos_a, sin_a)
    first_head = (lax.broadcasted_iota(jnp.int32, (LANES, ROW_TILE), 0) % A_HEAD_DIM) < A_HEAD_DIM // 2
    for g, qg in enumerate(q_groups):
        t = (qg * (A_HEAD_DIM ** -0.5 * LOG2_E)).T
        _store_transposed(qa_t_ref, (2 * g) * HEAD_SLOT, jnp.where(first_head, t, 0.0))
        _store_transposed(qa_t_ref, (2 * g + 1) * HEAD_SLOT, jnp.where(first_head, 0.0, t))
    for g, kg in enumerate(k_groups):
        ka_ref[:, g * LANES:(g + 1) * LANES] = kg.astype(BF16)
        for blk in range(ROW_TILE // MOBA_BLOCK):
            kmean_ref[0, blk:blk + 1, g * LANES:(g + 1) * LANES] = jnp.mean(
                kg[blk * MOBA_BLOCK:(blk + 1) * MOBA_BLOCK], axis=0, keepdims=True)
    for g in range(a_width // LANES):
        vg = qkv[:, 2 * a_width + g * LANES:2 * a_width + (g + 1) * LANES]
        _store_transposed(va_t_ref, g * LANES, vg.T)

    gate = jnp.dot(h, w_gate_ref[...], preferred_element_type=F32) + b_gate_ref[...]
    gate_ref[...] = jax.nn.sigmoid(gate).astype(BF16)

    cos_b, sin_b = cos_b_ref[...], sin_b_ref[...]
    cq = jnp.dot(h, w_cq_ref[...], preferred_element_type=F32)
    cqn = _rms(cq, g_cq_ref[...]).astype(BF16)
    qb = jnp.dot(cqn, w_qup_ref[...], preferred_element_type=F32)
    for hd, qg in enumerate(_rope_lanes(qb, cos_b, sin_b)):
        _store_transposed(qb_t_ref, hd * HEAD_SLOT, (qg * (B_QK ** -0.5 * LOG2_E)).T)
    ckv = jnp.dot(h, w_ckv_ref[...], preferred_element_type=F32)
    ckvn = _rms(ckv, g_ckv_ref[...]).astype(BF16)
    k_nope = jnp.dot(ckvn, w_kup_ref[...], preferred_element_type=F32)
    kr = jnp.dot(h, w_kr_ref[...], preferred_element_type=F32)
    kr = _rope_lanes(kr, cos_b, sin_b)[0]
    for hd in range(B_HEADS):
        kb_ref[:, hd * HEAD_SLOT:(hd + 1) * HEAD_SLOT] = (
            k_nope[:, hd * HEAD_SLOT:(hd + 1) * HEAD_SLOT] + kr).astype(BF16)
    vb = jnp.dot(ckvn, w_vup_ref[...], preferred_element_type=F32)
    for g in range(B_HEADS * B_V // LANES):
        _store_transposed(vb_t_ref, g * LANES, vb[:, g * LANES:(g + 1) * LANES].T)


def _attn_kernel(*refs, moba):
    if moba:
        q_t_ref, k_ref, v_t_ref, kmean_ref, o_t_ref = refs[:5]
        s_scr, smax_scr, p_scr, alpha_scr, m_scr, acc_scr, bias_scr, sbias_scr = refs[5:]
    else:
        q_t_ref, k_ref, v_t_ref, o_t_ref = refs[:4]
        s_scr, smax_scr, p_scr, alpha_scr, m_scr, acc_scr = refs[4:]
    own_lo = OWN_BLOCKS * pl.program_id(2)
    heads = HEADS_PER_STEP
    vdim = v_t_ref.shape[1] // heads
    qry_pos = lax.broadcasted_iota(jnp.int32, (1, Q_TILE), 1)

    def key_lanes(hd):
        g = hd // 2 if moba else hd
        return slice(g * LANES, (g + 1) * LANES)

    def q_head(hd, first=0):
        rows = slice(hd * HEAD_SLOT, (hd + 1) * HEAD_SLOT)
        return jnp.concatenate([q_t_ref[u, rows, :] for u in range(first, OWN_BLOCKS)], axis=1)

    if moba:
        nblk = kmean_ref.shape[0]
        blk = lax.broadcasted_iota(jnp.int32, (nblk, Q_TILE), 0).astype(F32)
        own = (own_lo + qry_pos // MOBA_BLOCK).astype(F32)
        past = blk < own
        for hd in range(heads):
            gate = jnp.dot(kmean_ref[:, key_lanes(hd)].astype(BF16), q_head(hd),
                           preferred_element_type=F32)
            gate = jnp.where(past, gate, -jnp.inf)
            bias = jnp.full(gate.shape, MASK_VALUE, F32)
            for _ in range(MOBA_TOPK):
                best = jnp.max(gate, axis=0, keepdims=True)
                first = jnp.min(jnp.where(gate == best, blk, float(nblk)), axis=0, keepdims=True)
                pick = blk == first
                bias = jnp.where(pick, 0.0, bias)
                gate = jnp.where(pick, -jnp.inf, gate)
            bias_scr[hd] = jnp.where(past, bias, MASK_VALUE)


    def qk_stage(j, first=0, own=False):
        start = pl.multiple_of(j * K_TILE, K_TILE)
        k_blk = k_ref[pl.ds(start, K_TILE), :]
        cols = slice(first * MOBA_BLOCK, Q_TILE)
        for hd in range(heads):
            s = jnp.dot(k_blk[:, key_lanes(hd)], q_head(hd, first), preferred_element_type=F32)
            if own:
                key_pos = lax.broadcasted_iota(jnp.int32, s.shape, 0)
                s = jnp.where(key_pos <= qry_pos[:, :s.shape[1]], s, MASK_VALUE)
            s_scr[hd, :, cols] = s
            smax = jnp.max(s, axis=0, keepdims=True)
            if moba:
                bias = bias_scr[hd, pl.ds(j, 1), cols]
                if own:
                    bias = jnp.where(qry_pos[:, :s.shape[1]] >= MOBA_BLOCK, bias, 0.0)
                sbias_scr[hd, :, cols] = bias
                smax = smax + bias
            smax_scr[hd, :, cols] = smax

    ones_rows = jnp.ones((SUM_ROWS, K_TILE), BF16)

    def softmax_stage(first=0):
        cols = slice(first * MOBA_BLOCK, Q_TILE)
        for hd in range(heads):
            m_old = m_scr[hd, :, cols]
            m_new = jnp.maximum(m_old, smax_scr[hd, :, cols])
            shift = m_new - sbias_scr[hd, :, cols] if moba else m_new
            p_scr[hd, :, cols] = jnp.exp2(s_scr[hd, :, cols] - shift).astype(BF16)
            alpha_scr[hd, :, cols] = jnp.exp2(m_old - m_new)
            m_scr[hd, :, cols] = m_new

    def pv_stage(j, first=0):
        v_t = v_t_ref[j]
        cols = slice(first * MOBA_BLOCK, Q_TILE)
        for hd in range(heads):
            v_ext = jnp.concatenate([v_t[hd * vdim:(hd + 1) * vdim], ones_rows], axis=0)
            acc_scr[hd, :, cols] = alpha_scr[hd, :, cols] * acc_scr[hd, :, cols] + jnp.dot(
                v_ext, p_scr[hd, :, cols], preferred_element_type=F32)

    def visited_block(v):
        return jnp.where(v < OWN_BLOCKS, own_lo + (OWN_BLOCKS - 1) - v, v - OWN_BLOCKS)

    m_scr[...] = jnp.full(m_scr.shape, MASK_VALUE, F32)
    acc_scr[...] = jnp.zeros(acc_scr.shape, F32)
    p_scr[:, :, :MOBA_BLOCK] = jnp.zeros(p_scr.shape[:2] + (MOBA_BLOCK,), BF16)
    alpha_scr[:, :, :MOBA_BLOCK] = jnp.ones(alpha_scr.shape[:2] + (MOBA_BLOCK,), F32)
    qk_stage(own_lo + OWN_BLOCKS - 1, OWN_BLOCKS - 1, own=True)
    for v in range(OWN_BLOCKS - 1):
        u = OWN_BLOCKS - 1 - v
        if v > 0:
            pv_stage(own_lo + u + 1, u + 1)
        softmax_stage(u)
        qk_stage(own_lo + u - 1, u - 1, own=True)

    def body(t, carry):
        pv_stage(visited_block(t - 1))
        softmax_stage()
        qk_stage(t + 1 - OWN_BLOCKS)
        return carry

    last = own_lo + OWN_BLOCKS - 1
    lax.fori_loop(OWN_BLOCKS - 1, last, body, 0)
    pv_stage(visited_block(last - 1))
    softmax_stage()
    pv_stage(visited_block(last))

    for hd in range(heads):
        acc = acc_scr[hd]
        o = (acc[:vdim] / acc[vdim:vdim + 1]).astype(o_t_ref.dtype)
        for u in range(OWN_BLOCKS):
            o_t_ref[u, hd * vdim:(hd + 1) * vdim, :] = o[:, u * MOBA_BLOCK:(u + 1) * MOBA_BLOCK]


def _post_kernel(x_ref, oa_t_ref, ob_t_ref, gate_ref, w_oa_ref, w_ob_ref, w_out_ref,
                 g_mlp_ref, w_ff1_ref, w_ff2_ref, g_final_ref, out_ref, *, ff_chunk, final_norm):
    d_model = x_ref.shape[1]
    contract_rows = (((0,), (0,)), ((), ()))
    ya, yb = [], []
    for qt in range(ROW_TILE // K_TILE):
        ya.append(lax.dot_general(oa_t_ref[qt], w_oa_ref[...], contract_rows, preferred_element_type=F32))
        yb.append(lax.dot_general(ob_t_ref[qt], w_ob_ref[...], contract_rows, preferred_element_type=F32))
    ya = ya[0] if len(ya) == 1 else jnp.concatenate(ya, axis=0)
    yb = yb[0] if len(yb) == 1 else jnp.concatenate(yb, axis=0)
    mixed = gate_ref[:, :d_model].astype(F32) * ya + gate_ref[:, d_model:].astype(F32) * yb
    x1 = x_ref[...] + jnp.dot(mixed.astype(BF16), w_out_ref[...], preferred_element_type=F32)

    h2 = _rms(x1, g_mlp_ref[...]).astype(BF16)
    mlp = jnp.zeros_like(x1)
    for c in range(w_ff1_ref.shape[1] // ff_chunk):
        u = jnp.maximum(jnp.dot(h2, w_ff1_ref[:, c * ff_chunk:(c + 1) * ff_chunk],
                                preferred_element_type=F32), 0.0)
        mlp = mlp + jnp.dot((u * u).astype(BF16), w_ff2_ref[c * ff_chunk:(c + 1) * ff_chunk, :],
                            preferred_element_type=F32)
    x2 = x1 + mlp
    out_ref[...] = _rms(x2, g_final_ref[...]) if final_norm else x2


def _resident(shape):
    return pl.BlockSpec(shape, lambda *_: (0,) * len(shape), pipeline_mode=pl.Buffered(1))


def _rope_tables(seq, dim):
    half = dim // 2
    inv_freq = np.power(ROPE_THETA, -np.arange(half, dtype=np.float64) / half)
    ang = np.arange(seq, dtype=np.float64)[:, None] * inv_freq[None, :]
    return np.cos(ang).astype(np.float32), np.sin(ang).astype(np.float32)


def _moba_group_columns():
    half = A_HEAD_DIM // 2
    order = []
    for pair in range(A_HEADS // 2):
        h0, h1 = 2 * pair * A_HEAD_DIM, (2 * pair + 1) * A_HEAD_DIM
        for start in (h0, h1, h0 + half, h1 + half):
            order.extend(range(start, start + half))
    return jnp.asarray(order, jnp.int32)


def _mla_slot_columns(nope, rope, xp=jnp):
    r = B_ROPE // 2
    split = LANES // 2 - r
    pad = xp.zeros(nope.shape[:-1] + (HEAD_SLOT - B_QK,), nope.dtype)
    return xp.concatenate([rope[..., :r], nope[..., :split], rope[..., r:], nope[..., split:], pad], axis=-1)


def _params(*semantics):
    return pltpu.CompilerParams(dimension_semantics=semantics, vmem_limit_bytes=VMEM_LIMIT_BYTES)


def _pre_call(x2, seq, g_mix, w_in, b_gate, g_cq, w_q_up, g_ckv, w_kv_up):
    rows, d_model = x2.shape
    a_width = A_HEADS * A_HEAD_DIM
    o_q, o_k, o_v = 0, a_width, 2 * a_width
    o_cq = 3 * a_width
    o_ckv = o_cq + B_Q_RANK
    o_kr = o_ckv + B_KV_RANK
    o_gate = o_kr + B_ROPE
    w_in = w_in.astype(BF16)
    cols = _moba_group_columns()
    w_qkv = jnp.concatenate([w_in[:, o_q:o_k][:, cols], w_in[:, o_k:o_v][:, cols], w_in[:, o_v:o_cq]], axis=1)
    w_cq = w_in[:, o_cq:o_ckv]
    w_ckv = w_in[:, o_ckv:o_kr]
    w_kr = _mla_slot_columns(jnp.zeros((d_model, B_NOPE), BF16), w_in[:, o_kr:o_gate])
    w_gate = w_in[:, o_gate:]
    w_q = w_q_up.astype(BF16).reshape(B_Q_RANK, B_HEADS, B_QK)
    w_qup = _mla_slot_columns(w_q[..., :B_NOPE], w_q[..., B_NOPE:]).reshape(B_Q_RANK, B_HEADS * HEAD_SLOT)
    w_kv = w_kv_up.astype(BF16).reshape(B_KV_RANK, B_HEADS, B_NOPE + B_V)
    w_kup = _mla_slot_columns(w_kv[..., :B_NOPE], jnp.zeros((B_KV_RANK, B_HEADS, B_ROPE), BF16)).reshape(
        B_KV_RANK, B_HEADS * HEAD_SLOT)
    w_vup = w_kv[:, :, B_NOPE:].reshape(B_KV_RANK, B_HEADS * B_V)

    cos_a, sin_a = _rope_tables(seq, A_HEAD_DIM)
    cos_a = np.tile(cos_a, (1, LANES // (A_HEAD_DIM // 2)))
    sin_a = np.concatenate([-sin_a, -sin_a, sin_a, sin_a], axis=1)
    cos_b, sin_b = _rope_tables(seq, B_ROPE)
    cos_b = _mla_slot_columns(np.ones((seq, B_NOPE), np.float32), np.concatenate([cos_b, cos_b], axis=1), np)
    sin_b = _mla_slot_columns(np.zeros((seq, B_NOPE), np.float32), np.concatenate([-sin_b, sin_b], axis=1), np)

    n_tiles = rows // ROW_TILE
    seq_tiles = seq // ROW_TILE
    row_spec = lambda w: pl.BlockSpec((ROW_TILE, w), lambda r: (r, 0))
    table_spec = pl.BlockSpec((ROW_TILE, LANES), lambda r: (r % seq_tiles, 0))
    t_spec = lambda h, width: pl.BlockSpec((ROW_TILE // width, h, width), lambda r: (r, 0, 0))
    t_shape = lambda h, width: jax.ShapeDtypeStruct((rows // width, h, width), BF16)
    vec = lambda a: a.reshape(1, -1).astype(F32)
    operands = [
        (x2, row_spec(d_model)), (vec(g_mix), _resident((1, d_model))),
        (cos_a, table_spec), (sin_a, table_spec), (cos_b, table_spec), (sin_b, table_spec),
        (w_qkv, _resident(w_qkv.shape)), (w_cq, _resident(w_cq.shape)), (w_ckv, _resident(w_ckv.shape)),
        (w_kr, _resident(w_kr.shape)), (w_gate, _resident(w_gate.shape)),
        (vec(b_gate), _resident((1, 2 * d_model))),
        (vec(g_cq), _resident((1, B_Q_RANK))), (w_qup, _resident(w_qup.shape)),
        (vec(g_ckv), _resident((1, B_KV_RANK))), (w_kup, _resident(w_kup.shape)),
        (w_vup, _resident(w_vup.shape)),
    ]
    out_shape = [
        t_shape(A_HEADS * HEAD_SLOT, K_TILE),
        jax.ShapeDtypeStruct((rows, a_width), BF16),
        t_shape(a_width, K_TILE),
        jax.ShapeDtypeStruct((n_tiles, ROW_TILE // MOBA_BLOCK, a_width), F32),
        jax.ShapeDtypeStruct((rows, 2 * d_model), BF16),
        t_shape(B_HEADS * HEAD_SLOT, K_TILE),
        jax.ShapeDtypeStruct((rows, B_HEADS * HEAD_SLOT), BF16),
        t_shape(B_HEADS * B_V, K_TILE),
    ]
    out_specs = [
        t_spec(A_HEADS * HEAD_SLOT, K_TILE), row_spec(a_width), t_spec(a_width, K_TILE),
        pl.BlockSpec((1, ROW_TILE // MOBA_BLOCK, a_width), lambda r: (r, 0, 0)),
        row_spec(2 * d_model), t_spec(B_HEADS * HEAD_SLOT, K_TILE), row_spec(B_HEADS * HEAD_SLOT),
        t_spec(B_HEADS * B_V, K_TILE),
    ]
    return pl.pallas_call(
        _pre_kernel,
        grid=(n_tiles,),
        in_specs=[s for _, s in operands],
        out_specs=out_specs,
        out_shape=out_shape,
        compiler_params=_params("parallel"),
        name="pre_proj",
    )(*[a for a, _ in operands])


def _attn_call(q_t, k, v_t, kmean, *, batch, seq, moba):
    nblk = seq // K_TILE
    n_q = seq // Q_TILE
    heads = q_t.shape[1] // HEAD_SLOT
    groups = heads // HEADS_PER_STEP
    vdim = v_t.shape[2] // heads
    key_lanes = k.shape[2] // groups
    kv_mode = None if HEADS_PER_STEP <= 4 else pl.Buffered(1)
    in_specs = [
        pl.BlockSpec((OWN_BLOCKS, HEADS_PER_STEP * HEAD_SLOT, K_TILE), lambda b, g, i: (b * n_q + i, g, 0)),
        pl.BlockSpec((None, seq, key_lanes), lambda b, g, i: (b, 0, g), pipeline_mode=kv_mode),
        pl.BlockSpec((None, nblk, HEADS_PER_STEP * vdim, K_TILE), lambda b, g, i: (b, 0, g, 0),
                     pipeline_mode=kv_mode),
    ]
    operands = [q_t, k, v_t]
    stat = pltpu.VMEM((HEADS_PER_STEP, 1, Q_TILE), F32)
    scratch = [
        pltpu.VMEM((HEADS_PER_STEP, K_TILE, Q_TILE), F32),
        stat,
        pltpu.VMEM((HEADS_PER_STEP, K_TILE, Q_TILE), BF16),
        stat,
        stat,
        pltpu.VMEM((HEADS_PER_STEP, vdim + SUM_ROWS, Q_TILE), F32),
    ]
    if moba:
        in_specs.append(pl.BlockSpec((None, nblk, key_lanes), lambda b, g, i: (b, 0, g)))
        operands.append(kmean)
        scratch.append(pltpu.VMEM((HEADS_PER_STEP, nblk, Q_TILE), F32))
        scratch.append(stat)
    return pl.pallas_call(
        functools.partial(_attn_kernel, moba=moba),
        grid=(batch, groups, n_q),
        in_specs=in_specs,
        out_specs=pl.BlockSpec((OWN_BLOCKS, HEADS_PER_STEP * vdim, K_TILE), lambda b, g, i: (b * n_q + i, g, 0)),
        out_shape=jax.ShapeDtypeStruct((batch * nblk, heads * vdim, K_TILE), BF16),
        scratch_shapes=scratch,
        compiler_params=_params("parallel", "parallel", "arbitrary"),
        name="moba_attn" if moba else "mla_attn",
    )(*operands)


def _post_call(x2, oa_t, ob_t, gates, w_o_a, w_o_b, w_out, g_mlp, w_ff1, w_ff2, g_final, final_norm):
    rows, d_model = x2.shape
    row_spec = lambda w: pl.BlockSpec((ROW_TILE, w), lambda r: (r, 0))
    t_spec = lambda a: pl.BlockSpec((ROW_TILE // K_TILE,) + a.shape[1:], lambda r: (r, 0, 0))
    vec = lambda a: a.reshape(1, -1).astype(F32)
    weights = [w.astype(BF16) for w in (w_o_a, w_o_b, w_out)]
    operands = [(x2, row_spec(d_model)), (oa_t, t_spec(oa_t)), (ob_t, t_spec(ob_t)),
                (gates, row_spec(2 * d_model))]
    operands += [(w, _resident(w.shape)) for w in weights]
    operands += [(vec(g_mlp), _resident((1, d_model))),
                 (w_ff1.astype(BF16), _resident(w_ff1.shape)), (w_ff2.astype(BF16), _resident(w_ff2.shape)),
                 (vec(g_final), _resident((1, d_model)))]
    return pl.pallas_call(
        functools.partial(_post_kernel, ff_chunk=FF_CHUNK, final_norm=final_norm),
        grid=(rows // ROW_TILE,),
        in_specs=[s for _, s in operands],
        out_specs=row_spec(d_model),
        out_shape=jax.ShapeDtypeStruct((rows, d_model), F32),
        compiler_params=_params("parallel"),
        name="post_mlp",
    )(*[a for a, _ in operands])


def kernel(x, g_mix, w_in, b_gate, g_cq, w_q_up, g_ckv, w_kv_up, w_o_a, w_o_b, w_out, g_mlp, w_ff1,
           w_ff2, g_final):
    batch, seq, d_model = x.shape
    depth = w_in.shape[0]
    assert seq % ROW_TILE == 0 and ROW_TILE % MOBA_BLOCK == 0 and seq % Q_TILE == 0 and OWN_BLOCKS >= 2
    assert A_HEADS % HEADS_PER_STEP == 0 and B_HEADS % HEADS_PER_STEP == 0 and HEADS_PER_STEP % 2 == 0
    nblk = seq // K_TILE
    x2 = x.reshape(batch * seq, d_model)
    for l in range(depth):
        qa_t, ka, va_t, kmean, gates, qb_t, kb, vb_t = _pre_call(
            x2, seq, g_mix[l], w_in[l], b_gate[l], g_cq[l], w_q_up[l], g_ckv[l], w_kv_up[l])
        oa_t = _attn_call(qa_t, ka.reshape(batch, seq, -1), va_t.reshape(batch, nblk, -1, K_TILE),
                          kmean.reshape(batch, nblk, -1), batch=batch, seq=seq, moba=True)
        ob_t = _attn_call(qb_t, kb.reshape(batch, seq, -1), vb_t.reshape(batch, nblk, -1, K_TILE),
                          None, batch=batch, seq=seq, moba=False)
        x2 = _post_call(x2, oa_t, ob_t, gates, w_o_a[l], w_o_b[l], w_out[l], g_mlp[l], w_ff1[l],
                        w_ff2[l], g_final, final_norm=(l == depth - 1))
    return x2.reshape(batch, seq, d_model)
```

```python
import functools

import jax
import jax.numpy as jnp
import numpy as np
from jax import lax
from jax.experimental import pallas as pl
from jax.experimental.pallas import tpu as pltpu

EPS = 1e-6
ROPE_THETA = 10000.0

A_HEADS = 8
A_HEAD_DIM = 64
MOBA_BLOCK = 256
MOBA_TOPK = 3

B_HEADS = 8
B_NOPE = 64
B_ROPE = 32
B_QK = B_NOPE + B_ROPE
B_V = 64
B_Q_RANK = 384
B_KV_RANK = 256

LANES = 128
HEAD_SLOT = 128
ROW_TILE = 512
FF_CHUNK = 1024
K_TILE = MOBA_BLOCK
OWN_BLOCKS = 16
Q_TILE = OWN_BLOCKS * MOBA_BLOCK
HEADS_PER_STEP = 2
MASK_VALUE = -1e30
LOG2_E = 1.4426950408889634
SUM_ROWS = 16
VMEM_LIMIT_BYTES = 56 * 1024 * 1024

F32 = jnp.float32
BF16 = jnp.bfloat16


def _rms(x, g):
    return x * lax.rsqrt(jnp.mean(x * x, axis=-1, keepdims=True) + EPS) * g


def _rope_lanes(x, cos, sin_signed):
    groups = []
    for g in range(x.shape[1] // LANES):
        xg = x[:, g * LANES:(g + 1) * LANES]
        groups.append(xg * cos + pltpu.roll(xg, LANES // 2, axis=1) * sin_signed)
    return groups


def _store_transposed(dst_ref, row0, t):
    rows, width = t.shape[0], dst_ref.shape[2]
    for tile in range(ROW_TILE // width):
        dst_ref[tile, row0:row0 + rows, :] = t[:, tile * width:(tile + 1) * width].astype(dst_ref.dtype)


def _pre_kernel(x_ref, gmix_ref, cos_a_ref, sin_a_ref, cos_b_ref, sin_b_ref,
                w_qkv_ref, w_cq_ref, w_ckv_ref, w_kr_ref, w_gate_ref, b_gate_ref,
                g_cq_ref, w_qup_ref, g_ckv_ref, w_kup_ref, w_vup_ref,
                qa_t_ref, ka_ref, va_t_ref, kmean_ref, gate_ref, qb_t_ref, kb_ref, vb_t_ref):
    a_width = A_HEADS * A_HEAD_DIM
    h = _rms(x_ref[...], gmix_ref[...]).astype(BF16)

    qkv = jnp.dot(h, w_qkv_ref[...], preferred_element_type=F32)
    cos_a, sin_a = cos_a_ref[...], sin_a_ref[...]
    q_groups = _rope_lanes(qkv[:, :a_width], cos_a, sin_a)
    k_groups = _rope_lanes(qkv[:, a_width:2 * a_width], cos_a, sin_a)
    first_head = (lax.broadcasted_iota(jnp.int32, (LANES, ROW_TILE), 0) % A_HEAD_DIM) < A_HEAD_DIM // 2
    for g, qg in enumerate(q_groups):
        t = (qg * (A_HEAD_DIM ** -0.5 * LOG2_E)).T
        _store_transposed(qa_t_ref, (2 * g) * HEAD_SLOT, jnp.where(first_head, t, 0.0))
        _store_transposed(qa_t_ref, (2 * g + 1) * HEAD_SLOT, jnp.where(first_head, 0.0, t))
    for g, kg in enumerate(k_groups):
        ka_ref[:, g * LANES:(g + 1) * LANES] = kg.astype(BF16)
        for blk in range(ROW_TILE // MOBA_BLOCK):
            kmean_ref[0, blk:blk + 1, g * LANES:(g + 1) * LANES] = jnp.mean(
                kg[blk * MOBA_BLOCK:(blk + 1) * MOBA_BLOCK], axis=0, keepdims=True)
    for g in range(a_width // LANES):
        vg = qkv[:, 2 * a_width + g * LANES:2 * a_width + (g + 1) * LANES]
        _store_transposed(va_t_ref, g * LANES, vg.T)

    gate = jnp.dot(h, w_gate_ref[...], preferred_element_type=F32) + b_gate_ref[...]
    gate_ref[...] = jax.nn.sigmoid(gate).astype(BF16)

    cos_b, sin_b = cos_b_ref[...], sin_b_ref[...]
    cq = jnp.dot(h, w_cq_ref[...], preferred_element_type=F32)
    cqn = _rms(cq, g_cq_ref[...]).astype(BF16)
    qb = jnp.dot(cqn, w_qup_ref[...], preferred_element_type=F32)
    for hd, qg in enumerate(_rope_lanes(qb, cos_b, sin_b)):
        _store_transposed(qb_t_ref, hd * HEAD_SLOT, (qg * (B_QK ** -0.5 * LOG2_E)).T)
    ckv = jnp.dot(h, w_ckv_ref[...], preferred_element_type=F32)
    ckvn = _rms(ckv, g_ckv_ref[...]).astype(BF16)
    k_nope = jnp.dot(ckvn, w_kup_ref[...], preferred_element_type=F32)
    kr = jnp.dot(h, w_kr_ref[...], preferred_element_type=F32)
    kr = _rope_lanes(kr, cos_b, sin_b)[0]
    for hd in range(B_HEADS):
        kb_ref[:, hd * HEAD_SLOT:(hd + 1) * HEAD_SLOT] = (
            k_nope[:, hd * HEAD_SLOT:(hd + 1) * HEAD_SLOT] + kr).astype(BF16)
    vb = jnp.dot(ckvn, w_vup_ref[...], preferred_element_type=F32)
    for g in range(B_HEADS * B_V // LANES):
        _store_transposed(vb_t_ref, g * LANES, vb[:, g * LANES:(g + 1) * LANES].T)


def _attn_kernel(*refs, moba):
    if moba:
        q_t_ref, k_ref, v_t_ref, kmean_ref, o_t_ref = refs[:5]
        s_scr, smax_scr, p_scr, alpha_scr, m_scr, acc_scr, bias_scr, sbias_scr = refs[5:]
    else:
        q_t_ref, k_ref, v_t_ref, o_t_ref = refs[:4]
        s_scr, smax_scr, p_scr, alpha_scr, m_scr, acc_scr = refs[4:]
    own_lo = OWN_BLOCKS * pl.program_id(2)
    heads = HEADS_PER_STEP
    vdim = v_t_ref.shape[1] // heads
    qry_pos = lax.broadcasted_iota(jnp.int32, (1, Q_TILE), 1)

    def key_lanes(hd):
        g = hd // 2 if moba else hd
        return slice(g * LANES, (g + 1) * LANES)

    def q_head(hd, first=0):
        rows = slice(hd * HEAD_SLOT, (hd + 1) * HEAD_SLOT)
        return jnp.concatenate([q_t_ref[u, rows, :] for u in range(first, OWN_BLOCKS)], axis=1)

    if moba:
        nblk = kmean_ref.shape[0]
        blk = lax.broadcasted_iota(jnp.int32, (nblk, Q_TILE), 0).astype(F32)
        own = (own_lo + qry_pos // MOBA_BLOCK).astype(F32)
        past = blk < own
        for hd in range(heads):
            gate = jnp.dot(kmean_ref[:, key_lanes(hd)].astype(BF16), q_head(hd),
                           preferred_element_type=F32)
            gate = jnp.where(past, gate, -jnp.inf)
            bias = jnp.full(gate.shape, MASK_VALUE, F32)
            for _ in range(MOBA_TOPK):
                best = jnp.max(gate, axis=0, keepdims=True)
                first = jnp.min(jnp.where(gate == best, blk, float(nblk)), axis=0, keepdims=True)
                pick = blk == first
                bias = jnp.where(pick, 0.0, bias)
                gate = jnp.where(pick, -jnp.inf, gate)
            bias_scr[hd] = jnp.where(past, bias, MASK_VALUE)


    def qk_stage(j, first=0, own=False):
        start = pl.multiple_of(j * K_TILE, K_TILE)
        k_blk = k_ref[pl.ds(start, K_TILE), :]
        cols = slice(first * MOBA_BLOCK, Q_TILE)
        for hd in range(heads):
            s = jnp.dot(k_blk[:, key_lanes(hd)], q_head(hd, first), preferred_element_type=F32)
            if own:
                key_pos = lax.broadcasted_iota(jnp.int32, s.shape, 0)
                s = jnp.where(key_pos <= qry_pos[:, :s.shape[1]], s, MASK_VALUE)
            s_scr[hd, :, cols] = s
            smax = jnp.max(s, axis=0, keepdims=True)
            if moba:
                bias = bias_scr[hd, pl.ds(j, 1), cols]
                if own:
                    bias = jnp.where(qry_pos[:, :s.shape[1]] >= MOBA_BLOCK, bias, 0.0)
                sbias_scr[hd, :, cols] = bias
                smax = smax + bias
            smax_scr[hd, :, cols] = smax

    ones_rows = jnp.ones((SUM_ROWS, K_TILE), BF16)

    def softmax_stage(first=0):
        cols = slice(first * MOBA_BLOCK, Q_TILE)
        for hd in range(heads):
            m_old = m_scr[hd, :, cols]
            m_new = jnp.maximum(m_old, smax_scr[hd, :, cols])
            shift = m_new - sbias_scr[hd, :, cols] if moba else m_new
            p_scr[hd, :, cols] = jnp.exp2(s_scr[hd, :, cols] - shift).astype(BF16)
            alpha_scr[hd, :, cols] = jnp.exp2(m_old - m_new)
            m_scr[hd, :, cols] = m_new

    def pv_stage(j, first=0):
        v_t = v_t_ref[j]
        cols = slice(first * MOBA_BLOCK, Q_TILE)
        for hd in range(heads):
            v_ext = jnp.concatenate([v_t[hd * vdim:(hd + 1) * vdim], ones_rows], axis=0)
            acc_scr[hd, :, cols] = alpha_scr[hd, :, cols] * acc_scr[hd, :, cols] + jnp.dot(
                v_ext, p_scr[hd, :, cols], preferred_element_type=F32)

    def visited_block(v):
        return jnp.where(v < OWN_BLOCKS, own_lo + (OWN_BLOCKS - 1) - v, v - OWN_BLOCKS)

    m_scr[...] = jnp.full(m_scr.shape, MASK_VALUE, F32)
    acc_scr[...] = jnp.zeros(acc_scr.shape, F32)
    p_scr[:, :, :MOBA_BLOCK] = jnp.zeros(p_scr.shape[:2] + (MOBA_BLOCK,), BF16)
    alpha_scr[:, :, :MOBA_BLOCK] = jnp.ones(alpha_scr.shape[:2] + (MOBA_BLOCK,), F32)
    qk_stage(own_lo + OWN_BLOCKS - 1, OWN_BLOCKS - 1, own=True)
    for v in range(OWN_BLOCKS - 1):
        u = OWN_BLOCKS - 1 - v
        if v > 0:
            pv_stage(own_lo + u + 1, u + 1)
        softmax_stage(u)
        qk_stage(own_lo + u - 1, u - 1, own=True)

    def body(t, carry):
        pv_stage(visited_block(t - 1))
        softmax_stage()
        qk_stage(t + 1 - OWN_BLOCKS)
        return carry

    last = own_lo + OWN_BLOCKS - 1
    lax.fori_loop(OWN_BLOCKS - 1, last, body, 0)
    pv_stage(visited_block(last - 1))
    softmax_stage()
    pv_stage(visited_block(last))

    for hd in range(heads):
        acc = acc_scr[hd]
        o = (acc[:vdim] / acc[vdim:vdim + 1]).astype(o_t_ref.dtype)
        for u in range(OWN_BLOCKS):
            o_t_ref[u, hd * vdim:(hd + 1) * vdim, :] = o[:, u * MOBA_BLOCK:(u + 1) * MOBA_BLOCK]


def _post_kernel(x_ref, oa_t_ref, ob_t_ref, gate_ref, w_oa_ref, w_ob_ref, w_out_ref,
                 g_mlp_ref, w_ff1_ref, w_ff2_ref, g_final_ref, out_ref, *, ff_chunk, final_norm):
    d_model = x_ref.shape[1]
    contract_rows = (((0,), (0,)), ((), ()))
    ya, yb = [], []
    for qt in range(ROW_TILE // K_TILE):
        ya.append(lax.dot_general(oa_t_ref[qt], w_oa_ref[...], contract_rows, preferred_element_type=F32))
        yb.append(lax.dot_general(ob_t_ref[qt], w_ob_ref[...], contract_rows, preferred_element_type=F32))
    ya = ya[0] if len(ya) == 1 else jnp.concatenate(ya, axis=0)
    yb = yb[0] if len(yb) == 1 else jnp.concatenate(yb, axis=0)
    mixed = gate_ref[:, :d_model].astype(F32) * ya + gate_ref[:, d_model:].astype(F32) * yb
    x1 = x_ref[...] + jnp.dot(mixed.astype(BF16), w_out_ref[...], preferred_element_type=F32)

    h2 = _rms(x1, g_mlp_ref[...]).astype(BF16)
    mlp = jnp.zeros_like(x1)
    for c in range(w_ff1_ref.shape[1] // ff_chunk):
        u = jnp.maximum(jnp.dot(h2, w_ff1_ref[:, c * ff_chunk:(c + 1) * ff_chunk],
                                preferred_element_type=F32), 0.0)
        mlp = mlp + jnp.dot((u * u).astype(BF16), w_ff2_ref[c * ff_chunk:(c + 1) * ff_chunk, :],
                            preferred_element_type=F32)
    x2 = x1 + mlp
    out_ref[...] = _rms(x2, g_final_ref[...]) if final_norm else x2


def _resident(shape):
    return pl.BlockSpec(shape, lambda *_: (0,) * len(shape), pipeline_mode=pl.Buffered(1))


def _rope_tables(seq, dim):
    half = dim // 2
    inv_freq = np.power(ROPE_THETA, -np.arange(half, dtype=np.float64) / half)
    ang = np.arange(seq, dtype=np.float64)[:, None] * inv_freq[None, :]
    return np.cos(ang).astype(np.float32), np.sin(ang).astype(np.float32)


def _moba_group_columns():
    half = A_HEAD_DIM // 2
    order = []
    for pair in range(A_HEADS // 2):
        h0, h1 = 2 * pair * A_HEAD_DIM, (2 * pair + 1) * A_HEAD_DIM
        for start in (h0, h1, h0 + half, h1 + half):
            order.extend(range(start, start + half))
    return jnp.asarray(order, jnp.int32)


def _mla_slot_columns(nope, rope, xp=jnp):
    r = B_ROPE // 2
    split = LANES // 2 - r
    pad = xp.zeros(nope.shape[:-1] + (HEAD_SLOT - B_QK,), nope.dtype)
    return xp.concatenate([rope[..., :r], nope[..., :split], rope[..., r:], nope[..., split:], pad], axis=-1)


def _params(*semantics):
    return pltpu.CompilerParams(dimension_semantics=semantics, vmem_limit_bytes=VMEM_LIMIT_BYTES)


def _pre_call(x2, seq, g_mix, w_in, b_gate, g_cq, w_q_up, g_ckv, w_kv_up):
    rows, d_model = x2.shape
    a_width = A_HEADS * A_HEAD_DIM
    o_q, o_k, o_v = 0, a_width, 2 * a_width
    o_cq = 3 * a_width
    o_ckv = o_cq + B_Q_RANK
    o_kr = o_ckv + B_KV_RANK
    o_gate = o_kr + B_ROPE
    w_in = w_in.astype(BF16)
    cols = _moba_group_columns()
    w_qkv = jnp.concatenate([w_in[:, o_q:o_k][:, cols], w_in[:, o_k:o_v][:, cols], w_in[:, o_v:o_cq]], axis=1)
    w_cq = w_in[:, o_cq:o_ckv]
    w_ckv = w_in[:, o_ckv:o_kr]
    w_kr = _mla_slot_columns(jnp.zeros((d_model, B_NOPE), BF16), w_in[:, o_kr:o_gate])
    w_gate = w_in[:, o_gate:]
    w_q = w_q_up.astype(BF16).reshape(B_Q_RANK, B_HEADS, B_QK)
    w_qup = _mla_slot_columns(w_q[..., :B_NOPE], w_q[..., B_NOPE:]).reshape(B_Q_RANK, B_HEADS * HEAD_SLOT)
    w_kv = w_kv_up.astype(BF16).reshape(B_KV_RANK, B_HEADS, B_NOPE + B_V)
    w_kup = _mla_slot_columns(w_kv[..., :B_NOPE], jnp.zeros((B_KV_RANK, B_HEADS, B_ROPE), BF16)).reshape(
        B_KV_RANK, B_HEADS * HEAD_SLOT)
    w_vup = w_kv[:, :, B_NOPE:].reshape(B_KV_RANK, B_HEADS * B_V)

    cos_a, sin_a = _rope_tables(seq, A_HEAD_DIM)
    cos_a = np.tile(cos_a, (1, LANES // (A_HEAD_DIM // 2)))
    sin_a = np.concatenate([-sin_a, -sin_a, sin_a, sin_a], axis=1)
    cos_b, sin_b = _rope_tables(seq, B_ROPE)
    cos_b = _mla_slot_columns(np.ones((seq, B_NOPE), np.float32), np.concatenate([cos_b, cos_b], axis=1), np)
    sin_b = _mla_slot_columns(np.zeros((seq, B_NOPE), np.float32), np.concatenate([-sin_b, sin_b], axis=1), np)

    n_tiles = rows // ROW_TILE
    seq_tiles = seq // ROW_TILE
    row_spec = lambda w: pl.BlockSpec((ROW_TILE, w), lambda r: (r, 0))
    table_spec = pl.BlockSpec((ROW_TILE, LANES), lambda r: (r % seq_tiles, 0))
    t_spec = lambda h, width: pl.BlockSpec((ROW_TILE // width, h, width), lambda r: (r, 0, 0))
    t_shape = lambda h, width: jax.ShapeDtypeStruct((rows // width, h, width), BF16)
    vec = lambda a: a.reshape(1, -1).astype(F32)
    operands = [
        (x2, row_spec(d_model)), (vec(g_mix), _resident((1, d_model))),
        (cos_a, table_spec), (sin_a, table_spec), (cos_b, table_spec), (sin_b, table_spec),
        (w_qkv, _resident(w_qkv.shape)), (w_cq, _resident(w_cq.shape)), (w_ckv, _resident(w_ckv.shape)),
        (w_kr, _resident(w_kr.shape)), (w_gate, _resident(w_gate.shape)),
        (vec(b_gate), _resident((1, 2 * d_model))),
        (vec(g_cq), _resident((1, B_Q_RANK))), (w_qup, _resident(w_qup.shape)),
        (vec(g_ckv), _resident((1, B_KV_RANK))), (w_kup, _resident(w_kup.shape)),
        (w_vup, _resident(w_vup.shape)),
    ]
    out_shape = [
        t_shape(A_HEADS * HEAD_SLOT, K_TILE),
        jax.ShapeDtypeStruct((rows, a_width), BF16),
        t_shape(a_width, K_TILE),
        jax.ShapeDtypeStruct((n_tiles, ROW_TILE // MOBA_BLOCK, a_width), F32),
        jax.ShapeDtypeStruct((rows, 2 * d_model), BF16),
        t_shape(B_HEADS * HEAD_SLOT, K_TILE),
        jax.ShapeDtypeStruct((rows, B_HEADS * HEAD_SLOT), BF16),
        t_shape(B_HEADS * B_V, K_TILE),
    ]
    out_specs = [
        t_spec(A_HEADS * HEAD_SLOT, K_TILE), row_spec(a_width), t_spec(a_width, K_TILE),
        pl.BlockSpec((1, ROW_TILE // MOBA_BLOCK, a_width), lambda r: (r, 0, 0)),
        row_spec(2 * d_model), t_spec(B_HEADS * HEAD_SLOT, K_TILE), row_spec(B_HEADS * HEAD_SLOT),
        t_spec(B_HEADS * B_V, K_TILE),
    ]
    return pl.pallas_call(
        _pre_kernel,
        grid=(n_tiles,),
        in_specs=[s for _, s in operands],
        out_specs=out_specs,
        out_shape=out_shape,
        compiler_params=_params("parallel"),
        name="pre_proj",
    )(*[a for a, _ in operands])


def _attn_call(q_t, k, v_t, kmean, *, batch, seq, moba):
    nblk = seq // K_TILE
    n_q = seq // Q_TILE
    heads = q_t.shape[1] // HEAD_SLOT
    groups = heads // HEADS_PER_STEP
    vdim = v_t.shape[2] // heads
    key_lanes = k.shape[2] // groups
    kv_mode = None if HEADS_PER_STEP <= 4 else pl.Buffered(1)
    in_specs = [
        pl.BlockSpec((OWN_BLOCKS, HEADS_PER_STEP * HEAD_SLOT, K_TILE), lambda b, g, i: (b * n_q + i, g, 0)),
        pl.BlockSpec((None, seq, key_lanes), lambda b, g, i: (b, 0, g), pipeline_mode=kv_mode),
        pl.BlockSpec((None, nblk, HEADS_PER_STEP * vdim, K_TILE), lambda b, g, i: (b, 0, g, 0),
                     pipeline_mode=kv_mode),
    ]
    operands = [q_t, k, v_t]
    stat = pltpu.VMEM((HEADS_PER_STEP, 1, Q_TILE), F32)
    scratch = [
        pltpu.VMEM((HEADS_PER_STEP, K_TILE, Q_TILE), F32),
        stat,
        pltpu.VMEM((HEADS_PER_STEP, K_TILE, Q_TILE), BF16),
        stat,
        stat,
        pltpu.VMEM((HEADS_PER_STEP, vdim + SUM_ROWS, Q_TILE), F32),
    ]
    if moba:
        in_specs.append(pl.BlockSpec((None, nblk, key_lanes), lambda b, g, i: (b, 0, g)))
        operands.append(kmean)
        scratch.append(pltpu.VMEM((HEADS_PER_STEP, nblk, Q_TILE), F32))
        scratch.append(stat)
    return pl.pallas_call(
        functools.partial(_attn_kernel, moba=moba),
        grid=(batch, groups, n_q),
        in_specs=in_specs,
        out_specs=pl.BlockSpec((OWN_BLOCKS, HEADS_PER_STEP * vdim, K_TILE), lambda b, g, i: (b * n_q + i, g, 0)),
        out_shape=jax.ShapeDtypeStruct((batch * nblk, heads * vdim, K_TILE), BF16),
        scratch_shapes=scratch,
        compiler_params=_params("parallel", "parallel", "arbitrary"),
        name="moba_attn" if moba else "mla_attn",
    )(*operands)


def _post_call(x2, oa_t, ob_t, gates, w_o_a, w_o_b, w_out, g_mlp, w_ff1, w_ff2, g_final, final_norm):
    rows, d_model = x2.shape
    row_spec = lambda w: pl.BlockSpec((ROW_TILE, w), lambda r: (r, 0))
    t_spec = lambda a: pl.BlockSpec((ROW_TILE // K_TILE,) + a.shape[1:], lambda r: (r, 0, 0))
    vec = lambda a: a.reshape(1, -1).astype(F32)
    weights = [w.astype(BF16) for w in (w_o_a, w_o_b, w_out)]
    operands = [(x2, row_spec(d_model)), (oa_t, t_spec(oa_t)), (ob_t, t_spec(ob_t)),
                (gates, row_spec(2 * d_model))]
    operands += [(w, _resident(w.shape)) for w in weights]
    operands += [(vec(g_mlp), _resident((1, d_model))),
                 (w_ff1.astype(BF16), _resident(w_ff1.shape)), (w_ff2.astype(BF16), _resident(w_ff2.shape)),
                 (vec(g_final), _resident((1, d_model)))]
    return pl.pallas_call(
        functools.partial(_post_kernel, ff_chunk=FF_CHUNK, final_norm=final_norm),
        grid=(rows // ROW_TILE,),
        in_specs=[s for _, s in operands],
        out_specs=row_spec(d_model),
        out_shape=jax.ShapeDtypeStruct((rows, d_model), F32),
        compiler_params=_params("parallel"),
        name="post_mlp",
    )(*[a for a, _ in operands])


def kernel(x, g_mix, w_in, b_gate, g_cq, w_q_up, g_ckv, w_kv_up, w_o_a, w_o_b, w_out, g_mlp, w_ff1,
           w_ff2, g_final):
    batch, seq, d_model = x.shape
    depth = w_in.shape[0]
    assert seq % ROW_TILE == 0 and ROW_TILE % MOBA_BLOCK == 0 and seq % Q_TILE == 0 and OWN_BLOCKS >= 2
    assert A_HEADS % HEADS_PER_STEP == 0 and B_HEADS % HEADS_PER_STEP == 0 and HEADS_PER_STEP % 2 == 0
    nblk = seq // K_TILE
    x2 = x.reshape(batch * seq, d_model)
    for l in range(depth):
        qa_t, ka, va_t, kmean, gates, qb_t, kb, vb_t = _pre_call(
            x2, seq, g_mix[l], w_in[l], b_gate[l], g_cq[l], w_q_up[l], g_ckv[l], w_kv_up[l])
        oa_t = _attn_call(qa_t, ka.reshape(batch, seq, -1), va_t.reshape(batch, nblk, -1, K_TILE),
                          kmean.reshape(batch, nblk, -1), batch=batch, seq=seq, moba=True)
        ob_t = _attn_call(qb_t, kb.reshape(batch, seq, -1), vb_t.reshape(batch, nblk, -1, K_TILE),
                          None, batch=batch, seq=seq, moba=False)
        x2 = _post_call(x2, oa_t, ob_t, gates, w_o_a[l], w_o_b[l], w_out[l], g_mlp[l], w_ff1[l],
                        w_ff2[l], g_final, final_norm=(l == depth - 1))
    return x2.reshape(batch, seq, d_model)
```

```python
import functools

import jax
import jax.numpy as jnp
import numpy as np
from jax import lax
from jax.experimental import pallas as pl
from jax.experimental.pallas import tpu as pltpu

EPS = 1e-6
ROPE_THETA = 10000.0

A_HEADS = 8
A_HEAD_DIM = 64
MOBA_BLOCK = 256
MOBA_TOPK = 3

B_HEADS = 8
B_NOPE = 64
B_ROPE = 32
B_QK = B_NOPE + B_ROPE
B_V = 64
B_Q_RANK = 384
B_KV_RANK = 256

LANES = 128
HEAD_SLOT = 128
ROW_TILE = 512
FF_CHUNK = 1024
K_TILE = MOBA_BLOCK
OWN_BLOCKS = 8
Q_TILE = OWN_BLOCKS * MOBA_BLOCK
HEADS_PER_STEP = 4
MASK_VALUE = -1e30
LOG2_E = 1.4426950408889634
SUM_ROWS = 16
VMEM_LIMIT_BYTES = 56 * 1024 * 1024

F32 = jnp.float32
BF16 = jnp.bfloat16


def _rms(x, g):
    return x * lax.rsqrt(jnp.mean(x * x, axis=-1, keepdims=True) + EPS) * g


def _rope_lanes(x, cos, sin_signed):
    groups = []
    for g in range(x.shape[1] // LANES):
        xg = x[:, g * LANES:(g + 1) * LANES]
        groups.append(xg * cos + pltpu.roll(xg, LANES // 2, axis=1) * sin_signed)
    return groups


def _store_transposed(dst_ref, row0, t):
    rows, width = t.shape[0], dst_ref.shape[2]
    for tile in range(ROW_TILE // width):
        dst_ref[tile, row0:row0 + rows, :] = t[:, tile * width:(tile + 1) * width].astype(dst_ref.dtype)


def _rope_rows(x_t, cos_t, sin_signed_t):
    groups = []
    for g in range(x_t.shape[0] // LANES):
        xg = x_t[g * LANES:(g + 1) * LANES]
        partner = jnp.concatenate([xg[LANES // 2:], xg[:LANES // 2]], axis=0)
        groups.append(xg * cos_t + partner * sin_signed_t)
    return groups


def _pre_kernel(x_ref, gmix_ref, cos_a_ref, sin_a_ref, cos_b_ref, sin_b_ref,
                cos_at_ref, sin_at_ref, cos_bt_ref, sin_bt_ref,
                w_q_t_ref, w_k_ref, w_v_t_ref, w_cq_t_ref, w_ckv_ref, w_kr_ref, w_gate_ref, b_gate_ref,
                g_cq_ref, w_qup_t_ref, g_ckv_ref, w_kup_ref, w_vup_t_ref,
                qa_t_ref, ka_ref, va_t_ref, kmean_ref, gate_ref, qb_t_ref, kb_ref, vb_t_ref):
    xn = _rms(x_ref[...], gmix_ref[...])
    h = xn.astype(BF16)
    h_t = xn.T.astype(BF16)

    q_t = jnp.dot(w_q_t_ref[...], h_t, preferred_element_type=F32)
    first_head = (lax.broadcasted_iota(jnp.int32, (LANES, ROW_TILE), 0) % A_HEAD_DIM) < A_HEAD_DIM // 2
    for g, qg in enumerate(_rope_rows(q_t, cos_at_ref[...], sin_at_ref[...])):
        t = qg * (A_HEAD_DIM ** -0.5 * LOG2_E)
        _store_transposed(qa_t_ref, (2 * g) * HEAD_SLOT, jnp.where(first_head, t, 0.0))
        _store_transposed(qa_t_ref, (2 * g + 1) * HEAD_SLOT, jnp.where(first_head, 0.0, t))
    k = jnp.dot(h, w_k_ref[...], preferred_element_type=F32)
    for g, kg in enumerate(_rope_lanes(k, cos_a_ref[...], sin_a_ref[...])):
        ka_ref[:, g * LANES:(g + 1) * LANES] = kg.astype(BF16)
        for blk in range(ROW_TILE // MOBA_BLOCK):
            kmean_ref[0, blk:blk + 1, g * LANES:(g + 1) * LANES] = jnp.mean(
                kg[blk * MOBA_BLOCK:(blk + 1) * MOBA_BLOCK], axis=0, keepdims=True)
    _store_transposed(va_t_ref, 0, jnp.dot(w_v_t_ref[...], h_t, preferred_element_type=F32))

    gate = jnp.dot(h, w_gate_ref[...], preferred_element_type=F32) + b_gate_ref[...]
    gate_ref[...] = jax.nn.sigmoid(gate).astype(BF16)

    cq_t = jnp.dot(w_cq_t_ref[...], h_t, preferred_element_type=F32)
    cqn_t = cq_t * lax.rsqrt(jnp.mean(cq_t * cq_t, axis=0, keepdims=True) + EPS) * g_cq_ref[...]
    qb_t = jnp.dot(w_qup_t_ref[...], cqn_t.astype(BF16), preferred_element_type=F32)
    for hd, qg in enumerate(_rope_rows(qb_t, cos_bt_ref[...], sin_bt_ref[...])):
        _store_transposed(qb_t_ref, hd * HEAD_SLOT, qg * (B_QK ** -0.5 * LOG2_E))

    cos_b, sin_b = cos_b_ref[...], sin_b_ref[...]
    ckv = jnp.dot(h, w_ckv_ref[...], preferred_element_type=F32)
    ckvn = _rms(ckv, g_ckv_ref[...])
    k_nope = jnp.dot(ckvn.astype(BF16), w_kup_ref[...], preferred_element_type=F32)
    kr = jnp.dot(h, w_kr_ref[...], preferred_element_type=F32)
    kr = _rope_lanes(kr, cos_b, sin_b)[0]
    for hd in range(B_HEADS):
        kb_ref[:, hd * HEAD_SLOT:(hd + 1) * HEAD_SLOT] = (
            k_nope[:, hd * HEAD_SLOT:(hd + 1) * HEAD_SLOT] + kr).astype(BF16)
    _store_transposed(vb_t_ref, 0, jnp.dot(w_vup_t_ref[...], ckvn.T.astype(BF16), preferred_element_type=F32))


def _attn_kernel(*refs, moba):
    if moba:
        q_t_ref, k_ref, v_t_ref, kmean_ref, o_t_ref = refs[:5]
        s_scr, smax_scr, p_scr, alpha_scr, m_scr, acc_scr, bias_scr, sbias_scr = refs[5:]
    else:
        q_t_ref, k_ref, v_t_ref, o_t_ref = refs[:4]
        s_scr, smax_scr, p_scr, alpha_scr, m_scr, acc_scr = refs[4:]
    own_lo = OWN_BLOCKS * pl.program_id(2)
    heads = HEADS_PER_STEP
    vdim = v_t_ref.shape[1] // heads
    qry_pos = lax.broadcasted_iota(jnp.int32, (1, Q_TILE), 1)

    def key_lanes(hd):
        g = hd // 2 if moba else hd
        return slice(g * LANES, (g + 1) * LANES)

    def q_head(hd, first=0):
        rows = slice(hd * HEAD_SLOT, (hd + 1) * HEAD_SLOT)
        return jnp.concatenate([q_t_ref[u, rows, :] for u in range(first, OWN_BLOCKS)], axis=1)

    if moba:
        nblk = kmean_ref.shape[0]
        blk = lax.broadcasted_iota(jnp.int32, (nblk, Q_TILE), 0).astype(F32)
        own = (own_lo + qry_pos // MOBA_BLOCK).astype(F32)
        past = blk < own
        for hd in range(heads):
            gate = jnp.dot(kmean_ref[:, key_lanes(hd)].astype(BF16), q_head(hd),
                           preferred_element_type=F32)
            gate = jnp.where(past, gate, -jnp.inf)
            bias = jnp.full(gate.shape, MASK_VALUE, F32)
            for _ in range(MOBA_TOPK):
                best = jnp.max(gate, axis=0, keepdims=True)
                first = jnp.min(jnp.where(gate == best, blk, float(nblk)), axis=0, keepdims=True)
                pick = blk == first
                bias = jnp.where(pick, 0.0, bias)
                gate = jnp.where(pick, -jnp.inf, gate)
            bias_scr[hd] = jnp.where(past, bias, MASK_VALUE)


    def qk_stage(j, first=0, own=False):
        start = pl.multiple_of(j * K_TILE, K_TILE)
        k_blk = k_ref[pl.ds(start, K_TILE), :]
        cols = slice(first * MOBA_BLOCK, Q_TILE)
        for hd in range(heads):
            s = jnp.dot(k_blk[:, key_lanes(hd)], q_head(hd, first), preferred_element_type=F32)
            if own:
                key_pos = lax.broadcasted_iota(jnp.int32, s.shape, 0)
                s = jnp.where(key_pos <= qry_pos[:, :s.shape[1]], s, MASK_VALUE)
            s_scr[hd, :, cols] = s
            smax = jnp.max(s, axis=0, keepdims=True)
            if moba:
                bias = bias_scr[hd, pl.ds(j, 1), cols]
                if own:
                    bias = jnp.where(qry_pos[:, :s.shape[1]] >= MOBA_BLOCK, bias, 0.0)
                sbias_scr[hd, :, cols] = bias
                smax = smax + bias
            smax_scr[hd, :, cols] = smax

    ones_rows = jnp.ones((SUM_ROWS, K_TILE), BF16)

    def softmax_stage(first=0):
        cols = slice(first * MOBA_BLOCK, Q_TILE)
        for hd in range(heads):
            m_old = m_scr[hd, :, cols]
            m_new = jnp.maximum(m_old, smax_scr[hd, :, cols])
            shift = m_new - sbias_scr[hd, :, cols] if moba else m_new
            p_scr[hd, :, cols] = jnp.exp2(s_scr[hd, :, cols] - shift).astype(BF16)
            alpha_scr[hd, :, cols] = jnp.exp2(m_old - m_new)
            m_scr[hd, :, cols] = m_new

    def pv_stage(j, first=0):
        v_t = v_t_ref[j]
        cols = slice(first * MOBA_BLOCK, Q_TILE)
        for hd in range(heads):
            v_ext = jnp.concatenate([v_t[hd * vdim:(hd + 1) * vdim], ones_rows], axis=0)
            acc_scr[hd, :, cols] = alpha_scr[hd, :, cols] * acc_scr[hd, :, cols] + jnp.dot(
                v_ext, p_scr[hd, :, cols], preferred_element_type=F32)

    def visited_block(v):
        return jnp.where(v < OWN_BLOCKS, own_lo + (OWN_BLOCKS - 1) - v, v - OWN_BLOCKS)

    m_scr[...] = jnp.full(m_scr.shape, MASK_VALUE, F32)
    acc_scr[...] = jnp.zeros(acc_scr.shape, F32)
    p_scr[:, :, :MOBA_BLOCK] = jnp.zeros(p_scr.shape[:2] + (MOBA_BLOCK,), BF16)
    alpha_scr[:, :, :MOBA_BLOCK] = jnp.ones(alpha_scr.shape[:2] + (MOBA_BLOCK,), F32)
    qk_stage(own_lo + OWN_BLOCKS - 1, OWN_BLOCKS - 1, own=True)
    for v in range(OWN_BLOCKS - 1):
        u = OWN_BLOCKS - 1 - v
        if v > 0:
            pv_stage(own_lo + u + 1, u + 1)
        softmax_stage(u)
        qk_stage(own_lo + u - 1, u - 1, own=True)

    def body(t, carry):
        pv_stage(visited_block(t - 1))
        softmax_stage()
        qk_stage(t + 1 - OWN_BLOCKS)
        return carry

    last = own_lo + OWN_BLOCKS - 1
    lax.fori_loop(OWN_BLOCKS - 1, last, body, 0)
    pv_stage(visited_block(last - 1))
    softmax_stage()
    pv_stage(visited_block(last))

    for hd in range(heads):
        acc = acc_scr[hd]
        o = (acc[:vdim] / acc[vdim:vdim + 1]).astype(o_t_ref.dtype)
        for u in range(OWN_BLOCKS):
            o_t_ref[u, hd * vdim:(hd + 1) * vdim, :] = o[:, u * MOBA_BLOCK:(u + 1) * MOBA_BLOCK]


def _post_kernel(x_ref, oa_t_ref, ob_t_ref, gate_ref, w_oa_ref, w_ob_ref, w_out_ref,
                 g_mlp_ref, w_ff1_ref, w_ff2_ref, g_final_ref, out_ref, *, ff_chunk, final_norm):
    d_model = x_ref.shape[1]
    contract_rows = (((0,), (0,)), ((), ()))
    ya, yb = [], []
    for qt in range(ROW_TILE // K_TILE):
        ya.append(lax.dot_general(oa_t_ref[qt], w_oa_ref[...], contract_rows, preferred_element_type=F32))
        yb.append(lax.dot_general(ob_t_ref[qt], w_ob_ref[...], contract_rows, preferred_element_type=F32))
    ya = ya[0] if len(ya) == 1 else jnp.concatenate(ya, axis=0)
    yb = yb[0] if len(yb) == 1 else jnp.concatenate(yb, axis=0)
    mixed = gate_ref[:, :d_model].astype(F32) * ya + gate_ref[:, d_model:].astype(F32) * yb
    x1 = x_ref[...] + jnp.dot(mixed.astype(BF16), w_out_ref[...], preferred_element_type=F32)

    h2 = _rms(x1, g_mlp_ref[...]).astype(BF16)
    mlp = jnp.zeros_like(x1)
    for c in range(w_ff1_ref.shape[1] // ff_chunk):
        u = jnp.maximum(jnp.dot(h2, w_ff1_ref[:, c * ff_chunk:(c + 1) * ff_chunk],
                                preferred_element_type=F32), 0.0)
        mlp = mlp + jnp.dot((u * u).astype(BF16), w_ff2_ref[c * ff_chunk:(c + 1) * ff_chunk, :],
                            preferred_element_type=F32)
    x2 = x1 + mlp
    out_ref[...] = _rms(x2, g_final_ref[...]) if final_norm else x2


def _resident(shape):
    return pl.BlockSpec(shape, lambda *_: (0,) * len(shape), pipeline_mode=pl.Buffered(1))


def _rope_tables(seq, dim):
    half = dim // 2
    inv_freq = np.power(ROPE_THETA, -np.arange(half, dtype=np.float64) / half)
    ang = np.arange(seq, dtype=np.float64)[:, None] * inv_freq[None, :]
    return np.cos(ang).astype(np.float32), np.sin(ang).astype(np.float32)


def _moba_group_columns():
    half = A_HEAD_DIM // 2
    order = []
    for pair in range(A_HEADS // 2):
        h0, h1 = 2 * pair * A_HEAD_DIM, (2 * pair + 1) * A_HEAD_DIM
        for start in (h0, h1, h0 + half, h1 + half):
            order.extend(range(start, start + half))
    return jnp.asarray(order, jnp.int32)


def _mla_slot_columns(nope, rope, xp=jnp):
    r = B_ROPE // 2
    split = LANES // 2 - r
    pad = xp.zeros(nope.shape[:-1] + (HEAD_SLOT - B_QK,), nope.dtype)
    return xp.concatenate([rope[..., :r], nope[..., :split], rope[..., r:], nope[..., split:], pad], axis=-1)


def _params(*semantics):
    return pltpu.CompilerParams(dimension_semantics=semantics, vmem_limit_bytes=VMEM_LIMIT_BYTES)


def _pre_call(x2, seq, g_mix, w_in, b_gate, g_cq, w_q_up, g_ckv, w_kv_up):
    rows, d_model = x2.shape
    a_width = A_HEADS * A_HEAD_DIM
    o_q, o_k, o_v = 0, a_width, 2 * a_width
    o_cq = 3 * a_width
    o_ckv = o_cq + B_Q_RANK
    o_kr = o_ckv + B_KV_RANK
    o_gate = o_kr + B_ROPE
    w_in = w_in.astype(BF16)
    cols = _moba_group_columns()
    w_q_t = w_in[:, o_q:o_k][:, cols].T
    w_k = w_in[:, o_k:o_v][:, cols]
    w_v_t = w_in[:, o_v:o_cq].T
    w_cq_t = w_in[:, o_cq:o_ckv].T
    w_ckv = w_in[:, o_ckv:o_kr]
    w_kr = _mla_slot_columns(jnp.zeros((d_model, B_NOPE), BF16), w_in[:, o_kr:o_gate])
    w_gate = w_in[:, o_gate:]
    w_q = w_q_up.astype(BF16).reshape(B_Q_RANK, B_HEADS, B_QK)
    w_qup_t = _mla_slot_columns(w_q[..., :B_NOPE], w_q[..., B_NOPE:]).reshape(B_Q_RANK, B_HEADS * HEAD_SLOT).T
    w_kv = w_kv_up.astype(BF16).reshape(B_KV_RANK, B_HEADS, B_NOPE + B_V)
    w_kup = _mla_slot_columns(w_kv[..., :B_NOPE], jnp.zeros((B_KV_RANK, B_HEADS, B_ROPE), BF16)).reshape(
        B_KV_RANK, B_HEADS * HEAD_SLOT)
    w_vup_t = w_kv[:, :, B_NOPE:].reshape(B_KV_RANK, B_HEADS * B_V).T

    cos_a, sin_a = _rope_tables(seq, A_HEAD_DIM)
    cos_a = np.tile(cos_a, (1, LANES // (A_HEAD_DIM // 2)))
    sin_a = np.concatenate([-sin_a, -sin_a, sin_a, sin_a], axis=1)
    cos_b, sin_b = _rope_tables(seq, B_ROPE)
    cos_b = _mla_slot_columns(np.ones((seq, B_NOPE), np.float32), np.concatenate([cos_b, cos_b], axis=1), np)
    sin_b = _mla_slot_columns(np.zeros((seq, B_NOPE), np.float32), np.concatenate([-sin_b, sin_b], axis=1), np)
    tables_t = [np.ascontiguousarray(t.T) for t in (cos_a, sin_a, cos_b, sin_b)]

    n_tiles = rows // ROW_TILE
    seq_tiles = seq // ROW_TILE
    row_spec = lambda w: pl.BlockSpec((ROW_TILE, w), lambda r: (r, 0))
    table_spec = pl.BlockSpec((ROW_TILE, LANES), lambda r: (r % seq_tiles, 0))
    table_t_spec = pl.BlockSpec((LANES, ROW_TILE), lambda r: (0, r % seq_tiles))
    t_spec = lambda h, width: pl.BlockSpec((ROW_TILE // width, h, width), lambda r: (r, 0, 0))
    t_shape = lambda h, width: jax.ShapeDtypeStruct((rows // width, h, width), BF16)
    vec = lambda a: a.reshape(1, -1).astype(F32)
    g_cq_col = jnp.broadcast_to(g_cq.astype(F32)[:, None], (B_Q_RANK, ROW_TILE))
    operands = [
        (x2, row_spec(d_model)), (vec(g_mix), _resident((1, d_model))),
        (cos_a, table_spec), (sin_a, table_spec), (cos_b, table_spec), (sin_b, table_spec),
        *[(t, table_t_spec) for t in tables_t],
        (w_q_t, _resident(w_q_t.shape)), (w_k, _resident(w_k.shape)), (w_v_t, _resident(w_v_t.shape)),
        (w_cq_t, _resident(w_cq_t.shape)), (w_ckv, _resident(w_ckv.shape)),
        (w_kr, _resident(w_kr.shape)), (w_gate, _resident(w_gate.shape)),
        (vec(b_gate), _resident((1, 2 * d_model))),
        (g_cq_col, _resident(g_cq_col.shape)), (w_qup_t, _resident(w_qup_t.shape)),
        (vec(g_ckv), _resident((1, B_KV_RANK))), (w_kup, _resident(w_kup.shape)),
        (w_vup_t, _resident(w_vup_t.shape)),
    ]
    out_shape = [
        t_shape(A_HEADS * HEAD_SLOT, K_TILE),
        jax.ShapeDtypeStruct((rows, a_width), BF16),
        t_shape(a_width, K_TILE),
        jax.ShapeDtypeStruct((n_tiles, ROW_TILE // MOBA_BLOCK, a_width), F32),
        jax.ShapeDtypeStruct((rows, 2 * d_model), BF16),
        t_shape(B_HEADS * HEAD_SLOT, K_TILE),
        jax.ShapeDtypeStruct((rows, B_HEADS * HEAD_SLOT), BF16),
        t_shape(B_HEADS * B_V, K_TILE),
    ]
    out_specs = [
        t_spec(A_HEADS * HEAD_SLOT, K_TILE), row_spec(a_width), t_spec(a_width, K_TILE),
        pl.BlockSpec((1, ROW_TILE // MOBA_BLOCK, a_width), lambda r: (r, 0, 0)),
        row_spec(2 * d_model), t_spec(B_HEADS * HEAD_SLOT, K_TILE), row_spec(B_HEADS * HEAD_SLOT),
        t_spec(B_HEADS * B_V, K_TILE),
    ]
    return pl.pallas_call(
        _pre_kernel,
        grid=(n_tiles,),
        in_specs=[s for _, s in operands],
        out_specs=out_specs,
        out_shape=out_shape,
        compiler_params=_params("parallel"),
        name="pre_proj",
    )(*[a for a, _ in operands])


def _attn_call(q_t, k, v_t, kmean, *, batch, seq, moba):
    nblk = seq // K_TILE
    n_q = seq // Q_TILE
    heads = q_t.shape[1] // HEAD_SLOT
    groups = heads // HEADS_PER_STEP
    vdim = v_t.shape[2] // heads
    key_lanes = k.shape[2] // groups
    kv_mode = None if HEADS_PER_STEP <= 4 else pl.Buffered(1)
    in_specs = [
        pl.BlockSpec((OWN_BLOCKS, HEADS_PER_STEP * HEAD_SLOT, K_TILE), lambda b, g, i: (b * n_q + i, g, 0)),
        pl.BlockSpec((None, seq, key_lanes), lambda b, g, i: (b, 0, g), pipeline_mode=kv_mode),
        pl.BlockSpec((None, nblk, HEADS_PER_STEP * vdim, K_TILE), lambda b, g, i: (b, 0, g, 0),
                     pipeline_mode=kv_mode),
    ]
    operands = [q_t, k, v_t]
    stat = pltpu.VMEM((HEADS_PER_STEP, 1, Q_TILE), F32)
    scratch = [
        pltpu.VMEM((HEADS_PER_STEP, K_TILE, Q_TILE), F32),
        stat,
        pltpu.VMEM((HEADS_PER_STEP, K_TILE, Q_TILE), BF16),
        stat,
        stat,
        pltpu.VMEM((HEADS_PER_STEP, vdim + SUM_ROWS, Q_TILE), F32),
    ]
    if moba:
        in_specs.append(pl.BlockSpec((None, nblk, key_lanes), lambda b, g, i: (b, 0, g)))
        operands.append(kmean)
        scratch.append(pltpu.VMEM((HEADS_PER_STEP, nblk, Q_TILE), F32))
        scratch.append(stat)
    return pl.pallas_call(
        functools.partial(_attn_kernel, moba=moba),
        grid=(batch, groups, n_q),
        in_specs=in_specs,
        out_specs=pl.BlockSpec((OWN_BLOCKS, HEADS_PER_STEP * vdim, K_TILE), lambda b, g, i: (b * n_q + i, g, 0)),
        out_shape=jax.ShapeDtypeStruct((batch * nblk, heads * vdim, K_TILE), BF16),
        scratch_shapes=scratch,
        compiler_params=_params("parallel", "parallel", "arbitrary"),
        name="moba_attn" if moba else "mla_attn",
    )(*operands)


def _post_call(x2, oa_t, ob_t, gates, w_o_a, w_o_b, w_out, g_mlp, w_ff1, w_ff2, g_final, final_norm):
    rows, d_model = x2.shape
    row_spec = lambda w: pl.BlockSpec((ROW_TILE, w), lambda r: (r, 0))
    t_spec = lambda a: pl.BlockSpec((ROW_TILE // K_TILE,) + a.shape[1:], lambda r: (r, 0, 0))
    vec = lambda a: a.reshape(1, -1).astype(F32)
    weights = [w.astype(BF16) for w in (w_o_a, w_o_b, w_out)]
    operands = [(x2, row_spec(d_model)), (oa_t, t_spec(oa_t)), (ob_t, t_spec(ob_t)),
                (gates, row_spec(2 * d_model))]
    operands += [(w, _resident(w.shape)) for w in weights]
    operands += [(vec(g_mlp), _resident((1, d_model))),
                 (w_ff1.astype(BF16), _resident(w_ff1.shape)), (w_ff2.astype(BF16), _resident(w_ff2.shape)),
                 (vec(g_final), _resident((1, d_model)))]
    return pl.pallas_call(
        functools.partial(_post_kernel, ff_chunk=FF_CHUNK, final_norm=final_norm),
        grid=(rows // ROW_TILE,),
        in_specs=[s for _, s in operands],
        out_specs=row_spec(d_model),
        out_shape=jax.ShapeDtypeStruct((rows, d_model), F32),
        compiler_params=_params("parallel"),
        name="post_mlp",
    )(*[a for a, _ in operands])


def kernel(x, g_mix, w_in, b_gate, g_cq, w_q_up, g_ckv, w_kv_up, w_o_a, w_o_b, w_out, g_mlp, w_ff1,
           w_ff2, g_final):
    batch, seq, d_model = x.shape
    depth = w_in.shape[0]
    assert seq % ROW_TILE == 0 and ROW_TILE % MOBA_BLOCK == 0 and seq % Q_TILE == 0 and OWN_BLOCKS >= 2
    assert A_HEADS % HEADS_PER_STEP == 0 and B_HEADS % HEADS_PER_STEP == 0 and HEADS_PER_STEP % 2 == 0
    nblk = seq // K_TILE
    x2 = x.reshape(batch * seq, d_model)
    for l in range(depth):
        qa_t, ka, va_t, kmean, gates, qb_t, kb, vb_t = _pre_call(
            x2, seq, g_mix[l], w_in[l], b_gate[l], g_cq[l], w_q_up[l], g_ckv[l], w_kv_up[l])
        oa_t = _attn_call(qa_t, ka.reshape(batch, seq, -1), va_t.reshape(batch, nblk, -1, K_TILE),
                          kmean.reshape(batch, nblk, -1), batch=batch, seq=seq, moba=True)
        ob_t = _attn_call(qb_t, kb.reshape(batch, seq, -1), vb_t.reshape(batch, nblk, -1, K_TILE),
                          None, batch=batch, seq=seq, moba=False)
        x2 = _post_call(x2, oa_t, ob_t, gates, w_o_a[l], w_o_b[l], w_out[l], g_mlp[l], w_ff1[l],
                        w_ff2[l], g_final, final_norm=(l == depth - 1))
    return x2.reshape(batch, seq, d_model)
```

```python
import functools

import jax
import jax.numpy as jnp
import numpy as np
from jax import lax
from jax.experimental import pallas as pl
from jax.experimental.pallas import tpu as pltpu

EPS = 1e-6
ROPE_THETA = 10000.0

A_HEADS = 8
A_HEAD_DIM = 64
MOBA_BLOCK = 256
MOBA_TOPK = 3

B_HEADS = 8
B_NOPE = 64
B_ROPE = 32
B_QK = B_NOPE + B_ROPE
B_V = 64
B_Q_RANK = 384
B_KV_RANK = 256

LANES = 128
HEAD_SLOT = 128
ROW_TILE = 512
FF_CHUNK = 1024
K_TILE = MOBA_BLOCK
MOBA_STEP = (8, 4)
MLA_STEP = (16, 2)
MASK_VALUE = -1e30
LOG2_E = 1.4426950408889634
SUM_ROWS = 16
VMEM_LIMIT_BYTES = 56 * 1024 * 1024

F32 = jnp.float32
BF16 = jnp.bfloat16


def _rms(x, g):
    return x * lax.rsqrt(jnp.mean(x * x, axis=-1, keepdims=True) + EPS) * g


def _rope_lanes(x, cos, sin_signed):
    groups = []
    for g in range(x.shape[1] // LANES):
        xg = x[:, g * LANES:(g + 1) * LANES]
        groups.append(xg * cos + pltpu.roll(xg, LANES // 2, axis=1) * sin_signed)
    return groups


def _store_transposed(dst_ref, row0, t):
    rows, width = t.shape[0], dst_ref.shape[2]
    for tile in range(ROW_TILE // width):
        dst_ref[tile, row0:row0 + rows, :] = t[:, tile * width:(tile + 1) * width].astype(dst_ref.dtype)


def _rope_rows(x_t, cos_t, sin_signed_t):
    groups = []
    for g in range(x_t.shape[0] // LANES):
        xg = x_t[g * LANES:(g + 1) * LANES]
        partner = jnp.concatenate([xg[LANES // 2:], xg[:LANES // 2]], axis=0)
        groups.append(xg * cos_t + partner * sin_signed_t)
    return groups


def _pre_kernel(x_ref, gmix_ref, cos_a_ref, sin_a_ref, cos_b_ref, sin_b_ref,
                cos_at_ref, sin_at_ref, cos_bt_ref, sin_bt_ref,
                w_q_t_ref, w_k_ref, w_v_t_ref, w_cq_t_ref, w_ckv_ref, w_kr_ref, w_gate_ref, b_gate_ref,
                g_cq_ref, w_qup_t_ref, g_ckv_ref, w_kup_ref, w_vup_t_ref,
                qa_t_ref, ka_ref, va_t_ref, kmean_ref, gate_ref, qb_t_ref, kb_ref, vb_t_ref):
    xn = _rms(x_ref[...], gmix_ref[...])
    h = xn.astype(BF16)
    h_t = xn.T.astype(BF16)

    q_t = jnp.dot(w_q_t_ref[...], h_t, preferred_element_type=F32)
    first_head = (lax.broadcasted_iota(jnp.int32, (LANES, ROW_TILE), 0) % A_HEAD_DIM) < A_HEAD_DIM // 2
    for g, qg in enumerate(_rope_rows(q_t, cos_at_ref[...], sin_at_ref[...])):
        t = qg * (A_HEAD_DIM ** -0.5 * LOG2_E)
        _store_transposed(qa_t_ref, (2 * g) * HEAD_SLOT, jnp.where(first_head, t, 0.0))
        _store_transposed(qa_t_ref, (2 * g + 1) * HEAD_SLOT, jnp.where(first_head, 0.0, t))
    k = jnp.dot(h, w_k_ref[...], preferred_element_type=F32)
    for g, kg in enumerate(_rope_lanes(k, cos_a_ref[...], sin_a_ref[...])):
        ka_ref[:, g * LANES:(g + 1) * LANES] = kg.astype(BF16)
        for blk in range(ROW_TILE // MOBA_BLOCK):
            kmean_ref[0, blk:blk + 1, g * LANES:(g + 1) * LANES] = jnp.mean(
                kg[blk * MOBA_BLOCK:(blk + 1) * MOBA_BLOCK], axis=0, keepdims=True)
    _store_transposed(va_t_ref, 0, jnp.dot(w_v_t_ref[...], h_t, preferred_element_type=F32))

    gate = jnp.dot(h, w_gate_ref[...], preferred_element_type=F32) + b_gate_ref[...]
    gate_ref[...] = jax.nn.sigmoid(gate).astype(BF16)

    cq_t = jnp.dot(w_cq_t_ref[...], h_t, preferred_element_type=F32)
    cqn_t = cq_t * lax.rsqrt(jnp.mean(cq_t * cq_t, axis=0, keepdims=True) + EPS) * g_cq_ref[...]
    qb_t = jnp.dot(w_qup_t_ref[...], cqn_t.astype(BF16), preferred_element_type=F32)
    for hd, qg in enumerate(_rope_rows(qb_t, cos_bt_ref[...], sin_bt_ref[...])):
        _store_transposed(qb_t_ref, hd * HEAD_SLOT, qg * (B_QK ** -0.5 * LOG2_E))

    cos_b, sin_b = cos_b_ref[...], sin_b_ref[...]
    ckv = jnp.dot(h, w_ckv_ref[...], preferred_element_type=F32)
    ckvn = _rms(ckv, g_ckv_ref[...])
    k_nope = jnp.dot(ckvn.astype(BF16), w_kup_ref[...], preferred_element_type=F32)
    kr = jnp.dot(h, w_kr_ref[...], preferred_element_type=F32)
    kr = _rope_lanes(kr, cos_b, sin_b)[0]
    for hd in range(B_HEADS):
        kb_ref[:, hd * HEAD_SLOT:(hd + 1) * HEAD_SLOT] = (
            k_nope[:, hd * HEAD_SLOT:(hd + 1) * HEAD_SLOT] + kr).astype(BF16)
    _store_transposed(vb_t_ref, 0, jnp.dot(w_vup_t_ref[...], ckvn.T.astype(BF16), preferred_element_type=F32))


def _attn_kernel(*refs, moba, own_blocks, heads_per_step):
    if moba:
        q_t_ref, k_ref, v_t_ref, kmean_ref, o_t_ref = refs[:5]
        s_scr, smax_scr, p_scr, alpha_scr, m_scr, acc_scr, bias_scr, sbias_scr = refs[5:]
    else:
        q_t_ref, k_ref, v_t_ref, o_t_ref = refs[:4]
        s_scr, smax_scr, p_scr, alpha_scr, m_scr, acc_scr = refs[4:]
    own_lo = own_blocks * pl.program_id(2)
    q_tile = own_blocks * MOBA_BLOCK
    heads = heads_per_step
    vdim = v_t_ref.shape[1] // heads
    qry_pos = lax.broadcasted_iota(jnp.int32, (1, q_tile), 1)

    def key_lanes(hd):
        g = hd // 2 if moba else hd
        return slice(g * LANES, (g + 1) * LANES)

    def q_head(hd, first=0):
        rows = slice(hd * HEAD_SLOT, (hd + 1) * HEAD_SLOT)
        return jnp.concatenate([q_t_ref[u, rows, :] for u in range(first, own_blocks)], axis=1)

    if moba:
        nblk = kmean_ref.shape[0]
        blk = lax.broadcasted_iota(jnp.int32, (nblk, q_tile), 0).astype(F32)
        own = (own_lo + qry_pos // MOBA_BLOCK).astype(F32)
        past = blk < own
        for hd in range(heads):
            gate = jnp.dot(kmean_ref[:, key_lanes(hd)].astype(BF16), q_head(hd),
                           preferred_element_type=F32)
            gate = jnp.where(past, gate, -jnp.inf)
            bias = jnp.full(gate.shape, MASK_VALUE, F32)
            for _ in range(MOBA_TOPK):
                best = jnp.max(gate, axis=0, keepdims=True)
                first = jnp.min(jnp.where(gate == best, blk, float(nblk)), axis=0, keepdims=True)
                pick = blk == first
                bias = jnp.where(pick, 0.0, bias)
                gate = jnp.where(pick, -jnp.inf, gate)
            bias_scr[hd] = jnp.where(past, bias, MASK_VALUE)


    def qk_stage(j, first=0, own=False):
        start = pl.multiple_of(j * K_TILE, K_TILE)
        k_blk = k_ref[pl.ds(start, K_TILE), :]
        cols = slice(first * MOBA_BLOCK, q_tile)
        for hd in range(heads):
            s = jnp.dot(k_blk[:, key_lanes(hd)], q_head(hd, first), preferred_element_type=F32)
            if own:
                key_pos = lax.broadcasted_iota(jnp.int32, s.shape, 0)
                s = jnp.where(key_pos <= qry_pos[:, :s.shape[1]], s, MASK_VALUE)
            s_scr[hd, :, cols] = s
            smax = jnp.max(s, axis=0, keepdims=True)
            if moba:
                bias = bias_scr[hd, pl.ds(j, 1), cols]
                if own:
                    bias = jnp.where(qry_pos[:, :s.shape[1]] >= MOBA_BLOCK, bias, 0.0)
                sbias_scr[hd, :, cols] = bias
                smax = smax + bias
            smax_scr[hd, :, cols] = smax

    ones_rows = jnp.ones((SUM_ROWS, K_TILE), BF16)

    def softmax_stage(first=0):
        cols = slice(first * MOBA_BLOCK, q_tile)
        for hd in range(heads):
            m_old = m_scr[hd, :, cols]
            m_new = jnp.maximum(m_old, smax_scr[hd, :, cols])
            shift = m_new - sbias_scr[hd, :, cols] if moba else m_new
            p_scr[hd, :, cols] = jnp.exp2(s_scr[hd, :, cols] - shift).astype(BF16)
            alpha_scr[hd, :, cols] = jnp.exp2(m_old - m_new)
            m_scr[hd, :, cols] = m_new

    def pv_stage(j, first=0):
        v_t = v_t_ref[j]
        cols = slice(first * MOBA_BLOCK, q_tile)
        for hd in range(heads):
            v_ext = jnp.concatenate([v_t[hd * vdim:(hd + 1) * vdim], ones_rows], axis=0)
            acc_scr[hd, :, cols] = alpha_scr[hd, :, cols] * acc_scr[hd, :, cols] + jnp.dot(
                v_ext, p_scr[hd, :, cols], preferred_element_type=F32)

    def visited_block(v):
        return jnp.where(v < own_blocks, own_lo + (own_blocks - 1) - v, v - own_blocks)

    m_scr[...] = jnp.full(m_scr.shape, MASK_VALUE, F32)
    acc_scr[...] = jnp.zeros(acc_scr.shape, F32)
    p_scr[:, :, :MOBA_BLOCK] = jnp.zeros(p_scr.shape[:2] + (MOBA_BLOCK,), BF16)
    alpha_scr[:, :, :MOBA_BLOCK] = jnp.ones(alpha_scr.shape[:2] + (MOBA_BLOCK,), F32)
    qk_stage(own_lo + own_blocks - 1, own_blocks - 1, own=True)
    for v in range(own_blocks - 1):
        u = own_blocks - 1 - v
        if v > 0:
            pv_stage(own_lo + u + 1, u + 1)
        softmax_stage(u)
        qk_stage(own_lo + u - 1, u - 1, own=True)

    def body(t, carry):
        pv_stage(visited_block(t - 1))
        softmax_stage()
        qk_stage(t + 1 - own_blocks)
        return carry

    last = own_lo + own_blocks - 1
    lax.fori_loop(own_blocks - 1, last, body, 0)
    pv_stage(visited_block(last - 1))
    softmax_stage()
    pv_stage(visited_block(last))

    for hd in range(heads):
        acc = acc_scr[hd]
        o = (acc[:vdim] / acc[vdim:vdim + 1]).astype(o_t_ref.dtype)
        for u in range(own_blocks):
            o_t_ref[u, hd * vdim:(hd + 1) * vdim, :] = o[:, u * MOBA_BLOCK:(u + 1) * MOBA_BLOCK]


def _post_kernel(x_ref, oa_t_ref, ob_t_ref, gate_ref, w_oa_ref, w_ob_ref, w_out_ref,
                 g_mlp_ref, w_ff1_ref, w_ff2_ref, g_final_ref, out_ref, *, ff_chunk, final_norm):
    d_model = x_ref.shape[1]
    contract_rows = (((0,), (0,)), ((), ()))
    ya, yb = [], []
    for qt in range(ROW_TILE // K_TILE):
        ya.append(lax.dot_general(oa_t_ref[qt], w_oa_ref[...], contract_rows, preferred_element_type=F32))
        yb.append(lax.dot_general(ob_t_ref[qt], w_ob_ref[...], contract_rows, preferred_element_type=F32))
    ya = ya[0] if len(ya) == 1 else jnp.concatenate(ya, axis=0)
    yb = yb[0] if len(yb) == 1 else jnp.concatenate(yb, axis=0)
    mixed = gate_ref[:, :d_model].astype(F32) * ya + gate_ref[:, d_model:].astype(F32) * yb
    x1 = x_ref[...] + jnp.dot(mixed.astype(BF16), w_out_ref[...], preferred_element_type=F32)

    h2 = _rms(x1, g_mlp_ref[...]).astype(BF16)
    mlp = jnp.zeros_like(x1)
    for c in range(w_ff1_ref.shape[1] // ff_chunk):
        u = jnp.maximum(jnp.dot(h2, w_ff1_ref[:, c * ff_chunk:(c + 1) * ff_chunk],
                                preferred_element_type=F32), 0.0)
        mlp = mlp + jnp.dot((u * u).astype(BF16), w_ff2_ref[c * ff_chunk:(c + 1) * ff_chunk, :],
                            preferred_element_type=F32)
    x2 = x1 + mlp
    out_ref[...] = _rms(x2, g_final_ref[...]) if final_norm else x2


def _resident(shape):
    return pl.BlockSpec(shape, lambda *_: (0,) * len(shape), pipeline_mode=pl.Buffered(1))


def _rope_tables(seq, dim):
    half = dim // 2
    inv_freq = np.power(ROPE_THETA, -np.arange(half, dtype=np.float64) / half)
    ang = np.arange(seq, dtype=np.float64)[:, None] * inv_freq[None, :]
    return np.cos(ang).astype(np.float32), np.sin(ang).astype(np.float32)


def _moba_group_columns():
    half = A_HEAD_DIM // 2
    order = []
    for pair in range(A_HEADS // 2):
        h0, h1 = 2 * pair * A_HEAD_DIM, (2 * pair + 1) * A_HEAD_DIM
        for start in (h0, h1, h0 + half, h1 + half):
            order.extend(range(start, start + half))
    return jnp.asarray(order, jnp.int32)


def _mla_slot_columns(nope, rope, xp=jnp):
    r = B_ROPE // 2
    split = LANES // 2 - r
    pad = xp.zeros(nope.shape[:-1] + (HEAD_SLOT - B_QK,), nope.dtype)
    return xp.concatenate([rope[..., :r], nope[..., :split], rope[..., r:], nope[..., split:], pad], axis=-1)


def _params(*semantics):
    return pltpu.CompilerParams(dimension_semantics=semantics, vmem_limit_bytes=VMEM_LIMIT_BYTES)


def _pre_call(x2, seq, g_mix, w_in, b_gate, g_cq, w_q_up, g_ckv, w_kv_up):
    rows, d_model = x2.shape
    a_width = A_HEADS * A_HEAD_DIM
    o_q, o_k, o_v = 0, a_width, 2 * a_width
    o_cq = 3 * a_width
    o_ckv = o_cq + B_Q_RANK
    o_kr = o_ckv + B_KV_RANK
    o_gate = o_kr + B_ROPE
    w_in = w_in.astype(BF16)
    cols = _moba_group_columns()
    w_q_t = w_in[:, o_q:o_k][:, cols].T
    w_k = w_in[:, o_k:o_v][:, cols]
    w_v_t = w_in[:, o_v:o_cq].T
    w_cq_t = w_in[:, o_cq:o_ckv].T
    w_ckv = w_in[:, o_ckv:o_kr]
    w_kr = _mla_slot_columns(jnp.zeros((d_model, B_NOPE), BF16), w_in[:, o_kr:o_gate])
    w_gate = w_in[:, o_gate:]
    w_q = w_q_up.astype(BF16).reshape(B_Q_RANK, B_HEADS, B_QK)
    w_qup_t = _mla_slot_columns(w_q[..., :B_NOPE], w_q[..., B_NOPE:]).reshape(B_Q_RANK, B_HEADS * HEAD_SLOT).T
    w_kv = w_kv_up.astype(BF16).reshape(B_KV_RANK, B_HEADS, B_NOPE + B_V)
    w_kup = _mla_slot_columns(w_kv[..., :B_NOPE], jnp.zeros((B_KV_RANK, B_HEADS, B_ROPE), BF16)).reshape(
        B_KV_RANK, B_HEADS * HEAD_SLOT)
    w_vup_t = w_kv[:, :, B_NOPE:].reshape(B_KV_RANK, B_HEADS * B_V).T

    cos_a, sin_a = _rope_tables(seq, A_HEAD_DIM)
    cos_a = np.tile(cos_a, (1, LANES // (A_HEAD_DIM // 2)))
    sin_a = np.concatenate([-sin_a, -sin_a, sin_a, sin_a], axis=1)
    cos_b, sin_b = _rope_tables(seq, B_ROPE)
    cos_b = _mla_slot_columns(np.ones((seq, B_NOPE), np.float32), np.concatenate([cos_b, cos_b], axis=1), np)
    sin_b = _mla_slot_columns(np.zeros((seq, B_NOPE), np.float32), np.concatenate([-sin_b, sin_b], axis=1), np)
    tables_t = [np.ascontiguousarray(t.T) for t in (cos_a, sin_a, cos_b, sin_b)]

    n_tiles = rows // ROW_TILE
    seq_tiles = seq // ROW_TILE
    row_spec = lambda w: pl.BlockSpec((ROW_TILE, w), lambda r: (r, 0))
    table_spec = pl.BlockSpec((ROW_TILE, LANES), lambda r: (r % seq_tiles, 0))
    table_t_spec = pl.BlockSpec((LANES, ROW_TILE), lambda r: (0, r % seq_tiles))
    t_spec = lambda h, width: pl.BlockSpec((ROW_TILE // width, h, width), lambda r: (r, 0, 0))
    t_shape = lambda h, width: jax.ShapeDtypeStruct((rows // width, h, width), BF16)
    vec = lambda a: a.reshape(1, -1).astype(F32)
    g_cq_col = jnp.broadcast_to(g_cq.astype(F32)[:, None], (B_Q_RANK, ROW_TILE))
    operands = [
        (x2, row_spec(d_model)), (vec(g_mix), _resident((1, d_model))),
        (cos_a, table_spec), (sin_a, table_spec), (cos_b, table_spec), (sin_b, table_spec),
        *[(t, table_t_spec) for t in tables_t],
        (w_q_t, _resident(w_q_t.shape)), (w_k, _resident(w_k.shape)), (w_v_t, _resident(w_v_t.shape)),
        (w_cq_t, _resident(w_cq_t.shape)), (w_ckv, _resident(w_ckv.shape)),
        (w_kr, _resident(w_kr.shape)), (w_gate, _resident(w_gate.shape)),
        (vec(b_gate), _resident((1, 2 * d_model))),
        (g_cq_col, _resident(g_cq_col.shape)), (w_qup_t, _resident(w_qup_t.shape)),
        (vec(g_ckv), _resident((1, B_KV_RANK))), (w_kup, _resident(w_kup.shape)),
        (w_vup_t, _resident(w_vup_t.shape)),
    ]
    out_shape = [
        t_shape(A_HEADS * HEAD_SLOT, K_TILE),
        jax.ShapeDtypeStruct((rows, a_width), BF16),
        t_shape(a_width, K_TILE),
        jax.ShapeDtypeStruct((n_tiles, ROW_TILE // MOBA_BLOCK, a_width), F32),
        jax.ShapeDtypeStruct((rows, 2 * d_model), BF16),
        t_shape(B_HEADS * HEAD_SLOT, K_TILE),
        jax.ShapeDtypeStruct((rows, B_HEADS * HEAD_SLOT), BF16),
        t_shape(B_HEADS * B_V, K_TILE),
    ]
    out_specs = [
        t_spec(A_HEADS * HEAD_SLOT, K_TILE), row_spec(a_width), t_spec(a_width, K_TILE),
        pl.BlockSpec((1, ROW_TILE // MOBA_BLOCK, a_width), lambda r: (r, 0, 0)),
        row_spec(2 * d_model), t_spec(B_HEADS * HEAD_SLOT, K_TILE), row_spec(B_HEADS * HEAD_SLOT),
        t_spec(B_HEADS * B_V, K_TILE),
    ]
    return pl.pallas_call(
        _pre_kernel,
        grid=(n_tiles,),
        in_specs=[s for _, s in operands],
        out_specs=out_specs,
        out_shape=out_shape,
        compiler_params=_params("parallel"),
        name="pre_proj",
    )(*[a for a, _ in operands])


def _attn_call(q_t, k, v_t, kmean, *, batch, seq, moba):
    own_blocks, heads_per_step = MOBA_STEP if moba else MLA_STEP
    q_tile = own_blocks * MOBA_BLOCK
    nblk = seq // K_TILE
    n_q = seq // q_tile
    heads = q_t.shape[1] // HEAD_SLOT
    groups = heads // heads_per_step
    vdim = v_t.shape[2] // heads
    key_lanes = k.shape[2] // groups
    kv_mode = None if heads_per_step <= 4 else pl.Buffered(1)
    in_specs = [
        pl.BlockSpec((own_blocks, heads_per_step * HEAD_SLOT, K_TILE), lambda b, g, i: (b * n_q + i, g, 0)),
        pl.BlockSpec((None, seq, key_lanes), lambda b, g, i: (b, 0, g), pipeline_mode=kv_mode),
        pl.BlockSpec((None, nblk, heads_per_step * vdim, K_TILE), lambda b, g, i: (b, 0, g, 0),
                     pipeline_mode=kv_mode),
    ]
    operands = [q_t, k, v_t]
    stat = pltpu.VMEM((heads_per_step, 1, q_tile), F32)
    scratch = [
        pltpu.VMEM((heads_per_step, K_TILE, q_tile), F32),
        stat,
        pltpu.VMEM((heads_per_step, K_TILE, q_tile), BF16),
        stat,
        stat,
        pltpu.VMEM((heads_per_step, vdim + SUM_ROWS, q_tile), F32),
    ]
    if moba:
        in_specs.append(pl.BlockSpec((None, nblk, key_lanes), lambda b, g, i: (b, 0, g)))
        operands.append(kmean)
        scratch.append(pltpu.VMEM((heads_per_step, nblk, q_tile), F32))
        scratch.append(stat)
    return pl.pallas_call(
        functools.partial(_attn_kernel, moba=moba, own_blocks=own_blocks, heads_per_step=heads_per_step),
        grid=(batch, groups, n_q),
        in_specs=in_specs,
        out_specs=pl.BlockSpec((own_blocks, heads_per_step * vdim, K_TILE), lambda b, g, i: (b * n_q + i, g, 0)),
        out_shape=jax.ShapeDtypeStruct((batch * nblk, heads * vdim, K_TILE), BF16),
        scratch_shapes=scratch,
        compiler_params=_params("parallel", "parallel", "arbitrary"),
        name="moba_attn" if moba else "mla_attn",
    )(*operands)


def _post_call(x2, oa_t, ob_t, gates, w_o_a, w_o_b, w_out, g_mlp, w_ff1, w_ff2, g_final, final_norm):
    rows, d_model = x2.shape
    row_spec = lambda w: pl.BlockSpec((ROW_TILE, w), lambda r: (r, 0))
    t_spec = lambda a: pl.BlockSpec((ROW_TILE // K_TILE,) + a.shape[1:], lambda r: (r, 0, 0))
    vec = lambda a: a.reshape(1, -1).astype(F32)
    weights = [w.astype(BF16) for w in (w_o_a, w_o_b, w_out)]
    operands = [(x2, row_spec(d_model)), (oa_t, t_spec(oa_t)), (ob_t, t_spec(ob_t)),
                (gates, row_spec(2 * d_model))]
    operands += [(w, _resident(w.shape)) for w in weights]
    operands += [(vec(g_mlp), _resident((1, d_model))),
                 (w_ff1.astype(BF16), _resident(w_ff1.shape)), (w_ff2.astype(BF16), _resident(w_ff2.shape)),
                 (vec(g_final), _resident((1, d_model)))]
    return pl.pallas_call(
        functools.partial(_post_kernel, ff_chunk=FF_CHUNK, final_norm=final_norm),
        grid=(rows // ROW_TILE,),
        in_specs=[s for _, s in operands],
        out_specs=row_spec(d_model),
        out_shape=jax.ShapeDtypeStruct((rows, d_model), F32),
        compiler_params=_params("parallel"),
        name="post_mlp",
    )(*[a for a, _ in operands])


def kernel(x, g_mix, w_in, b_gate, g_cq, w_q_up, g_ckv, w_kv_up, w_o_a, w_o_b, w_out, g_mlp, w_ff1,
           w_ff2, g_final):
    batch, seq, d_model = x.shape
    depth = w_in.shape[0]
    assert seq % ROW_TILE == 0 and ROW_TILE % MOBA_BLOCK == 0
    for heads, (own_blocks, heads_per_step) in ((A_HEADS, MOBA_STEP), (B_HEADS, MLA_STEP)):
        assert seq % (own_blocks * MOBA_BLOCK) == 0 and own_blocks >= 2
        assert heads % heads_per_step == 0 and heads_per_step % 2 == 0
    nblk = seq // K_TILE
    x2 = x.reshape(batch * seq, d_model)
    for l in range(depth):
        qa_t, ka, va_t, kmean, gates, qb_t, kb, vb_t = _pre_call(
            x2, seq, g_mix[l], w_in[l], b_gate[l], g_cq[l], w_q_up[l], g_ckv[l], w_kv_up[l])
        oa_t = _attn_call(qa_t, ka.reshape(batch, seq, -1), va_t.reshape(batch, nblk, -1, K_TILE),
                          kmean.reshape(batch, nblk, -1), batch=batch, seq=seq, moba=True)
        ob_t = _attn_call(qb_t, kb.reshape(batch, seq, -1), vb_t.reshape(batch, nblk, -1, K_TILE),
                          None, batch=batch, seq=seq, moba=False)
        x2 = _post_call(x2, oa_t, ob_t, gates, w_o_a[l], w_o_b[l], w_out[l], g_mlp[l], w_ff1[l],
                        w_ff2[l], g_final, final_norm=(l == depth - 1))
    return x2.reshape(batch, seq, d_model)
```

```python
import functools

import jax
import jax.numpy as jnp
import numpy as np
from jax import lax
from jax.experimental import pallas as pl
from jax.experimental.pallas import tpu as pltpu

EPS = 1e-6
ROPE_THETA = 10000.0

A_HEADS = 8
A_HEAD_DIM = 64
MOBA_BLOCK = 256
MOBA_TOPK = 3

B_HEADS = 8
B_NOPE = 64
B_ROPE = 32
B_QK = B_NOPE + B_ROPE
B_V = 64
B_Q_RANK = 384
B_KV_RANK = 256

LANES = 128
HEAD_SLOT = 128
ROW_TILE = 512
FF_CHUNK = 1024
K_TILE = MOBA_BLOCK
MOBA_STEP = (8, 4)
MLA_STEP = (32, 1)
MASK_VALUE = -1e30
LOG2_E = 1.4426950408889634
SUM_ROWS = 16
VMEM_LIMIT_BYTES = 56 * 1024 * 1024

F32 = jnp.float32
BF16 = jnp.bfloat16


def _rms(x, g):
    return x * lax.rsqrt(jnp.mean(x * x, axis=-1, keepdims=True) + EPS) * g


def _rope_lanes(x, cos, sin_signed):
    groups = []
    for g in range(x.shape[1] // LANES):
        xg = x[:, g * LANES:(g + 1) * LANES]
        groups.append(xg * cos + pltpu.roll(xg, LANES // 2, axis=1) * sin_signed)
    return groups


def _store_transposed(dst_ref, row0, t):
    rows, width = t.shape[0], dst_ref.shape[2]
    for tile in range(ROW_TILE // width):
        dst_ref[tile, row0:row0 + rows, :] = t[:, tile * width:(tile + 1) * width].astype(dst_ref.dtype)


def _rope_rows(x_t, cos_t, sin_signed_t):
    groups = []
    for g in range(x_t.shape[0] // LANES):
        xg = x_t[g * LANES:(g + 1) * LANES]
        partner = jnp.concatenate([xg[LANES // 2:], xg[:LANES // 2]], axis=0)
        groups.append(xg * cos_t + partner * sin_signed_t)
    return groups


def _pre_kernel(x_ref, gmix_ref, cos_a_ref, sin_a_ref, cos_b_ref, sin_b_ref,
                cos_at_ref, sin_at_ref, cos_bt_ref, sin_bt_ref,
                w_q_t_ref, w_k_ref, w_v_t_ref, w_cq_t_ref, w_ckv_ref, w_kr_ref, w_gate_ref, b_gate_ref,
                g_cq_ref, w_qup_t_ref, g_ckv_ref, w_kup_ref, w_vup_t_ref,
                qa_t_ref, ka_ref, va_t_ref, kmean_ref, gate_ref, qb_t_ref, kb_ref, vb_t_ref):
    xn = _rms(x_ref[...], gmix_ref[...])
    h = xn.astype(BF16)
    h_t = xn.T.astype(BF16)

    q_t = jnp.dot(w_q_t_ref[...], h_t, preferred_element_type=F32)
    first_head = (lax.broadcasted_iota(jnp.int32, (LANES, ROW_TILE), 0) % A_HEAD_DIM) < A_HEAD_DIM // 2
    for g, qg in enumerate(_rope_rows(q_t, cos_at_ref[...], sin_at_ref[...])):
        t = qg * (A_HEAD_DIM ** -0.5 * LOG2_E)
        _store_transposed(qa_t_ref, (2 * g) * HEAD_SLOT, jnp.where(first_head, t, 0.0))
        _store_transposed(qa_t_ref, (2 * g + 1) * HEAD_SLOT, jnp.where(first_head, 0.0, t))
    k = jnp.dot(h, w_k_ref[...], preferred_element_type=F32)
    for g, kg in enumerate(_rope_lanes(k, cos_a_ref[...], sin_a_ref[...])):
        ka_ref[:, g * LANES:(g + 1) * LANES] = kg.astype(BF16)
        for blk in range(ROW_TILE // MOBA_BLOCK):
            kmean_ref[0, blk:blk + 1, g * LANES:(g + 1) * LANES] = jnp.mean(
                kg[blk * MOBA_BLOCK:(blk + 1) * MOBA_BLOCK], axis=0, keepdims=True)
    _store_transposed(va_t_ref, 0, jnp.dot(w_v_t_ref[...], h_t, preferred_element_type=F32))

    gate = jnp.dot(h, w_gate_ref[...], preferred_element_type=F32) + b_gate_ref[...]
    gate_ref[...] = jax.nn.sigmoid(gate).astype(BF16)

    cq_t = jnp.dot(w_cq_t_ref[...], h_t, preferred_element_type=F32)
    cqn_t = cq_t * lax.rsqrt(jnp.mean(cq_t * cq_t, axis=0, keepdims=True) + EPS) * g_cq_ref[...]
    qb_t = jnp.dot(w_qup_t_ref[...], cqn_t.astype(BF16), preferred_element_type=F32)
    for hd, qg in enumerate(_rope_rows(qb_t, cos_bt_ref[...], sin_bt_ref[...])):
        _store_transposed(qb_t_ref, hd * HEAD_SLOT, qg * (B_QK ** -0.5 * LOG2_E))

    cos_b, sin_b = cos_b_ref[...], sin_b_ref[...]
    ckv = jnp.dot(h, w_ckv_ref[...], preferred_element_type=F32)
    ckvn = _rms(ckv, g_ckv_ref[...])
    k_nope = jnp.dot(ckvn.astype(BF16), w_kup_ref[...], preferred_element_type=F32)
    kr = jnp.dot(h, w_kr_ref[...], preferred_element_type=F32)
    kr = _rope_lanes(kr, cos_b, sin_b)[0]
    for hd in range(B_HEADS):
        kb_ref[:, hd * HEAD_SLOT:(hd + 1) * HEAD_SLOT] = (
            k_nope[:, hd * HEAD_SLOT:(hd + 1) * HEAD_SLOT] + kr).astype(BF16)
    _store_transposed(vb_t_ref, 0, jnp.dot(w_vup_t_ref[...], ckvn.T.astype(BF16), preferred_element_type=F32))


def _attn_kernel(*refs, moba, own_blocks, heads_per_step):
    if moba:
        q_t_ref, k_ref, v_t_ref, kmean_ref, o_t_ref = refs[:5]
        s_scr, smax_scr, p_scr, alpha_scr, m_scr, acc_scr, bias_scr, sbias_scr = refs[5:]
    else:
        q_t_ref, k_ref, v_t_ref, o_t_ref = refs[:4]
        s_scr, smax_scr, p_scr, alpha_scr, m_scr, acc_scr = refs[4:]
    own_lo = own_blocks * pl.program_id(2)
    q_tile = own_blocks * MOBA_BLOCK
    heads = heads_per_step
    vdim = v_t_ref.shape[1] // heads
    qry_pos = lax.broadcasted_iota(jnp.int32, (1, q_tile), 1)

    def key_lanes(hd):
        g = hd // 2 if moba else hd
        return slice(g * LANES, (g + 1) * LANES)

    def q_head(hd, first=0):
        rows = slice(hd * HEAD_SLOT, (hd + 1) * HEAD_SLOT)
        return jnp.concatenate([q_t_ref[u, rows, :] for u in range(first, own_blocks)], axis=1)

    if moba:
        nblk = kmean_ref.shape[0]
        blk = lax.broadcasted_iota(jnp.int32, (nblk, q_tile), 0).astype(F32)
        own = (own_lo + qry_pos // MOBA_BLOCK).astype(F32)
        past = blk < own
        for hd in range(heads):
            gate = jnp.dot(kmean_ref[:, key_lanes(hd)].astype(BF16), q_head(hd),
                           preferred_element_type=F32)
            gate = jnp.where(past, gate, -jnp.inf)
            bias = jnp.full(gate.shape, MASK_VALUE, F32)
            for _ in range(MOBA_TOPK):
                best = jnp.max(gate, axis=0, keepdims=True)
                first = jnp.min(jnp.where(gate == best, blk, float(nblk)), axis=0, keepdims=True)
                pick = blk == first
                bias = jnp.where(pick, 0.0, bias)
                gate = jnp.where(pick, -jnp.inf, gate)
            bias_scr[hd] = jnp.where(past, bias, MASK_VALUE)


    def qk_stage(j, first=0, own=False):
        start = pl.multiple_of(j * K_TILE, K_TILE)
        k_blk = k_ref[pl.ds(start, K_TILE), :]
        cols = slice(first * MOBA_BLOCK, q_tile)
        for hd in range(heads):
            s = jnp.dot(k_blk[:, key_lanes(hd)], q_head(hd, first), preferred_element_type=F32)
            if own:
                key_pos = lax.broadcasted_iota(jnp.int32, s.shape, 0)
                s = jnp.where(key_pos <= qry_pos[:, :s.shape[1]], s, MASK_VALUE)
            s_scr[hd, :, cols] = s
            smax = jnp.max(s, axis=0, keepdims=True)
            if moba:
                bias = bias_scr[hd, pl.ds(j, 1), cols]
                if own:
                    bias = jnp.where(qry_pos[:, :s.shape[1]] >= MOBA_BLOCK, bias, 0.0)
                sbias_scr[hd, :, cols] = bias
                smax = smax + bias
            smax_scr[hd, :, cols] = smax

    ones_rows = jnp.ones((SUM_ROWS, K_TILE), BF16)

    def softmax_stage(first=0):
        cols = slice(first * MOBA_BLOCK, q_tile)
        for hd in range(heads):
            m_old = m_scr[hd, :, cols]
            m_new = jnp.maximum(m_old, smax_scr[hd, :, cols])
            shift = m_new - sbias_scr[hd, :, cols] if moba else m_new
            p_scr[hd, :, cols] = jnp.exp2(s_scr[hd, :, cols] - shift).astype(BF16)
            alpha_scr[hd, :, cols] = jnp.exp2(m_old - m_new)
            m_scr[hd, :, cols] = m_new

    def pv_stage(j, first=0):
        v_t = v_t_ref[j]
        cols = slice(first * MOBA_BLOCK, q_tile)
        for hd in range(heads):
            v_ext = jnp.concatenate([v_t[hd * vdim:(hd + 1) * vdim], ones_rows], axis=0)
            acc_scr[hd, :, cols] = alpha_scr[hd, :, cols] * acc_scr[hd, :, cols] + jnp.dot(
                v_ext, p_scr[hd, :, cols], preferred_element_type=F32)

    def visited_block(v):
        return jnp.where(v < own_blocks, own_lo + (own_blocks - 1) - v, v - own_blocks)

    m_scr[...] = jnp.full(m_scr.shape, MASK_VALUE, F32)
    acc_scr[...] = jnp.zeros(acc_scr.shape, F32)
    p_scr[:, :, :MOBA_BLOCK] = jnp.zeros(p_scr.shape[:2] + (MOBA_BLOCK,), BF16)
    alpha_scr[:, :, :MOBA_BLOCK] = jnp.ones(alpha_scr.shape[:2] + (MOBA_BLOCK,), F32)
    qk_stage(own_lo + own_blocks - 1, own_blocks - 1, own=True)
    for v in range(own_blocks - 1):
        u = own_blocks - 1 - v
        if v > 0:
            pv_stage(own_lo + u + 1, u + 1)
        softmax_stage(u)
        qk_stage(own_lo + u - 1, u - 1, own=True)

    def body(t, carry):
        pv_stage(visited_block(t - 1))
        softmax_stage()
        qk_stage(t + 1 - own_blocks)
        return carry

    last = own_lo + own_blocks - 1
    lax.fori_loop(own_blocks - 1, last, body, 0)
    pv_stage(visited_block(last - 1))
    softmax_stage()
    pv_stage(visited_block(last))

    for hd in range(heads):
        acc = acc_scr[hd]
        o = (acc[:vdim] / acc[vdim:vdim + 1]).astype(o_t_ref.dtype)
        for u in range(own_blocks):
            o_t_ref[u, hd * vdim:(hd + 1) * vdim, :] = o[:, u * MOBA_BLOCK:(u + 1) * MOBA_BLOCK]


def _post_kernel(x_ref, oa_t_ref, ob_t_ref, gate_ref, w_oa_ref, w_ob_ref, w_out_ref,
                 g_mlp_ref, w_ff1_ref, w_ff2_ref, g_final_ref, out_ref, *, ff_chunk, final_norm):
    d_model = x_ref.shape[1]
    contract_rows = (((0,), (0,)), ((), ()))
    ya, yb = [], []
    for qt in range(ROW_TILE // K_TILE):
        ya.append(lax.dot_general(oa_t_ref[qt], w_oa_ref[...], contract_rows, preferred_element_type=F32))
        yb.append(lax.dot_general(ob_t_ref[qt], w_ob_ref[...], contract_rows, preferred_element_type=F32))
    ya = ya[0] if len(ya) == 1 else jnp.concatenate(ya, axis=0)
    yb = yb[0] if len(yb) == 1 else jnp.concatenate(yb, axis=0)
    mixed = gate_ref[:, :d_model].astype(F32) * ya + gate_ref[:, d_model:].astype(F32) * yb
    x1 = x_ref[...] + jnp.dot(mixed.astype(BF16), w_out_ref[...], preferred_element_type=F32)

    h2 = _rms(x1, g_mlp_ref[...]).astype(BF16)
    mlp = jnp.zeros_like(x1)
    for c in range(w_ff1_ref.shape[1] // ff_chunk):
        u = jnp.maximum(jnp.dot(h2, w_ff1_ref[:, c * ff_chunk:(c + 1) * ff_chunk],
                                preferred_element_type=F32), 0.0)
        mlp = mlp + jnp.dot((u * u).astype(BF16), w_ff2_ref[c * ff_chunk:(c + 1) * ff_chunk, :],
                            preferred_element_type=F32)
    x2 = x1 + mlp
    out_ref[...] = _rms(x2, g_final_ref[...]) if final_norm else x2


def _resident(shape):
    return pl.BlockSpec(shape, lambda *_: (0,) * len(shape), pipeline_mode=pl.Buffered(1))


def _rope_tables(seq, dim):
    half = dim // 2
    inv_freq = np.power(ROPE_THETA, -np.arange(half, dtype=np.float64) / half)
    ang = np.arange(seq, dtype=np.float64)[:, None] * inv_freq[None, :]
    return np.cos(ang).astype(np.float32), np.sin(ang).astype(np.float32)


def _moba_group_columns():
    half = A_HEAD_DIM // 2
    order = []
    for pair in range(A_HEADS // 2):
        h0, h1 = 2 * pair * A_HEAD_DIM, (2 * pair + 1) * A_HEAD_DIM
        for start in (h0, h1, h0 + half, h1 + half):
            order.extend(range(start, start + half))
    return jnp.asarray(order, jnp.int32)


def _mla_slot_columns(nope, rope, xp=jnp):
    r = B_ROPE // 2
    split = LANES // 2 - r
    pad = xp.zeros(nope.shape[:-1] + (HEAD_SLOT - B_QK,), nope.dtype)
    return xp.concatenate([rope[..., :r], nope[..., :split], rope[..., r:], nope[..., split:], pad], axis=-1)


def _params(*semantics):
    return pltpu.CompilerParams(dimension_semantics=semantics, vmem_limit_bytes=VMEM_LIMIT_BYTES)


def _pre_call(x2, seq, g_mix, w_in, b_gate, g_cq, w_q_up, g_ckv, w_kv_up):
    rows, d_model = x2.shape
    a_width = A_HEADS * A_HEAD_DIM
    o_q, o_k, o_v = 0, a_width, 2 * a_width
    o_cq = 3 * a_width
    o_ckv = o_cq + B_Q_RANK
    o_kr = o_ckv + B_KV_RANK
    o_gate = o_kr + B_ROPE
    w_in = w_in.astype(BF16)
    cols = _moba_group_columns()
    w_q_t = w_in[:, o_q:o_k][:, cols].T
    w_k = w_in[:, o_k:o_v][:, cols]
    w_v_t = w_in[:, o_v:o_cq].T
    w_cq_t = w_in[:, o_cq:o_ckv].T
    w_ckv = w_in[:, o_ckv:o_kr]
    w_kr = _mla_slot_columns(jnp.zeros((d_model, B_NOPE), BF16), w_in[:, o_kr:o_gate])
    w_gate = w_in[:, o_gate:]
    w_q = w_q_up.astype(BF16).reshape(B_Q_RANK, B_HEADS, B_QK)
    w_qup_t = _mla_slot_columns(w_q[..., :B_NOPE], w_q[..., B_NOPE:]).reshape(B_Q_RANK, B_HEADS * HEAD_SLOT).T
    w_kv = w_kv_up.astype(BF16).reshape(B_KV_RANK, B_HEADS, B_NOPE + B_V)
    w_kup = _mla_slot_columns(w_kv[..., :B_NOPE], jnp.zeros((B_KV_RANK, B_HEADS, B_ROPE), BF16)).reshape(
        B_KV_RANK, B_HEADS * HEAD_SLOT)
    w_vup_t = w_kv[:, :, B_NOPE:].reshape(B_KV_RANK, B_HEADS * B_V).T

    cos_a, sin_a = _rope_tables(seq, A_HEAD_DIM)
    cos_a = np.tile(cos_a, (1, LANES // (A_HEAD_DIM // 2)))
    sin_a = np.concatenate([-sin_a, -sin_a, sin_a, sin_a], axis=1)
    cos_b, sin_b = _rope_tables(seq, B_ROPE)
    cos_b = _mla_slot_columns(np.ones((seq, B_NOPE), np.float32), np.concatenate([cos_b, cos_b], axis=1), np)
    sin_b = _mla_slot_columns(np.zeros((seq, B_NOPE), np.float32), np.concatenate([-sin_b, sin_b], axis=1), np)
    tables_t = [np.ascontiguousarray(t.T) for t in (cos_a, sin_a, cos_b, sin_b)]

    n_tiles = rows // ROW_TILE
    seq_tiles = seq // ROW_TILE
    row_spec = lambda w: pl.BlockSpec((ROW_TILE, w), lambda r: (r, 0))
    table_spec = pl.BlockSpec((ROW_TILE, LANES), lambda r: (r % seq_tiles, 0))
    table_t_spec = pl.BlockSpec((LANES, ROW_TILE), lambda r: (0, r % seq_tiles))
    t_spec = lambda h, width: pl.BlockSpec((ROW_TILE // width, h, width), lambda r: (r, 0, 0))
    t_shape = lambda h, width: jax.ShapeDtypeStruct((rows // width, h, width), BF16)
    vec = lambda a: a.reshape(1, -1).astype(F32)
    g_cq_col = jnp.broadcast_to(g_cq.astype(F32)[:, None], (B_Q_RANK, ROW_TILE))
    operands = [
        (x2, row_spec(d_model)), (vec(g_mix), _resident((1, d_model))),
        (cos_a, table_spec), (sin_a, table_spec), (cos_b, table_spec), (sin_b, table_spec),
        *[(t, table_t_spec) for t in tables_t],
        (w_q_t, _resident(w_q_t.shape)), (w_k, _resident(w_k.shape)), (w_v_t, _resident(w_v_t.shape)),
        (w_cq_t, _resident(w_cq_t.shape)), (w_ckv, _resident(w_ckv.shape)),
        (w_kr, _resident(w_kr.shape)), (w_gate, _resident(w_gate.shape)),
        (vec(b_gate), _resident((1, 2 * d_model))),
        (g_cq_col, _resident(g_cq_col.shape)), (w_qup_t, _resident(w_qup_t.shape)),
        (vec(g_ckv), _resident((1, B_KV_RANK))), (w_kup, _resident(w_kup.shape)),
        (w_vup_t, _resident(w_vup_t.shape)),
    ]
    out_shape = [
        t_shape(A_HEADS * HEAD_SLOT, K_TILE),
        jax.ShapeDtypeStruct((rows, a_width), BF16),
        t_shape(a_width, K_TILE),
        jax.ShapeDtypeStruct((n_tiles, ROW_TILE // MOBA_BLOCK, a_width), F32),
        jax.ShapeDtypeStruct((rows, 2 * d_model), BF16),
        t_shape(B_HEADS * HEAD_SLOT, K_TILE),
        jax.ShapeDtypeStruct((rows, B_HEADS * HEAD_SLOT), BF16),
        t_shape(B_HEADS * B_V, K_TILE),
    ]
    out_specs = [
        t_spec(A_HEADS * HEAD_SLOT, K_TILE), row_spec(a_width), t_spec(a_width, K_TILE),
        pl.BlockSpec((1, ROW_TILE // MOBA_BLOCK, a_width), lambda r: (r, 0, 0)),
        row_spec(2 * d_model), t_spec(B_HEADS * HEAD_SLOT, K_TILE), row_spec(B_HEADS * HEAD_SLOT),
        t_spec(B_HEADS * B_V, K_TILE),
    ]
    return pl.pallas_call(
        _pre_kernel,
        grid=(n_tiles,),
        in_specs=[s for _, s in operands],
        out_specs=out_specs,
        out_shape=out_shape,
        compiler_params=_params("parallel"),
        name="pre_proj",
    )(*[a for a, _ in operands])


def _attn_call(q_t, k, v_t, kmean, *, batch, seq, moba):
    own_blocks, heads_per_step = MOBA_STEP if moba else MLA_STEP
    q_tile = own_blocks * MOBA_BLOCK
    nblk = seq // K_TILE
    n_q = seq // q_tile
    heads = q_t.shape[1] // HEAD_SLOT
    groups = heads // heads_per_step
    vdim = v_t.shape[2] // heads
    key_lanes = k.shape[2] // groups
    kv_mode = None if heads_per_step <= 4 else pl.Buffered(1)
    in_specs = [
        pl.BlockSpec((own_blocks, heads_per_step * HEAD_SLOT, K_TILE), lambda b, g, i: (b * n_q + i, g, 0)),
        pl.BlockSpec((None, seq, key_lanes), lambda b, g, i: (b, 0, g), pipeline_mode=kv_mode),
        pl.BlockSpec((None, nblk, heads_per_step * vdim, K_TILE), lambda b, g, i: (b, 0, g, 0),
                     pipeline_mode=kv_mode),
    ]
    operands = [q_t, k, v_t]
    stat = pltpu.VMEM((heads_per_step, 1, q_tile), F32)
    scratch = [
        pltpu.VMEM((heads_per_step, K_TILE, q_tile), F32),
        stat,
        pltpu.VMEM((heads_per_step, K_TILE, q_tile), BF16),
        stat,
        stat,
        pltpu.VMEM((heads_per_step, vdim + SUM_ROWS, q_tile), F32),
    ]
    if moba:
        in_specs.append(pl.BlockSpec((None, nblk, key_lanes), lambda b, g, i: (b, 0, g)))
        operands.append(kmean)
        scratch.append(pltpu.VMEM((heads_per_step, nblk, q_tile), F32))
        scratch.append(stat)
    return pl.pallas_call(
        functools.partial(_attn_kernel, moba=moba, own_blocks=own_blocks, heads_per_step=heads_per_step),
        grid=(batch, groups, n_q),
        in_specs=in_specs,
        out_specs=pl.BlockSpec((own_blocks, heads_per_step * vdim, K_TILE), lambda b, g, i: (b * n_q + i, g, 0)),
        out_shape=jax.ShapeDtypeStruct((batch * nblk, heads * vdim, K_TILE), BF16),
        scratch_shapes=scratch,
        compiler_params=_params("parallel", "parallel", "arbitrary"),
        name="moba_attn" if moba else "mla_attn",
    )(*operands)


def _post_call(x2, oa_t, ob_t, gates, w_o_a, w_o_b, w_out, g_mlp, w_ff1, w_ff2, g_final, final_norm):
    rows, d_model = x2.shape
    row_spec = lambda w: pl.BlockSpec((ROW_TILE, w), lambda r: (r, 0))
    t_spec = lambda a: pl.BlockSpec((ROW_TILE // K_TILE,) + a.shape[1:], lambda r: (r, 0, 0))
    vec = lambda a: a.reshape(1, -1).astype(F32)
    weights = [w.astype(BF16) for w in (w_o_a, w_o_b, w_out)]
    operands = [(x2, row_spec(d_model)), (oa_t, t_spec(oa_t)), (ob_t, t_spec(ob_t)),
                (gates, row_spec(2 * d_model))]
    operands += [(w, _resident(w.shape)) for w in weights]
    operands += [(vec(g_mlp), _resident((1, d_model))),
                 (w_ff1.astype(BF16), _resident(w_ff1.shape)), (w_ff2.astype(BF16), _resident(w_ff2.shape)),
                 (vec(g_final), _resident((1, d_model)))]
    return pl.pallas_call(
        functools.partial(_post_kernel, ff_chunk=FF_CHUNK, final_norm=final_norm),
        grid=(rows // ROW_TILE,),
        in_specs=[s for _, s in operands],
        out_specs=row_spec(d_model),
        out_shape=jax.ShapeDtypeStruct((rows, d_model), F32),
        compiler_params=_params("parallel"),
        name="post_mlp",
    )(*[a for a, _ in operands])


def kernel(x, g_mix, w_in, b_gate, g_cq, w_q_up, g_ckv, w_kv_up, w_o_a, w_o_b, w_out, g_mlp, w_ff1,
           w_ff2, g_final):
    batch, seq, d_model = x.shape
    depth = w_in.shape[0]
    assert seq % ROW_TILE == 0 and ROW_TILE % MOBA_BLOCK == 0
    for heads, (own_blocks, heads_per_step) in ((A_HEADS, MOBA_STEP), (B_HEADS, MLA_STEP)):
        assert seq % (own_blocks * MOBA_BLOCK) == 0 and own_blocks >= 2
        assert heads % heads_per_step == 0
    assert MOBA_STEP[1] % 2 == 0
    nblk = seq // K_TILE
    x2 = x.reshape(batch * seq, d_model)
    for l in range(depth):
        qa_t, ka, va_t, kmean, gates, qb_t, kb, vb_t = _pre_call(
            x2, seq, g_mix[l], w_in[l], b_gate[l], g_cq[l], w_q_up[l], g_ckv[l], w_kv_up[l])
        oa_t = _attn_call(qa_t, ka.reshape(batch, seq, -1), va_t.reshape(batch, nblk, -1, K_TILE),
                          kmean.reshape(batch, nblk, -1), batch=batch, seq=seq, moba=True)
        ob_t = _attn_call(qb_t, kb.reshape(batch, seq, -1), vb_t.reshape(batch, nblk, -1, K_TILE),
                          None, batch=batch, seq=seq, moba=False)
        x2 = _post_call(x2, oa_t, ob_t, gates, w_o_a[l], w_o_b[l], w_out[l], g_mlp[l], w_ff1[l],
                        w_ff2[l], g_final, final_norm=(l == depth - 1))
    return x2.reshape(batch, seq, d_model)
```
